```python
import math
import jax, jax.numpy as jnp
from jax import lax
import numpy as np

D_MODEL = 1024
BATCH = 8
SEQ = 4096
DEPTH = 1
DEC_BATCH = 32
DEC_SEQ = 16
PAST_LEN = 2048

CHUNK = 64
Q_BLOCK = 128
H_G = 4
DK_G = 64
DV_G = 128
GATE_RANK = 16
GATE_TAU = 16.0
H_D = 4
DH_D = 64
DV_D = 2 * DH_D
D_FF = 2816
EPS = 1e-6
IN_SPLITS = (H_G * DK_G, H_G * DK_G, H_G * DV_G, GATE_RANK, H_G * DV_G,
             H_D * 2 * DH_D, H_D * 2 * DH_D, H_D * DV_D)
D_IN = H_G * DK_G * 2 + H_G * DV_G * 2 + GATE_RANK + H_D * 2 * DH_D * 2 + H_D * DV_D
MIX_WIDTH = H_G * DV_G + H_D * DV_D

kernel_name = "hybrid_gla_diffattn_streaming_step"


def rmsnorm(x, g):
    xf = x.astype(jnp.float32)
    r = lax.rsqrt(jnp.mean(xf * xf, axis=-1, keepdims=True) + EPS)
    return (xf * r * g.astype(jnp.float32)).astype(x.dtype)


def ffn_half(x, pre_g, post_g, w_gate, w_up, w_down):
    h = rmsnorm(x, pre_g)
    f = (jax.nn.silu(h @ w_gate) * (h @ w_up)) @ w_down
    return x + 0.5 * rmsnorm(f, post_g)


def split_in(z):
    offs, acc = [], 0
    for s in IN_SPLITS[:-1]:
        acc += s
        offs.append(acc)
    return jnp.split(z, offs, axis=-1)


def project(h, w_in, w_a2, b_a):
    B, T, _ = h.shape
    gq, gk, gv, gr, gg, dq, dk, dv = split_in(h @ w_in)
    to_bhtd = lambda t, d: t.reshape(B, T, -1, d).transpose(0, 2, 1, 3)
    gq = to_bhtd(gq, DK_G) * (DK_G ** -0.5)
    gk = to_bhtd(gk, DK_G)
    gv = to_bhtd(gv, DV_G)
    la = jax.nn.log_sigmoid((gr @ w_a2 + b_a).astype(jnp.float32)) / GATE_TAU
    la = to_bhtd(la, DK_G)
    dq = dq.reshape(B, T, H_D, 2, DH_D)
    dk = dk.reshape(B, T, H_D, 2, DH_D)
    dv = dv.reshape(B, T, H_D, DV_D)
    return gq, gk, gv, la, gg, dq, dk, dv


def gla_block(S0, q, k, v, la):
    L = q.shape[2]
    S0 = S0.astype(jnp.float32)
    b = jnp.cumsum(la, axis=2)
    causal = jnp.tril(jnp.ones((L, L), dtype=bool))
    decay = jnp.exp(jnp.where(causal[:, :, None],
                              b[:, :, :, None, :] - b[:, :, None, :, :], -jnp.inf))
    A = jnp.einsum('bhtd,bhsd,bhtsd->bhts', q, k, decay)
    o = jnp.einsum('bhts,bhsv->bhtv', A, v) + jnp.einsum('bhtd,bhdv->bhtv', q * jnp.exp(b), S0)
    bL = b[:, :, -1:, :]
    S = jnp.exp(bL[:, :, 0, :])[..., None] * S0 + jnp.einsum('bhsd,bhsv->bhdv', k * jnp.exp(bL - b), v)
    return S, o.astype(v.dtype)


def gla_prompt(q, k, v, la):
    B, H, S, _ = q.shape
    nc = S // CHUNK
    to_chunks = lambda t: t.reshape(B, H, nc, CHUNK, t.shape[-1]).transpose(2, 0, 1, 3, 4)
    S0 = jnp.zeros((B, H, DK_G, DV_G), jnp.float32)
    Sf, o = lax.scan(lambda St, xs: gla_block(St, *xs), S0,
                     (to_chunks(q), to_chunks(k), to_chunks(v), to_chunks(la)))
    o = o.transpose(1, 2, 0, 3, 4).reshape(B, H, S, DV_G)
    return o, Sf


def diff_attend(q, k, v, mask, lam, g, lam_init):
    s = jnp.einsum('bqhcd,bkhcd->bhcqk', q, k).astype(jnp.float32) * (DH_D ** -0.5)
    if mask is not None:
        s = jnp.where(mask, s, -jnp.inf)
    a = jax.nn.softmax(s, axis=-1)
    w = a[:, :, 0] - lam * a[:, :, 1]
    o = jnp.einsum('bhqk,bkhv->bqhv', w.astype(v.dtype), v)
    return rmsnorm(o, g) * (1.0 - lam_init)


def diff_attn_prompt(q, k, v, lam, g, lam_init):
    B, S = q.shape[:2]
    nb = S // Q_BLOCK
    qb = q.reshape(B, nb, Q_BLOCK, H_D, 2, DH_D).transpose(1, 0, 2, 3, 4, 5)
    key_chunk = jnp.arange(S) // CHUNK

    def one(args):
        qblk, i = args
        q_chunk = (i * Q_BLOCK + jnp.arange(Q_BLOCK)) // CHUNK
        mask = key_chunk[None, :] <= q_chunk[:, None]
        return diff_attend(qblk, k, v, mask, lam, g, lam_init)

    o = lax.map(one, (qb, jnp.arange(nb)))
    return o.transpose(1, 0, 2, 3, 4).reshape(B, S, H_D, DV_D)


def merge(o_gla, gg, gla_g, o_diff, w_out):
    B, H, T, _ = o_gla.shape
    g_out = rmsnorm(o_gla.transpose(0, 2, 1, 3), gla_g).reshape(B, T, H_G * DV_G) * jax.nn.silu(gg)
    d_out = o_diff.reshape(B, T, H_D * DV_D)
    return jnp.concatenate([g_out, d_out], axis=-1) @ w_out


def setup_inputs(seed: int = 0) -> dict:
    key = jax.random.key(seed)
    ks = iter(jax.random.split(key, 40))
    nrm = lambda shape, scale: jax.random.normal(next(ks), shape, jnp.float32) * scale
    gain = lambda n: 1.0 + nrm((DEPTH, n), 0.05)
    return {
        "x_prompt": nrm((BATCH, SEQ, D_MODEL), 1.0),
        "x_sample": nrm((DEC_BATCH, DEC_SEQ, D_MODEL), 1.0),
        "state_gla": nrm((DEPTH, DEC_BATCH, H_G, DK_G, DV_G), 0.1),
        "cache_diff_k": nrm((DEPTH, DEC_BATCH, PAST_LEN, H_D, 2 * DH_D), 1.0),
        "cache_diff_v": nrm((DEPTH, DEC_BATCH, PAST_LEN, H_D, DV_D), 1.0),
        "w_in": nrm((DEPTH, D_MODEL, D_IN), D_MODEL ** -0.5),
        "w_gate_a2": nrm((DEPTH, GATE_RANK, H_G * DK_G), GATE_RANK ** -0.5),
        "b_gate_a": nrm((DEPTH, H_G * DK_G), 0.1),
        "gla_norm_g": gain(DV_G),
        "lambda_q1": nrm((DEPTH, DH_D), 0.1),
        "lambda_k1": nrm((DEPTH, DH_D), 0.1),
        "lambda_q2": nrm((DEPTH, DH_D), 0.1),
        "lambda_k2": nrm((DEPTH, DH_D), 0.1),
        "diff_norm_g": gain(DV_D),
        "w_out": nrm((DEPTH, MIX_WIDTH, D_MODEL), MIX_WIDTH ** -0.5),
        "mix_pre_g": gain(D_MODEL),
        "mix_post_g": gain(D_MODEL),
        "ffn1_pre_g": gain(D_MODEL),
        "ffn1_post_g": gain(D_MODEL),
        "ffn1_w_gate": nrm((DEPTH, D_MODEL, D_FF), D_MODEL ** -0.5),
        "ffn1_w_up": nrm((DEPTH, D_MODEL, D_FF), D_MODEL ** -0.5),
        "ffn1_w_down": nrm((DEPTH, D_FF, D_MODEL), D_FF ** -0.5),
        "ffn2_pre_g": gain(D_MODEL),
        "ffn2_post_g": gain(D_MODEL),
        "ffn2_w_gate": nrm((DEPTH, D_MODEL, D_FF), D_MODEL ** -0.5),
        "ffn2_w_up": nrm((DEPTH, D_MODEL, D_FF), D_MODEL ** -0.5),
        "ffn2_w_down": nrm((DEPTH, D_FF, D_MODEL), D_FF ** -0.5),
    }


def reference(x_prompt, x_sample, state_gla, cache_diff_k, cache_diff_v,
              w_in, w_gate_a2, b_gate_a, gla_norm_g,
              lambda_q1, lambda_k1, lambda_q2, lambda_k2, diff_norm_g, w_out,
              mix_pre_g, mix_post_g,
              ffn1_pre_g, ffn1_post_g, ffn1_w_gate, ffn1_w_up, ffn1_w_down,
              ffn2_pre_g, ffn2_post_g, ffn2_w_gate, ffn2_w_up, ffn2_w_down):
    xp, xs = x_prompt, x_sample
    Bs, T = xs.shape[:2]
    P = cache_diff_k.shape[2]
    sg_p, k_p, v_p, sg_s, k_s, v_s = [], [], [], [], [], []
    for l in range(DEPTH):
        lam_init = 0.8 - 0.6 * math.exp(-0.3 * l)
        lam = (jnp.exp(jnp.sum(lambda_q1[l] * lambda_k1[l]).astype(jnp.float32))
               - jnp.exp(jnp.sum(lambda_q2[l] * lambda_k2[l]).astype(jnp.float32)) + lam_init)
        ffn1 = (ffn1_pre_g[l], ffn1_post_g[l], ffn1_w_gate[l], ffn1_w_up[l], ffn1_w_down[l])
        ffn2 = (ffn2_pre_g[l], ffn2_post_g[l], ffn2_w_gate[l], ffn2_w_up[l], ffn2_w_down[l])

        xp = ffn_half(xp, *ffn1)
        gq, gk, gv, la, gg, dq, dk, dv = project(rmsnorm(xp, mix_pre_g[l]), w_in[l], w_gate_a2[l], b_gate_a[l])
        o_gla, S_p = gla_prompt(gq, gk, gv, la)
        o_diff = diff_attn_prompt(dq, dk, dv, lam, diff_norm_g[l], lam_init)
        xp = xp + rmsnorm(merge(o_gla, gg, gla_norm_g[l], o_diff, w_out[l]), mix_post_g[l])
        xp = ffn_half(xp, *ffn2)
        sg_p.append(S_p.astype(x_prompt.dtype))
        k_p.append(dk.reshape(dk.shape[0], dk.shape[1], H_D, 2 * DH_D))
        v_p.append(dv)

        xs = ffn_half(xs, *ffn1)
        gq, gk, gv, la, gg, dq, dk, dv = project(rmsnorm(xs, mix_pre_g[l]), w_in[l], w_gate_a2[l], b_gate_a[l])
        S_s, o_gla = gla_block(state_gla[l], gq, gk, gv, la)
        keys = jnp.concatenate([cache_diff_k[l].reshape(Bs, P, H_D, 2, DH_D), dk], axis=1)
        vals = jnp.concatenate([cache_diff_v[l], dv], axis=1)
        o_diff = diff_attend(dq, keys, vals, None, lam, diff_norm_g[l], lam_init)
        xs = xs + rmsnorm(merge(o_gla, gg, gla_norm_g[l], o_diff, w_out[l]), mix_post_g[l])
        xs = ffn_half(xs, *ffn2)
        sg_s.append(S_s.astype(state_gla.dtype))
        k_s.append(dk.reshape(Bs, T, H_D, 2 * DH_D))
        v_s.append(dv)

    return (xp, xs, jnp.stack(sg_p), jnp.stack(k_p), jnp.stack(v_p),
            jnp.stack(sg_s), jnp.stack(k_s), jnp.stack(v_s))
```

```python
import functools
import math

import jax
import jax.numpy as jnp
from jax import lax
from jax.experimental import pallas as pl
from jax.experimental.pallas import tpu as pltpu

F32 = jnp.float32
BF16 = jnp.bfloat16

D_MODEL = 1024
D_FF = 2816
CHUNK = 64
H_G, DK_G, DV_G = 4, 64, 128
GATE_RANK = 16
GATE_TAU = 16.0
H_D, DH_D, DV_D = 4, 64, 128
EPS = 1e-6

GQ = H_G * DK_G
GV = H_G * DV_G
DQ = H_D * 2 * DH_D
DV = H_D * DV_D

LANES = 128
MXU_N = 256
TOKEN_TILE = 512
FF_TILE = MXU_N
ATT_TILE = 256
GLA_ROWS = 256
VMEM_LIMIT = 56 * 1024 * 1024


def _dot(a, b):
    return jnp.dot(a, b, preferred_element_type=F32)


def _dot_nt(a, b):
    return lax.dot_general(a, b, (((1,), (1,)), ((), ())), preferred_element_type=F32)


def _rms(x, g):
    r = lax.rsqrt(jnp.mean(x * x, axis=-1, keepdims=True) + EPS)
    return x * r * g


def _silu(x):
    return x * jax.nn.sigmoid(x)


def _const_spec(shape):
    nd = len(shape)
    return pl.BlockSpec(shape, lambda *_: (0,) * nd, pipeline_mode=pl.Buffered(1))


def _swiglu_half(x, pre_ref, post_ref, wg_ref, wu_ref, wd_ref, h_ref, act_ref):
    h_ref[...] = _rms(x, pre_ref[...]).astype(BF16)
    for j in range(D_FF // FF_TILE):
        cols = slice(j * FF_TILE, (j + 1) * FF_TILE)
        h = h_ref[...]
        gate = _dot(h, wg_ref[:, cols])
        up = _dot(h, wu_ref[:, cols])
        act_ref[:, cols] = (_silu(gate) * up).astype(BF16)
    f = _dot(act_ref[...], wd_ref[...])
    return x + 0.5 * _rms(f, post_ref[...])


def _ffn_kernel(x_ref, pre_ref, post_ref, wg_ref, wu_ref, wd_ref, o_ref, h_ref, act_ref):
    o_ref[...] = _swiglu_half(x_ref[...], pre_ref, post_ref, wg_ref, wu_ref, wd_ref, h_ref, act_ref)


def _merge_ffn_kernel(x_ref, g_ref, d_ref, wog_ref, wod_ref, mixg_ref,
                      pre_ref, post_ref, wg_ref, wu_ref, wd_ref, o_ref, h_ref, act_ref, x2_ref):
    y = _dot(g_ref[...], wog_ref[...]) + _dot(d_ref[...], wod_ref[...])
    x2_ref[...] = x_ref[...] + _rms(y, mixg_ref[...])
    o_ref[...] = _swiglu_half(x2_ref[...], pre_ref, post_ref, wg_ref, wu_ref, wd_ref, h_ref, act_ref)


def _ffn_call(x, pre_g, post_g, wg, wu, wd, merge=None):
    m = x.shape[0]
    tm = TOKEN_TILE
    row = lambda w: pl.BlockSpec((tm, w), lambda i: (i, 0))
    ffn_specs = [_const_spec((1, D_MODEL)), _const_spec((1, D_MODEL)),
                 _const_spec((D_MODEL, D_FF)), _const_spec((D_MODEL, D_FF)), _const_spec((D_FF, D_MODEL))]
    scratch = [pltpu.VMEM((tm, D_MODEL), BF16), pltpu.VMEM((tm, D_FF), BF16)]
    if merge is None:
        kern, ins = _ffn_kernel, (x, pre_g, post_g, wg, wu, wd)
        in_specs = [row(D_MODEL)] + ffn_specs
    else:
        g_out, d_out, wog, wod, mix_g = merge
        kern, ins = _merge_ffn_kernel, (x, g_out, d_out, wog, wod, mix_g, pre_g, post_g, wg, wu, wd)
        in_specs = ([row(D_MODEL), row(GV), row(DV), _const_spec((GV, D_MODEL)),
                     _const_spec((DV, D_MODEL)), _const_spec((1, D_MODEL))] + ffn_specs)
        scratch = scratch + [pltpu.VMEM((tm, D_MODEL), F32)]
    return pl.pallas_call(
        kern,
        grid=(m // tm,),
        in_specs=in_specs,
        out_specs=row(D_MODEL),
        out_shape=jax.ShapeDtypeStruct((m, D_MODEL), F32),
        scratch_shapes=scratch,
        compiler_params=pltpu.CompilerParams(dimension_semantics=("arbitrary",),
                                             vmem_limit_bytes=VMEM_LIMIT),
        name="merge_ffn" if merge is not None else "ffn",
    )(*ins)


def _inproj_kernel(x_ref, mixpre_ref, wgla_ref, wgr_ref, wa2_ref, ba_ref, wdq_ref, wdk_ref, wdv_ref,
                   gq_ref, gk_ref, gv_ref, gg_ref, la_ref, dq_ref, dkf_ref, dkb_ref, dvf_ref,
                   *maybe_dvt_ref, transposed):
    h = _rms(x_ref[...], mixpre_ref[...]).astype(BF16)
    gq_ref[...] = (_dot(h, wgla_ref[:, 0:GQ]) * (DK_G ** -0.5)).astype(BF16)
    gk_ref[...] = _dot(h, wgla_ref[:, GQ:2 * GQ]).astype(BF16)
    gv_ref[...] = _dot(h, wgla_ref[:, 2 * GQ:2 * GQ + GV]).astype(BF16)
    gg_ref[...] = _dot(h, wgla_ref[:, 2 * GQ + GV:2 * GQ + 2 * GV]).astype(BF16)
    gr = _dot(h, wgr_ref[...]).astype(BF16)
    a = _dot(gr, wa2_ref[...]) + ba_ref[...]
    la_ref[...] = (jnp.minimum(a, 0.0) - jnp.log1p(jnp.exp(-jnp.abs(a)))) / GATE_TAU
    dq = _dot(h, wdq_ref[...]) * (DH_D ** -0.5)
    dk = _dot(h, wdk_ref[...])
    dv = _dot(h, wdv_ref[...])
    dkf_ref[...] = dk
    dkb_ref[...] = dk.astype(BF16)
    dvf_ref[...] = dv
    if transposed:
        (dvt_ref,) = maybe_dvt_ref
        dq_ref[0] = dq.T.astype(BF16)
        for s in range(TOKEN_TILE // ATT_TILE):
            dvt_ref[0, s] = dv[s * ATT_TILE:(s + 1) * ATT_TILE, :].T.astype(BF16)
    else:
        dq_ref[...] = dq.astype(BF16)


def _inproj_call(x, mixpre_g, w, seq_len, transposed):
    m = x.shape[0]
    tm = TOKEN_TILE
    nb = m // seq_len
    row = lambda wd: pl.BlockSpec((tm, wd), lambda i: (i, 0))
    out_shape = [jax.ShapeDtypeStruct((m, GQ), BF16), jax.ShapeDtypeStruct((m, GQ), BF16),
                 jax.ShapeDtypeStruct((m, GV), BF16), jax.ShapeDtypeStruct((m, GV), BF16),
                 jax.ShapeDtypeStruct((m, GQ), F32)]
    out_specs = [row(GQ), row(GQ), row(GV), row(GV), row(GQ)]
    if transposed:
        tpb = seq_len // tm
        out_shape.append(jax.ShapeDtypeStruct((nb, DQ, seq_len), BF16))
        out_specs.append(pl.BlockSpec((1, DQ, tm), lambda i: (i // tpb, 0, i % tpb)))
    else:
        out_shape.append(jax.ShapeDtypeStruct((m, DQ), BF16))
        out_specs.append(row(DQ))
    out_shape += [jax.ShapeDtypeStruct((m, DQ), F32), jax.ShapeDtypeStruct((m, DQ), BF16),
                  jax.ShapeDtypeStruct((m, DV), F32)]
    out_specs += [row(DQ), row(DQ), row(DV)]
    if transposed:
        spb = tm // ATT_TILE
        out_shape.append(jax.ShapeDtypeStruct((nb, seq_len // ATT_TILE, DV, ATT_TILE), BF16))
        out_specs.append(pl.BlockSpec((1, spb, DV, ATT_TILE), lambda i: (i // tpb, i % tpb, 0, 0)))
    in_specs = [row(D_MODEL), _const_spec((1, D_MODEL)), _const_spec((D_MODEL, 2 * GQ + 2 * GV)),
                _const_spec((D_MODEL, LANES)), _const_spec((LANES, GQ)), _const_spec((1, GQ)),
                _const_spec((D_MODEL, DQ)), _const_spec((D_MODEL, DQ)), _const_spec((D_MODEL, DV))]
    return pl.pallas_call(
        functools.partial(_inproj_kernel, transposed=transposed),
        grid=(m // tm,),
        in_specs=in_specs,
        out_specs=out_specs,
        out_shape=out_shape,
        compiler_params=pltpu.CompilerParams(dimension_semantics=("arbitrary",),
                                             vmem_limit_bytes=VMEM_LIMIT),
        name="inproj_t" if transposed else "inproj",
    )(x, mixpre_g, w["gla"], w["gr"], w["a2"], w["ba"], w["dq"], w["dk"], w["dv"])


def _gla_kernel(q_ref, k_ref, v_ref, la_ref, gg_ref, g_ref, s0_ref, out_ref, sfin_ref, s_ref, *, L, nchunks):
    i = pl.program_id(1)

    @pl.when(i == 0)
    def _():
        s_ref[...] = s0_ref[0]

    lg = int(math.log2(L))
    r4 = lax.broadcasted_iota(jnp.int32, (4 * L, 4 * L), 0)
    c4 = lax.broadcasted_iota(jnp.int32, (4 * L, 4 * L), 1)
    same_head_causal = ((r4 >> lg) == (c4 >> lg)) & ((c4 & (L - 1)) <= (r4 & (L - 1)))
    lane_head = lax.broadcasted_iota(jnp.int32, (L, GQ), 1) >> int(math.log2(DK_G))
    tr = lax.broadcasted_iota(jnp.int32, (L, L), 0)
    tc = lax.broadcasted_iota(jnp.int32, (L, L), 1)
    tril = (tc <= tr).astype(F32)

    def stack_heads(x):
        return jnp.concatenate([jnp.where(lane_head == h, x, 0.0) for h in range(H_G)], axis=0).astype(BF16)

    s = s_ref[...]
    for c in range(nchunks):
        rows = slice(c * L, (c + 1) * L)
        la = la_ref[0, rows, :]
        b = lax.dot_general(tril, la, (((1,), (0,)), ((), ())),
                            precision=lax.Precision.HIGHEST, preferred_element_type=F32)
        b_mid = b[L // 2 - 1:L // 2, :]
        b_end = b[L - 1:L, :]
        q = q_ref[0, rows, :].astype(F32)
        k = k_ref[0, rows, :].astype(F32)
        v = v_ref[0, rows, :]
        q_mid = stack_heads(q * jnp.exp(b - b_mid))
        q_abs = stack_heads(q * jnp.exp(b))
        k_mid = (k * jnp.exp(b_mid - b)).astype(BF16)
        k_end = k * jnp.exp(b_end - b)
        decay = jnp.exp(b_end)

        a = _dot_nt(q_mid, jnp.concatenate([k_mid] * H_G, axis=0))
        a = jnp.where(same_head_causal, a, 0.0).astype(BF16)
        v_stack = jnp.concatenate([v[:, h * DV_G:(h + 1) * DV_G] for h in range(H_G)], axis=0)
        o = _dot(a, v_stack) + _dot(q_abs, s.astype(BF16))

        kt = jnp.concatenate([k_end, jnp.broadcast_to(decay, (8, GQ)),
                              jnp.zeros((LANES - L - 8, GQ), F32)], axis=0).T
        v_pad = jnp.concatenate([v, jnp.zeros((LANES - L, GV), BF16)], axis=0)
        upd = _dot(kt.astype(BF16), v_pad)
        decay_col = kt[:, L:L + 1]
        s = jnp.concatenate(
            [decay_col[h * DK_G:(h + 1) * DK_G] * s[h * DK_G:(h + 1) * DK_G]
             + upd[h * DK_G:(h + 1) * DK_G, h * DV_G:(h + 1) * DV_G] for h in range(H_G)], axis=0)

        for h in range(H_G):
            cols = slice(h * DV_G, (h + 1) * DV_G)
            gate = gg_ref[0, rows, cols].astype(F32)
            out_ref[0, rows, cols] = (_rms(o[h * L:(h + 1) * L], g_ref[...]) * _silu(gate)).astype(BF16)

    s_ref[...] = s

    @pl.when(i == pl.num_programs(1) - 1)
    def _():
        sfin_ref[0] = s


def _gla_call(gq, gk, gv, la, gg, gla_g, s0, L, rows_per_step):
    nb, seq_len = gq.shape[0], gq.shape[1]
    blk = lambda w: pl.BlockSpec((1, rows_per_step, w), lambda b, i: (b, i, 0))
    state = pl.BlockSpec((1, GQ, DV_G), lambda b, i: (b, 0, 0))
    return pl.pallas_call(
        functools.partial(_gla_kernel, L=L, nchunks=rows_per_step // L),
        grid=(nb, seq_len // rows_per_step),
        in_specs=[blk(GQ), blk(GQ), blk(GV), blk(GQ), blk(GV),
                  pl.BlockSpec((1, DV_G), lambda b, i: (0, 0)), state],
        out_specs=[blk(GV), state],
        out_shape=[jax.ShapeDtypeStruct((nb, seq_len, GV), BF16),
                   jax.ShapeDtypeStruct((nb, GQ, DV_G), F32)],
        scratch_shapes=[pltpu.VMEM((GQ, DV_G), F32)],
        compiler_params=pltpu.CompilerParams(dimension_semantics=("arbitrary", "arbitrary"),
                                             vmem_limit_bytes=VMEM_LIMIT),
        name="gla",
    )(gq, gk, gv, la, gg, gla_g, s0)


def _lambda(lq1_ref, lk1_ref, lq2_ref, lk2_ref, lam_init):
    s1 = jnp.sum(lq1_ref[...] * lk1_ref[...], axis=-1, keepdims=True)
    s2 = jnp.sum(lq2_ref[...] * lk2_ref[...], axis=-1, keepdims=True)
    return jnp.exp(s1) - jnp.exp(s2) + lam_init


def _diff_prompt_kernel(qt_ref, k_ref, vt_ref, lq1_ref, lk1_ref, lq2_ref, lk2_ref, g_ref, o_ref, *, lam_init):
    t = ATT_TILE
    i = pl.program_id(2)
    qt = qt_ref[0]
    z = jnp.zeros((DH_D, t), BF16)
    rhs = jnp.concatenate([jnp.concatenate([qt[:DH_D], z], axis=1),
                           jnp.concatenate([z, qt[DH_D:]], axis=1)], axis=0)

    def step(j, carry, masked):
        m, l, acc = carry
        kj = k_ref[0, pl.ds(pl.multiple_of(j * t, t), t), :]
        st = _dot(kj, rhs)
        if masked:
            kr = lax.broadcasted_iota(jnp.int32, (t, 2 * t), 0)
            qc = lax.broadcasted_iota(jnp.int32, (t, 2 * t), 1) & (t - 1)
            lg = int(math.log2(CHUNK))
            st = jnp.where((kr >> lg) <= (qc >> lg), st, -jnp.inf)
        m_new = jnp.maximum(m, jnp.max(st, axis=0, keepdims=True))
        alpha = jnp.exp(m - m_new)
        p = jnp.exp(st - m_new)
        l = alpha * l + jnp.sum(p, axis=0, keepdims=True)
        acc = alpha * acc + _dot(vt_ref[0, j], p.astype(BF16))
        return m_new, l, acc

    init = (jnp.full((1, 2 * t), -jnp.inf, F32), jnp.zeros((1, 2 * t), F32), jnp.zeros((DV_D, 2 * t), F32))
    carry = lax.fori_loop(0, i, functools.partial(step, masked=False), init)
    _, l, acc = step(i, carry, masked=True)

    lam = _lambda(lq1_ref, lk1_ref, lq2_ref, lk2_ref, lam_init)
    ot = acc[:, :t] / l[:, :t] - lam * (acc[:, t:] / l[:, t:])
    r = lax.rsqrt(jnp.mean(ot * ot, axis=0, keepdims=True) + EPS)
    ot = ot * r * g_ref[...] * (1.0 - lam_init)
    o_ref[0] = ot.T.astype(BF16)


def _diff_prompt_call(dqt, dkb, dvt, lams, g_col, lam_init):
    nb, _, seq_len = dqt.shape
    t = ATT_TILE
    vec = pl.BlockSpec((1, DH_D), lambda b, h, i: (0, 0))
    return pl.pallas_call(
        functools.partial(_diff_prompt_kernel, lam_init=lam_init),
        grid=(nb, H_D, seq_len // t),
        in_specs=[pl.BlockSpec((1, 2 * DH_D, t), lambda b, h, i: (b, h, i)),
                  pl.BlockSpec((1, seq_len, 2 * DH_D), lambda b, h, i: (b, 0, h)),
                  pl.BlockSpec((1, seq_len // t, DV_D, t), lambda b, h, i: (b, 0, h, 0)),
                  vec, vec, vec, vec,
                  pl.BlockSpec((DV_D, 1), lambda b, h, i: (0, 0))],
        out_specs=pl.BlockSpec((1, t, DV_D), lambda b, h, i: (b, i, h)),
        out_shape=jax.ShapeDtypeStruct((nb, seq_len, DV), BF16),
        compiler_params=pltpu.CompilerParams(dimension_semantics=("arbitrary", "arbitrary", "arbitrary"),
                                             vmem_limit_bytes=VMEM_LIMIT),
        name="diff_prompt",
    )(dqt, dkb, dvt, *lams, g_col)


def _diff_decode_kernel(q_ref, kc_ref, vc_ref, kn_ref, vn_ref, lq1_ref, lk1_ref, lq2_ref, lk2_ref, g_ref,
                        o_ref, *, lam_init, T):
    lam = _lambda(lq1_ref, lk1_ref, lq2_ref, lk2_ref, lam_init)
    lane = lax.broadcasted_iota(jnp.int32, (T, 2 * DH_D), 1)
    new_col = lax.broadcasted_iota(jnp.int32, (2 * T, LANES), 1)
    pad = jnp.zeros((LANES - T, 2 * DH_D), BF16)
    for h in range(H_D):
        cols = slice(h * 2 * DH_D, (h + 1) * 2 * DH_D)
        q = q_ref[0, :, cols].astype(F32)
        qs = jnp.concatenate([jnp.where(lane < DH_D, q, 0.0), jnp.where(lane >= DH_D, q, 0.0)],
                             axis=0).astype(BF16)
        kc = kc_ref[0, :, cols].astype(BF16)
        vc = vc_ref[0, :, cols].astype(BF16)
        kn = jnp.concatenate([kn_ref[0, :, cols].astype(BF16), pad], axis=0)
        vn = jnp.concatenate([vn_ref[0, :, cols].astype(BF16), pad], axis=0)
        sc = _dot_nt(qs, kc)
        sn = jnp.where(new_col < T, _dot_nt(qs, kn), -jnp.inf)
        m = jnp.maximum(jnp.max(sc, axis=-1, keepdims=True), jnp.max(sn, axis=-1, keepdims=True))
        pc = jnp.exp(sc - m)
        pn = jnp.exp(sn - m)
        l = jnp.sum(pc, axis=-1, keepdims=True) + jnp.sum(pn, axis=-1, keepdims=True)
        o2 = (_dot(pc.astype(BF16), vc) + _dot(pn.astype(BF16), vn)) / l
        o = o2[:T] - lam * o2[T:]
        o_ref[0, :, cols] = (_rms(o, g_ref[...]) * (1.0 - lam_init)).astype(BF16)


def _diff_decode_call(dq, kc, vc, kn, vn, lams, g_row, lam_init):
    nb, T, _ = dq.shape
    P = kc.shape[1]
    new = pl.BlockSpec((1, T, DQ), lambda b: (b, 0, 0))
    cache = pl.BlockSpec((1, P, DQ), lambda b: (b, 0, 0))
    vec = pl.BlockSpec((1, DH_D), lambda b: (0, 0))
    return pl.pallas_call(
        functools.partial(_diff_decode_kernel, lam_init=lam_init, T=T),
        grid=(nb,),
        in_specs=[new, cache, cache, new, new, vec, vec, vec, vec, pl.BlockSpec((1, DV_D), lambda b: (0, 0))],
        out_specs=new,
        out_shape=jax.ShapeDtypeStruct((nb, T, DV), BF16),
        compiler_params=pltpu.CompilerParams(dimension_semantics=("arbitrary",),
                                             vmem_limit_bytes=VMEM_LIMIT),
        name="diff_decode",
    )(dq, kc, vc, kn, vn, *lams, g_row)


def kernel(x_prompt, x_sample, state_gla, cache_diff_k, cache_diff_v, w_in, w_gate_a2, b_gate_a, gla_norm_g, lambda_q1, lambda_k1, lambda_q2, lambda_k2, diff_norm_g, w_out, mix_pre_g, mix_post_g, ffn1_pre_g, ffn1_post_g, ffn1_w_gate, ffn1_w_up, ffn1_w_down, ffn2_pre_g, ffn2_post_g, ffn2_w_gate, ffn2_w_up, ffn2_w_down):
    B, S, _ = x_prompt.shape
    Bs, T, _ = x_sample.shape
    depth = w_in.shape[0]
    xp = x_prompt.reshape(B * S, D_MODEL)
    xs = x_sample.reshape(Bs * T, D_MODEL)
    outs = [[] for _ in range(6)]
    for l in range(depth):
        lam_init = 0.8 - 0.6 * math.exp(-0.3 * l)
        row = lambda v: v[l].reshape(1, -1)
        o_gr = 2 * GQ + GV
        o_gg = o_gr + GATE_RANK
        o_dq = o_gg + GV
        wi = w_in[l]
        w = {
            "gla": jnp.concatenate([wi[:, :o_gr], wi[:, o_gg:o_dq]], axis=1).astype(BF16),
            "gr": jnp.pad(wi[:, o_gr:o_gg], ((0, 0), (0, LANES - GATE_RANK))).astype(BF16),
            "a2": jnp.pad(w_gate_a2[l], ((0, LANES - GATE_RANK), (0, 0))).astype(BF16),
            "ba": row(b_gate_a),
            "dq": wi[:, o_dq:o_dq + DQ].astype(BF16),
            "dk": wi[:, o_dq + DQ:o_dq + 2 * DQ].astype(BF16),
            "dv": wi[:, o_dq + 2 * DQ:].astype(BF16),
        }
        ffn1 = (row(ffn1_pre_g), row(ffn1_post_g), ffn1_w_gate[l].astype(BF16),
                ffn1_w_up[l].astype(BF16), ffn1_w_down[l].astype(BF16))
        ffn2 = (row(ffn2_pre_g), row(ffn2_post_g), ffn2_w_gate[l].astype(BF16),
                ffn2_w_up[l].astype(BF16), ffn2_w_down[l].astype(BF16))
        wo = w_out[l].astype(BF16)
        wog, wod = wo[:GV], wo[GV:]
        lams = (row(lambda_q1), row(lambda_k1), row(lambda_q2), row(lambda_k2))
        gla_g = row(gla_norm_g)

        x1 = _ffn_call(xp, *ffn1)
        gq, gk, gv, gg, la, dqt, dkf, dkb, dvf, dvt = _inproj_call(x1, row(mix_pre_g), w, S, True)
        b3 = lambda a: a.reshape(B, S, a.shape[-1])
        g_out, s_p = _gla_call(b3(gq), b3(gk), b3(gv), b3(la), b3(gg), gla_g,
                               jnp.zeros((B, GQ, DV_G), F32), CHUNK, GLA_ROWS)
        d_out = _diff_prompt_call(dqt, b3(dkb), dvt, lams, diff_norm_g[l].reshape(DV_D, 1), lam_init)
        xp = _ffn_call(x1, *ffn2, merge=(g_out.reshape(B * S, GV), d_out.reshape(B * S, DV),
                                         wog, wod, row(mix_post_g)))
        outs[0].append(s_p.reshape(B, H_G, DK_G, DV_G))
        outs[1].append(dkf.reshape(B, S, H_D, 2 * DH_D))
        outs[2].append(dvf.reshape(B, S, H_D, DV_D))

        x1 = _ffn_call(xs, *ffn1)
        gq, gk, gv, gg, la, dq, dkf, dkb, dvf = _inproj_call(x1, row(mix_pre_g), w, T, False)
        b3 = lambda a: a.reshape(Bs, T, a.shape[-1])
        g_out, s_s = _gla_call(b3(gq), b3(gk), b3(gv), b3(la), b3(gg), gla_g,
                               state_gla[l].reshape(Bs, GQ, DV_G), T, T)
        P = cache_diff_k.shape[2]
        d_out = _diff_decode_call(b3(dq), cache_diff_k[l].reshape(Bs, P, DQ), cache_diff_v[l].reshape(Bs, P, DV),
                                  b3(dkf), b3(dvf), lams, row(diff_norm_g), lam_init)
        xs = _ffn_call(x1, *ffn2, merge=(g_out.reshape(Bs * T, GV), d_out.reshape(Bs * T, DV),
                                         wog, wod, row(mix_post_g)))
        outs[3].append(s_s.reshape(Bs, H_G, DK_G, DV_G))
        outs[4].append(dkf.reshape(Bs, T, H_D, 2 * DH_D))
        outs[5].append(dvf.reshape(Bs, T, H_D, DV_D))

    sg_p, k_p, v_p, sg_s, k_s, v_s = (jnp.stack(o) for o in outs)
    return (xp.reshape(B, S, D_MODEL), xs.reshape(Bs, T, D_MODEL), sg_p, k_p, v_p, sg_s, k_s, v_s)
```

```python
import functools
import math

import jax
import jax.numpy as jnp
from jax import lax
from jax.experimental import pallas as pl
from jax.experimental.pallas import tpu as pltpu

F32 = jnp.float32
BF16 = jnp.bfloat16

D_MODEL = 1024
D_FF = 2816
CHUNK = 64
H_G, DK_G, DV_G = 4, 64, 128
GATE_RANK = 16
GATE_TAU = 16.0
H_D, DH_D, DV_D = 4, 64, 128
EPS = 1e-6

GQ = H_G * DK_G
GV = H_G * DV_G
DQ = H_D * 2 * DH_D
DV = H_D * DV_D

LANES = 128
MXU_N = 256
TOKEN_TILE = 512
FF_TILE = MXU_N
ATT_TILE = 512
ATT_COLS = 2 * MXU_N
GLA_ROWS = 256
VMEM_LIMIT = 56 * 1024 * 1024


def _dot(a, b):
    return jnp.dot(a, b, preferred_element_type=F32)


def _dot_nt(a, b):
    return lax.dot_general(a, b, (((1,), (1,)), ((), ())), preferred_element_type=F32)


def _rms(x, g):
    r = lax.rsqrt(jnp.mean(x * x, axis=-1, keepdims=True) + EPS)
    return x * r * g


def _silu(x):
    return x * jax.nn.sigmoid(x)


def _const_spec(shape):
    nd = len(shape)
    return pl.BlockSpec(shape, lambda *_: (0,) * nd, pipeline_mode=pl.Buffered(1))


def _swiglu_half(x, pre_ref, post_ref, wg_ref, wu_ref, wd_ref, h_ref, act_ref):
    h_ref[...] = _rms(x, pre_ref[...]).astype(BF16)
    for j in range(D_FF // FF_TILE):
        cols = slice(j * FF_TILE, (j + 1) * FF_TILE)
        h = h_ref[...]
        gate = _dot(h, wg_ref[:, cols])
        up = _dot(h, wu_ref[:, cols])
        act_ref[:, cols] = (_silu(gate) * up).astype(BF16)
    f = _dot(act_ref[...], wd_ref[...])
    return x + 0.5 * _rms(f, post_ref[...])


def _ffn_kernel(x_ref, pre_ref, post_ref, wg_ref, wu_ref, wd_ref, o_ref, h_ref, act_ref):
    o_ref[...] = _swiglu_half(x_ref[...], pre_ref, post_ref, wg_ref, wu_ref, wd_ref, h_ref, act_ref)


def _merge_ffn_kernel(x_ref, g_ref, d_ref, wog_ref, wod_ref, mixg_ref,
                      pre_ref, post_ref, wg_ref, wu_ref, wd_ref, o_ref, h_ref, act_ref, x2_ref):
    y = _dot(g_ref[...], wog_ref[...]) + _dot(d_ref[...], wod_ref[...])
    x2_ref[...] = x_ref[...] + _rms(y, mixg_ref[...])
    o_ref[...] = _swiglu_half(x2_ref[...], pre_ref, post_ref, wg_ref, wu_ref, wd_ref, h_ref, act_ref)


def _ffn_call(x, pre_g, post_g, wg, wu, wd, merge=None):
    m = x.shape[0]
    tm = TOKEN_TILE
    row = lambda w: pl.BlockSpec((tm, w), lambda i: (i, 0))
    ffn_specs = [_const_spec((1, D_MODEL)), _const_spec((1, D_MODEL)),
                 _const_spec((D_MODEL, D_FF)), _const_spec((D_MODEL, D_FF)), _const_spec((D_FF, D_MODEL))]
    scratch = [pltpu.VMEM((tm, D_MODEL), BF16), pltpu.VMEM((tm, D_FF), BF16)]
    if merge is None:
        kern, ins = _ffn_kernel, (x, pre_g, post_g, wg, wu, wd)
        in_specs = [row(D_MODEL)] + ffn_specs
    else:
        g_out, d_out, wog, wod, mix_g = merge
        kern, ins = _merge_ffn_kernel, (x, g_out, d_out, wog, wod, mix_g, pre_g, post_g, wg, wu, wd)
        in_specs = ([row(D_MODEL), row(GV), row(DV), _const_spec((GV, D_MODEL)),
                     _const_spec((DV, D_MODEL)), _const_spec((1, D_MODEL))] + ffn_specs)
        scratch = scratch + [pltpu.VMEM((tm, D_MODEL), F32)]
    return pl.pallas_call(
        kern,
        grid=(m // tm,),
        in_specs=in_specs,
        out_specs=row(D_MODEL),
        out_shape=jax.ShapeDtypeStruct((m, D_MODEL), F32),
        scratch_shapes=scratch,
        compiler_params=pltpu.CompilerParams(dimension_semantics=("arbitrary",),
                                             vmem_limit_bytes=VMEM_LIMIT),
        name="merge_ffn" if merge is not None else "ffn",
    )(*ins)


def _store_heads(ref, x):
    rows = x.shape[0]
    for h in range(H_D):
        ref[pl.ds(h, rows, stride=H_D), :] = x[:, h * LANES:(h + 1) * LANES]


def _inproj_kernel(x_ref, mixpre_ref, wgla_ref, wgr_ref, wa2_ref, ba_ref, wdq_ref, wdk_ref, wdv_ref,
                   gq_ref, gk_ref, gv_ref, gg_ref, la_ref, dq_ref, dkf_ref, dkb_ref, dvf_ref,
                   *maybe_dvt_ref, transposed):
    h = _rms(x_ref[...], mixpre_ref[...]).astype(BF16)
    gq_ref[...] = (_dot(h, wgla_ref[:, 0:GQ]) * (DK_G ** -0.5)).astype(BF16)
    gk_ref[...] = _dot(h, wgla_ref[:, GQ:2 * GQ]).astype(BF16)
    gv_ref[...] = _dot(h, wgla_ref[:, 2 * GQ:2 * GQ + GV]).astype(BF16)
    gg_ref[...] = _dot(h, wgla_ref[:, 2 * GQ + GV:2 * GQ + 2 * GV]).astype(BF16)
    gr = _dot(h, wgr_ref[...]).astype(BF16)
    a = _dot(gr, wa2_ref[...]) + ba_ref[...]
    la_ref[...] = (jnp.minimum(a, 0.0) - jnp.log1p(jnp.exp(-jnp.abs(a)))) / GATE_TAU
    dq = _dot(h, wdq_ref[...])
    dk = _dot(h, wdk_ref[...])
    dv = _dot(h, wdv_ref[...])
    _store_heads(dkf_ref, dk)
    _store_heads(dvf_ref, dv)
    dkb_ref[...] = dk.astype(BF16)
    if transposed:
        (dvt_ref,) = maybe_dvt_ref
        dq_ref[0] = (dq * (DH_D ** -0.5 * math.log2(math.e))).T.astype(BF16)
        dvt_ref[0, 0] = dv.T.astype(BF16)
    else:
        dq_ref[...] = (dq * (DH_D ** -0.5)).astype(BF16)


def _inproj_call(x, mixpre_g, w, seq_len, transposed):
    m = x.shape[0]
    tm = TOKEN_TILE
    nb = m // seq_len
    row = lambda wd: pl.BlockSpec((tm, wd), lambda i: (i, 0))
    heads = pl.BlockSpec((tm * H_D, LANES), lambda i: (i, 0))
    out_shape = [jax.ShapeDtypeStruct((m, GQ), BF16), jax.ShapeDtypeStruct((m, GQ), BF16),
                 jax.ShapeDtypeStruct((m, GV), BF16), jax.ShapeDtypeStruct((m, GV), BF16),
                 jax.ShapeDtypeStruct((m, GQ), F32)]
    out_specs = [row(GQ), row(GQ), row(GV), row(GV), row(GQ)]
    if transposed:
        assert tm == ATT_TILE
        tpb = seq_len // tm
        out_shape.append(jax.ShapeDtypeStruct((nb, DQ, seq_len), BF16))
        out_specs.append(pl.BlockSpec((1, DQ, tm), lambda i: (i // tpb, 0, i % tpb)))
    else:
        out_shape.append(jax.ShapeDtypeStruct((m, DQ), BF16))
        out_specs.append(row(DQ))
    out_shape += [jax.ShapeDtypeStruct((m * H_D, LANES), F32), jax.ShapeDtypeStruct((m, DQ), BF16),
                  jax.ShapeDtypeStruct((m * H_D, LANES), F32)]
    out_specs += [heads, row(DQ), heads]
    if transposed:
        out_shape.append(jax.ShapeDtypeStruct((nb, tpb, DV, tm), BF16))
        out_specs.append(pl.BlockSpec((1, 1, DV, tm), lambda i: (i // tpb, i % tpb, 0, 0)))
    in_specs = [row(D_MODEL), _const_spec((1, D_MODEL)), _const_spec((D_MODEL, 2 * GQ + 2 * GV)),
                _const_spec((D_MODEL, LANES)), _const_spec((LANES, GQ)), _const_spec((1, GQ)),
                _const_spec((D_MODEL, DQ)), _const_spec((D_MODEL, DQ)), _const_spec((D_MODEL, DV))]
    return pl.pallas_call(
        functools.partial(_inproj_kernel, transposed=transposed),
        grid=(m // tm,),
        in_specs=in_specs,
        out_specs=out_specs,
        out_shape=out_shape,
        compiler_params=pltpu.CompilerParams(dimension_semantics=("arbitrary",),
                                             vmem_limit_bytes=VMEM_LIMIT),
        name="inproj_t" if transposed else "inproj",
    )(x, mixpre_g, w["gla"], w["gr"], w["a2"], w["ba"], w["dq"], w["dk"], w["dv"])


def _cumsum_rows(tril, x):
    hi = x.astype(BF16)
    r = x - hi.astype(F32)
    mid = r.astype(BF16)
    lo = (r - mid.astype(F32)).astype(BF16)
    return _dot(tril, hi) + _dot(tril, mid) + _dot(tril, lo)


def _gla_kernel(q_ref, k_ref, v_ref, la_ref, gg_ref, g_ref, s0_ref, out_ref, sfin_ref, s_ref, *, L, nchunks):
    i = pl.program_id(1)

    @pl.when(i == 0)
    def _():
        s_ref[...] = s0_ref[0]

    lg = int(math.log2(L))
    r4 = lax.broadcasted_iota(jnp.int32, (4 * L, 4 * L), 0)
    c4 = lax.broadcasted_iota(jnp.int32, (4 * L, 4 * L), 1)
    same_head_causal = ((r4 >> lg) == (c4 >> lg)) & ((c4 & (L - 1)) <= (r4 & (L - 1)))
    lane_head = lax.broadcasted_iota(jnp.int32, (L, GQ), 1) >> int(math.log2(DK_G))
    tr = lax.broadcasted_iota(jnp.int32, (L, L), 0)
    tc = lax.broadcasted_iota(jnp.int32, (L, L), 1)
    tril = (tc <= tr).astype(BF16)

    def stack_heads(x):
        return jnp.concatenate([jnp.where(lane_head == h, x, 0.0) for h in range(H_G)], axis=0).astype(BF16)

    s = s_ref[...]
    for c in range(nchunks):
        rows = slice(c * L, (c + 1) * L)
        b = _cumsum_rows(tril, la_ref[0, rows, :])
        b_mid = b[L // 2 - 1:L // 2, :]
        b_end = b[L - 1:L, :]
        q = q_ref[0, rows, :].astype(F32)
        k = k_ref[0, rows, :].astype(F32)
        v = v_ref[0, rows, :]
        q_mid = stack_heads(q * jnp.exp(b - b_mid))
        q_abs = stack_heads(q * jnp.exp(b))
        k_mid = (k * jnp.exp(b_mid - b)).astype(BF16)
        k_end = k * jnp.exp(b_end - b)
        decay = jnp.exp(b_end)

        a = _dot_nt(q_mid, jnp.concatenate([k_mid] * H_G, axis=0))
        a = jnp.where(same_head_causal, a, 0.0).astype(BF16)
        v_stack = jnp.concatenate([v[:, h * DV_G:(h + 1) * DV_G] for h in range(H_G)], axis=0)
        o = _dot(a, v_stack) + _dot(q_abs, s.astype(BF16))

        kt = jnp.concatenate([k_end, jnp.broadcast_to(decay, (8, GQ)),
                              jnp.zeros((LANES - L - 8, GQ), F32)], axis=0).T
        kt_b = kt.astype(BF16)
        v_pad = jnp.concatenate([v, jnp.zeros((LANES - L, GV), BF16)], axis=0)
        decay_col = kt[:, L:L + 1]
        s = jnp.concatenate(
            [decay_col[h * DK_G:(h + 1) * DK_G] * s[h * DK_G:(h + 1) * DK_G]
             + _dot(kt_b[h * DK_G:(h + 1) * DK_G], v_pad[:, h * DV_G:(h + 1) * DV_G])
             for h in range(H_G)], axis=0)

        for h in range(H_G):
            cols = slice(h * DV_G, (h + 1) * DV_G)
            gate = gg_ref[0, rows, cols].astype(F32)
            out_ref[0, rows, cols] = (_rms(o[h * L:(h + 1) * L], g_ref[...]) * _silu(gate)).astype(BF16)

    s_ref[...] = s

    @pl.when(i == pl.num_programs(1) - 1)
    def _():
        sfin_ref[0] = s


def _gla_call(gq, gk, gv, la, gg, gla_g, s0, L, rows_per_step):
    nb, seq_len = gq.shape[0], gq.shape[1]
    blk = lambda w: pl.BlockSpec((1, rows_per_step, w), lambda b, i: (b, i, 0))
    state = pl.BlockSpec((1, GQ, DV_G), lambda b, i: (b, 0, 0))
    return pl.pallas_call(
        functools.partial(_gla_kernel, L=L, nchunks=rows_per_step // L),
        grid=(nb, seq_len // rows_per_step),
        in_specs=[blk(GQ), blk(GQ), blk(GV), blk(GQ), blk(GV),
                  pl.BlockSpec((1, DV_G), lambda b, i: (0, 0)), state],
        out_specs=[blk(GV), state],
        out_shape=[jax.ShapeDtypeStruct((nb, seq_len, GV), BF16),
                   jax.ShapeDtypeStruct((nb, GQ, DV_G), F32)],
        scratch_shapes=[pltpu.VMEM((GQ, DV_G), F32)],
        compiler_params=pltpu.CompilerParams(dimension_semantics=("arbitrary", "arbitrary"),
                                             vmem_limit_bytes=VMEM_LIMIT),
        name="gla",
    )(gq, gk, gv, la, gg, gla_g, s0)


def _lambda(lq1_ref, lk1_ref, lq2_ref, lk2_ref, lam_init):
    s1 = jnp.sum(lq1_ref[...] * lk1_ref[...], axis=-1, keepdims=True)
    s2 = jnp.sum(lq2_ref[...] * lk2_ref[...], axis=-1, keepdims=True)
    return jnp.exp(s1) - jnp.exp(s2) + lam_init


def _diff_prompt_kernel(qt_ref, k_ref, vt_ref, lq1_ref, lk1_ref, lq2_ref, lk2_ref, g_ref, o_ref,
                        rhs_ref, m_ref, l_ref, acc_ref, s_ref, *, lam_init):
    t, w = ATT_TILE, ATT_COLS
    nsub = t // w
    i = pl.program_id(2)
    qt = qt_ref[0].astype(F32)
    qrow = lax.broadcasted_iota(jnp.int32, (2 * DH_D, t), 0)
    for c in range(2):
        qc = jnp.where((qrow >= c * DH_D) & (qrow < (c + 1) * DH_D), qt, 0.0).astype(BF16)
        for sb in range(nsub):
            rhs_ref[c * nsub + sb] = qc[:, sb * w:(sb + 1) * w]
    m_ref[...] = jnp.full(m_ref.shape, -jnp.inf, F32)
    l_ref[...] = jnp.zeros(l_ref.shape, F32)
    acc_ref[...] = jnp.zeros(acc_ref.shape, F32)

    def keys(j):
        return k_ref[0, pl.ds(pl.multiple_of(j * t, t), t), :]

    def step(j, last):
        kj = keys(j)
        vtj = vt_ref[0, j]
        st = s_ref[...]
        for g in range(2 * nsub):
            if g + 1 < 2 * nsub:
                st_next = _dot(kj, rhs_ref[g + 1])
            elif not last:
                s_ref[...] = _dot(keys(j + 1), rhs_ref[0])
            if last:
                lg = int(math.log2(CHUNK))
                kr = lax.broadcasted_iota(jnp.int32, (t, w), 0)
                qc = lax.broadcasted_iota(jnp.int32, (t, w), 1) + (g % nsub) * w
                st = jnp.where((kr >> lg) <= (qc >> lg), st, -jnp.inf)
            m_old = m_ref[g]
            m_new = jnp.maximum(m_old, jnp.max(st, axis=0, keepdims=True))
            alpha = jnp.exp2(m_old - m_new)
            p = jnp.exp2(st - m_new)
            l_ref[g] = alpha * l_ref[g] + jnp.sum(p, axis=0, keepdims=True)
            acc_ref[g] = alpha * acc_ref[g] + _dot(vtj, p.astype(BF16))
            m_ref[g] = m_new
            st = st_next

    def body(j, carry):
        step(j, last=False)
        return carry

    s_ref[...] = _dot(keys(0), rhs_ref[0])
    lax.fori_loop(0, i, body, 0)
    step(i, last=True)

    lam = _lambda(lq1_ref, lk1_ref, lq2_ref, lk2_ref, lam_init)
    for sb in range(nsub):
        ot = acc_ref[sb] / l_ref[sb] - lam * (acc_ref[nsub + sb] / l_ref[nsub + sb])
        r = lax.rsqrt(jnp.mean(ot * ot, axis=0, keepdims=True) + EPS)
        ot = ot * r * g_ref[...] * (1.0 - lam_init)
        o_ref[0, sb * w:(sb + 1) * w, :] = ot.T.astype(BF16)


def _diff_prompt_call(dqt, dkb, dvt, lams, g_col, lam_init):
    nb, _, seq_len = dqt.shape
    t, w = ATT_TILE, ATT_COLS
    nchains = 2 * (t // w)
    vec = pl.BlockSpec((1, DH_D), lambda b, h, i: (0, 0))
    return pl.pallas_call(
        functools.partial(_diff_prompt_kernel, lam_init=lam_init),
        grid=(nb, H_D, seq_len // t),
        in_specs=[pl.BlockSpec((1, 2 * DH_D, t), lambda b, h, i: (b, h, i)),
                  pl.BlockSpec((1, seq_len, 2 * DH_D), lambda b, h, i: (b, 0, h)),
                  pl.BlockSpec((1, seq_len // t, DV_D, t), lambda b, h, i: (b, 0, h, 0)),
                  vec, vec, vec, vec,
                  pl.BlockSpec((DV_D, 1), lambda b, h, i: (0, 0))],
        out_specs=pl.BlockSpec((1, t, DV_D), lambda b, h, i: (b, i, h)),
        out_shape=jax.ShapeDtypeStruct((nb, seq_len, DV), BF16),
        scratch_shapes=[pltpu.VMEM((nchains, 2 * DH_D, w), BF16), pltpu.VMEM((nchains, 1, w), F32),
                        pltpu.VMEM((nchains, 1, w), F32), pltpu.VMEM((nchains, DV_D, w), F32),
                        pltpu.VMEM((t, w), F32)],
        compiler_params=pltpu.CompilerParams(dimension_semantics=("arbitrary", "arbitrary", "arbitrary"),
                                             vmem_limit_bytes=VMEM_LIMIT),
        name="diff_prompt",
    )(dqt, dkb, dvt, *lams, g_col)


def _diff_decode_kernel(q_ref, kc_ref, vc_ref, kn_ref, vn_ref, lq1_ref, lk1_ref, lq2_ref, lk2_ref, g_ref,
                        o_ref, *, lam_init, T, P):
    lam = _lambda(lq1_ref, lk1_ref, lq2_ref, lk2_ref, lam_init)
    lane = lax.broadcasted_iota(jnp.int32, (T, 2 * DH_D), 1)
    new_col = lax.broadcasted_iota(jnp.int32, (2 * T, LANES), 1)
    pad = jnp.zeros((LANES - T, 2 * DH_D), BF16)
    for h in range(H_D):
        cols = slice(h * 2 * DH_D, (h + 1) * 2 * DH_D)
        q = q_ref[0, :, cols].astype(F32)
        qs = jnp.concatenate([jnp.where(lane < DH_D, q, 0.0), jnp.where(lane >= DH_D, q, 0.0)],
                             axis=0).astype(BF16)
        kc = kc_ref[0, pl.ds(h, P, stride=H_D), :].astype(BF16)
        vc = vc_ref[0, pl.ds(h, P, stride=H_D), :].astype(BF16)
        kn = jnp.concatenate([kn_ref[0, pl.ds(h, T, stride=H_D), :].astype(BF16), pad], axis=0)
        vn = jnp.concatenate([vn_ref[0, pl.ds(h, T, stride=H_D), :].astype(BF16), pad], axis=0)
        sc = _dot_nt(qs, kc)
        sn = jnp.where(new_col < T, _dot_nt(qs, kn), -jnp.inf)
        m = jnp.maximum(jnp.max(sc, axis=-1, keepdims=True), jnp.max(sn, axis=-1, keepdims=True))
        pc = jnp.exp(sc - m)
        pn = jnp.exp(sn - m)
        l = jnp.sum(pc, axis=-1, keepdims=True) + jnp.sum(pn, axis=-1, keepdims=True)
        o2 = (_dot(pc.astype(BF16), vc) + _dot(pn.astype(BF16), vn)) / l
        o = o2[:T] - lam * o2[T:]
        o_ref[0, :, cols] = (_rms(o, g_ref[...]) * (1.0 - lam_init)).astype(BF16)


def _diff_decode_call(dq, kc, vc, kn, vn, cache_base, lams, g_row, lam_init):
    nb, T, _ = dq.shape
    P = kc.shape[1] // H_D
    new_q = pl.BlockSpec((1, T, DQ), lambda b: (b, 0, 0))
    new_kv = pl.BlockSpec((1, T * H_D, LANES), lambda b: (b, 0, 0))
    cache = pl.BlockSpec((1, P * H_D, LANES), lambda b: (cache_base + b, 0, 0))
    vec = pl.BlockSpec((1, DH_D), lambda b: (0, 0))
    return pl.pallas_call(
        functools.partial(_diff_decode_kernel, lam_init=lam_init, T=T, P=P),
        grid=(nb,),
        in_specs=[new_q, cache, cache, new_kv, new_kv, vec, vec, vec, vec,
                  pl.BlockSpec((1, DV_D), lambda b: (0, 0))],
        out_specs=new_q,
        out_shape=jax.ShapeDtypeStruct((nb, T, DV), BF16),
        compiler_params=pltpu.CompilerParams(dimension_semantics=("arbitrary",),
                                             vmem_limit_bytes=VMEM_LIMIT),
        name="diff_decode",
    )(dq, kc, vc, kn, vn, *lams, g_row)


def kernel(x_prompt, x_sample, state_gla, cache_diff_k, cache_diff_v, w_in, w_gate_a2, b_gate_a, gla_norm_g, lambda_q1, lambda_k1, lambda_q2, lambda_k2, diff_norm_g, w_out, mix_pre_g, mix_post_g, ffn1_pre_g, ffn1_post_g, ffn1_w_gate, ffn1_w_up, ffn1_w_down, ffn2_pre_g, ffn2_post_g, ffn2_w_gate, ffn2_w_up, ffn2_w_down):
    B, S, _ = x_prompt.shape
    Bs, T, _ = x_sample.shape
    depth = w_in.shape[0]
    P = cache_diff_k.shape[2]
    xp = x_prompt.reshape(B * S, D_MODEL)
    xs = x_sample.reshape(Bs * T, D_MODEL)
    cache_k = cache_diff_k.reshape(depth * Bs, P * H_D, LANES)
    cache_v = cache_diff_v.reshape(depth * Bs, P * H_D, LANES)
    outs = [[] for _ in range(6)]
    for l in range(depth):
        lam_init = 0.8 - 0.6 * math.exp(-0.3 * l)
        row = lambda v: v[l].reshape(1, -1)
        o_gr = 2 * GQ + GV
        o_gg = o_gr + GATE_RANK
        o_dq = o_gg + GV
        wi = w_in[l]
        w = {
            "gla": jnp.concatenate([wi[:, :o_gr], wi[:, o_gg:o_dq]], axis=1).astype(BF16),
            "gr": jnp.pad(wi[:, o_gr:o_gg], ((0, 0), (0, LANES - GATE_RANK))).astype(BF16),
            "a2": jnp.pad(w_gate_a2[l], ((0, LANES - GATE_RANK), (0, 0))).astype(BF16),
            "ba": row(b_gate_a),
            "dq": wi[:, o_dq:o_dq + DQ].astype(BF16),
            "dk": wi[:, o_dq + DQ:o_dq + 2 * DQ].astype(BF16),
            "dv": wi[:, o_dq + 2 * DQ:].astype(BF16),
        }
        ffn1 = (row(ffn1_pre_g), row(ffn1_post_g), ffn1_w_gate[l].astype(BF16),
                ffn1_w_up[l].astype(BF16), ffn1_w_down[l].astype(BF16))
        ffn2 = (row(ffn2_pre_g), row(ffn2_post_g), ffn2_w_gate[l].astype(BF16),
                ffn2_w_up[l].astype(BF16), ffn2_w_down[l].astype(BF16))
        wo = w_out[l].astype(BF16)
        wog, wod = wo[:GV], wo[GV:]
        lams = (row(lambda_q1), row(lambda_k1), row(lambda_q2), row(lambda_k2))
        gla_g = row(gla_norm_g)

        x1 = _ffn_call(xp, *ffn1)
        gq, gk, gv, gg, la, dqt, dkf, dkb, dvf, dvt = _inproj_call(x1, row(mix_pre_g), w, S, True)
        b3 = lambda a: a.reshape(B, S, a.shape[-1])
        g_out, s_p = _gla_call(b3(gq), b3(gk), b3(gv), b3(la), b3(gg), gla_g,
                               jnp.zeros((B, GQ, DV_G), F32), CHUNK, GLA_ROWS)
        d_out = _diff_prompt_call(dqt, b3(dkb), dvt, lams, diff_norm_g[l].reshape(DV_D, 1), lam_init)
        xp = _ffn_call(x1, *ffn2, merge=(g_out.reshape(B * S, GV), d_out.reshape(B * S, DV),
                                         wog, wod, row(mix_post_g)))
        outs[0].append(s_p.reshape(B, H_G, DK_G, DV_G))
        outs[1].append(dkf.reshape(B, S, H_D, 2 * DH_D))
        outs[2].append(dvf.reshape(B, S, H_D, DV_D))

        x1 = _ffn_call(xs, *ffn1)
        gq, gk, gv, gg, la, dq, dkf, dkb, dvf = _inproj_call(x1, row(mix_pre_g), w, T, False)
        b3 = lambda a: a.reshape(Bs, T, a.shape[-1])
        g_out, s_s = _gla_call(b3(gq), b3(gk), b3(gv), b3(la), b3(gg), gla_g,
                               state_gla[l].reshape(Bs, GQ, DV_G), T, T)
        d_out = _diff_decode_call(b3(dq), cache_k, cache_v, dkf.reshape(Bs, T * H_D, LANES),
                                  dvf.reshape(Bs, T * H_D, LANES), l * Bs, lams, row(diff_norm_g), lam_init)
        xs = _ffn_call(x1, *ffn2, merge=(g_out.reshape(Bs * T, GV), d_out.reshape(Bs * T, DV),
                                         wog, wod, row(mix_post_g)))
        outs[3].append(s_s.reshape(Bs, H_G, DK_G, DV_G))
        outs[4].append(dkf.reshape(Bs, T, H_D, 2 * DH_D))
        outs[5].append(dvf.reshape(Bs, T, H_D, DV_D))

    sg_p, k_p, v_p, sg_s, k_s, v_s = (jnp.stack(o) for o in outs)
    return (xp.reshape(B, S, D_MODEL), xs.reshape(Bs, T, D_MODEL), sg_p, k_p, v_p, sg_s, k_s, v_s)
```

```python
import functools
import math

import jax
import jax.numpy as jnp
from jax import lax
from jax.experimental import pallas as pl
from jax.experimental.pallas import tpu as pltpu

F32 = jnp.float32
BF16 = jnp.bfloat16

D_MODEL = 1024
D_FF = 2816
CHUNK = 64
H_G, DK_G, DV_G = 4, 64, 128
GATE_RANK = 16
GATE_TAU = 16.0
H_D, DH_D, DV_D = 4, 64, 128
EPS = 1e-6

GQ = H_G * DK_G
GV = H_G * DV_G
DQ = H_D * 2 * DH_D
DV = H_D * DV_D

LANES = 128
MXU_N = 256
TOKEN_TILE = 512
FF_TILE = MXU_N
ATT_TILE = 512
VT_ROWS = DV_D + 16
GLA_ROWS = 256
GLA_EXP_RANGE = 80.0
VMEM_LIMIT = 56 * 1024 * 1024


def _dot(a, b):
    return jnp.dot(a, b, preferred_element_type=F32)


def _dot_nt(a, b):
    return lax.dot_general(a, b, (((1,), (1,)), ((), ())), preferred_element_type=F32)


def _rms(x, g):
    r = lax.rsqrt(jnp.mean(x * x, axis=-1, keepdims=True) + EPS)
    return x * r * g


def _silu(x):
    return x * jax.nn.sigmoid(x)


def _const_spec(shape):
    nd = len(shape)
    return pl.BlockSpec(shape, lambda *_: (0,) * nd, pipeline_mode=pl.Buffered(1))


def _swiglu_half(x, pre_ref, post_ref, wg_ref, wu_ref, wd_ref, h_ref, act_ref):
    h_ref[...] = _rms(x, pre_ref[...]).astype(BF16)
    for j in range(D_FF // FF_TILE):
        cols = slice(j * FF_TILE, (j + 1) * FF_TILE)
        h = h_ref[...]
        gate = _dot(h, wg_ref[:, cols])
        up = _dot(h, wu_ref[:, cols])
        act_ref[:, cols] = (_silu(gate) * up).astype(BF16)
    f = _dot(act_ref[...], wd_ref[...])
    return x + 0.5 * _rms(f, post_ref[...])


def _ffn_kernel(x_ref, pre_ref, post_ref, wg_ref, wu_ref, wd_ref, o_ref, h_ref, act_ref):
    o_ref[...] = _swiglu_half(x_ref[...], pre_ref, post_ref, wg_ref, wu_ref, wd_ref, h_ref, act_ref)


def _merge_ffn_kernel(x_ref, g_ref, d_ref, wog_ref, wod_ref, mixg_ref,
                      pre_ref, post_ref, wg_ref, wu_ref, wd_ref, o_ref, h_ref, act_ref, x2_ref):
    y = _dot(g_ref[...], wog_ref[...]) + _dot(d_ref[...], wod_ref[...])
    x2_ref[...] = x_ref[...] + _rms(y, mixg_ref[...])
    o_ref[...] = _swiglu_half(x2_ref[...], pre_ref, post_ref, wg_ref, wu_ref, wd_ref, h_ref, act_ref)


def _ffn_call(x, pre_g, post_g, wg, wu, wd, merge=None):
    m = x.shape[0]
    tm = TOKEN_TILE
    row = lambda w: pl.BlockSpec((tm, w), lambda i: (i, 0))
    ffn_specs = [_const_spec((1, D_MODEL)), _const_spec((1, D_MODEL)),
                 _const_spec((D_MODEL, D_FF)), _const_spec((D_MODEL, D_FF)), _const_spec((D_FF, D_MODEL))]
    scratch = [pltpu.VMEM((tm, D_MODEL), BF16), pltpu.VMEM((tm, D_FF), BF16)]
    if merge is None:
        kern, ins = _ffn_kernel, (x, pre_g, post_g, wg, wu, wd)
        in_specs = [row(D_MODEL)] + ffn_specs
    else:
        g_out, d_out, wog, wod, mix_g = merge
        kern, ins = _merge_ffn_kernel, (x, g_out, d_out, wog, wod, mix_g, pre_g, post_g, wg, wu, wd)
        in_specs = ([row(D_MODEL), row(GV), row(DV), _const_spec((GV, D_MODEL)),
                     _const_spec((DV, D_MODEL)), _const_spec((1, D_MODEL))] + ffn_specs)
        scratch = scratch + [pltpu.VMEM((tm, D_MODEL), F32)]
    return pl.pallas_call(
        kern,
        grid=(m // tm,),
        in_specs=in_specs,
        out_specs=row(D_MODEL),
        out_shape=jax.ShapeDtypeStruct((m, D_MODEL), F32),
        scratch_shapes=scratch,
        compiler_params=pltpu.CompilerParams(dimension_semantics=("arbitrary",),
                                             vmem_limit_bytes=VMEM_LIMIT),
        name="merge_ffn" if merge is not None else "ffn",
    )(*ins)


def _store_heads(ref, x):
    rows = x.shape[0]
    for h in range(H_D):
        ref[pl.ds(h, rows, stride=H_D), :] = x[:, h * LANES:(h + 1) * LANES]


def _inproj_kernel(x_ref, mixpre_ref, wgla_ref, wgr_ref, wa2_ref, ba_ref, wdq_ref, wdk_ref, wdv_ref,
                   gq_ref, gk_ref, gv_ref, gg_ref, la_ref, dq_ref, dkf_ref, dkb_ref, dvf_ref,
                   *maybe_dvt_ref, transposed):
    h = _rms(x_ref[...], mixpre_ref[...]).astype(BF16)
    gq_ref[...] = (_dot(h, wgla_ref[:, 0:GQ]) * (DK_G ** -0.5)).astype(BF16)
    gk_ref[...] = _dot(h, wgla_ref[:, GQ:2 * GQ]).astype(BF16)
    gv_ref[...] = _dot(h, wgla_ref[:, 2 * GQ:2 * GQ + GV]).astype(BF16)
    gg_ref[...] = _dot(h, wgla_ref[:, 2 * GQ + GV:2 * GQ + 2 * GV]).astype(BF16)
    gr = _dot(h, wgr_ref[...]).astype(BF16)
    a = _dot(gr, wa2_ref[...]) + ba_ref[...]
    la_ref[...] = (jnp.minimum(a, 0.0) - jnp.log1p(jnp.exp(-jnp.abs(a)))) / GATE_TAU
    dq = _dot(h, wdq_ref[...])
    dk = _dot(h, wdk_ref[...])
    dv = _dot(h, wdv_ref[...])
    _store_heads(dkf_ref, dk)
    _store_heads(dvf_ref, dv)
    dkb_ref[...] = dk.astype(BF16)
    if transposed:
        (dvt_ref,) = maybe_dvt_ref
        dqt = (dq * (DH_D ** -0.5 * math.log2(math.e))).T.astype(BF16)
        dvt = dv.T.astype(BF16)
        for hd in range(H_D):
            dq_ref[0, 0, hd] = dqt[hd * 2 * DH_D:(hd + 1) * 2 * DH_D]
            dvt_ref[0, 0, hd, :DV_D, :] = dvt[hd * DV_D:(hd + 1) * DV_D]
            dvt_ref[0, 0, hd, DV_D:, :] = jnp.ones((VT_ROWS - DV_D, x_ref.shape[0]), BF16)
    else:
        dq_ref[...] = (dq * (DH_D ** -0.5)).astype(BF16)


def _inproj_call(x, mixpre_g, w, seq_len, transposed):
    m = x.shape[0]
    tm = TOKEN_TILE
    nb = m // seq_len
    row = lambda wd: pl.BlockSpec((tm, wd), lambda i: (i, 0))
    heads = pl.BlockSpec((tm * H_D, LANES), lambda i: (i, 0))
    out_shape = [jax.ShapeDtypeStruct((m, GQ), BF16), jax.ShapeDtypeStruct((m, GQ), BF16),
                 jax.ShapeDtypeStruct((m, GV), BF16), jax.ShapeDtypeStruct((m, GV), BF16),
                 jax.ShapeDtypeStruct((m, GQ), F32)]
    out_specs = [row(GQ), row(GQ), row(GV), row(GV), row(GQ)]
    if transposed:
        assert tm == ATT_TILE
        tpb = seq_len // tm
        out_shape.append(jax.ShapeDtypeStruct((nb, tpb, H_D, 2 * DH_D, tm), BF16))
        out_specs.append(pl.BlockSpec((1, 1, H_D, 2 * DH_D, tm), lambda i: (i // tpb, i % tpb, 0, 0, 0)))
    else:
        out_shape.append(jax.ShapeDtypeStruct((m, DQ), BF16))
        out_specs.append(row(DQ))
    out_shape += [jax.ShapeDtypeStruct((m * H_D, LANES), F32), jax.ShapeDtypeStruct((m, DQ), BF16),
                  jax.ShapeDtypeStruct((m * H_D, LANES), F32)]
    out_specs += [heads, row(DQ), heads]
    if transposed:
        out_shape.append(jax.ShapeDtypeStruct((nb, tpb, H_D, VT_ROWS, tm), BF16))
        out_specs.append(pl.BlockSpec((1, 1, H_D, VT_ROWS, tm), lambda i: (i // tpb, i % tpb, 0, 0, 0)))
    in_specs = [row(D_MODEL), _const_spec((1, D_MODEL)), _const_spec((D_MODEL, 2 * GQ + 2 * GV)),
                _const_spec((D_MODEL, LANES)), _const_spec((LANES, GQ)), _const_spec((1, GQ)),
                _const_spec((D_MODEL, DQ)), _const_spec((D_MODEL, DQ)), _const_spec((D_MODEL, DV))]
    return pl.pallas_call(
        functools.partial(_inproj_kernel, transposed=transposed),
        grid=(m // tm,),
        in_specs=in_specs,
        out_specs=out_specs,
        out_shape=out_shape,
        compiler_params=pltpu.CompilerParams(dimension_semantics=("arbitrary",),
                                             vmem_limit_bytes=VMEM_LIMIT),
        name="inproj_t" if transposed else "inproj",
    )(x, mixpre_g, w["gla"], w["gr"], w["a2"], w["ba"], w["dq"], w["dk"], w["dv"])


def _cumsum_rows(tril, x):
    hi = x.astype(BF16)
    r = x - hi.astype(F32)
    mid = r.astype(BF16)
    lo = (r - mid.astype(F32)).astype(BF16)
    return _dot(tril, hi) + _dot(tril, mid) + _dot(tril, lo)


def _gla_exact_scores(q_ref, k_ref, b_ref, a_ref, c, L):
    rows = slice(c * L, (c + 1) * L)
    q = q_ref[0, rows, :].astype(F32)
    k = k_ref[0, rows, :].astype(F32)
    b = b_ref[c]
    row_id = lax.broadcasted_iota(jnp.int32, (L, GQ), 0)
    col_id = lax.broadcasted_iota(jnp.int32, (16, L), 1)
    head_of_lane = lax.broadcasted_iota(jnp.int32, (16, GQ), 1) >> int(math.log2(DK_G))
    head_ind = (head_of_lane == lax.broadcasted_iota(jnp.int32, (16, GQ), 0)).astype(BF16)

    def one_row(t, carry):
        sel = row_id == t
        qt = jnp.sum(jnp.where(sel, q, 0.0), axis=0, keepdims=True)
        bt = jnp.sum(jnp.where(sel, b, 0.0), axis=0, keepdims=True)
        prod = qt * k * jnp.exp(jnp.minimum(bt - b, 0.0))
        hi = prod.astype(BF16)
        lo = (prod - hi.astype(F32)).astype(BF16)
        per_head = _dot_nt(head_ind, hi) + _dot_nt(head_ind, lo)
        per_head = jnp.where(col_id <= t, per_head, 0.0)
        for h in range(H_G):
            a_ref[c, pl.ds(h * L + t, 1), :] = per_head[h:h + 1, :]
        return carry

    lax.fori_loop(0, L, one_row, 0)


def _gla_kernel(q_ref, k_ref, v_ref, la_ref, gg_ref, g_ref, s0_ref, out_ref, sfin_ref,
                s_ref, a_ref, b_ref, *, L, nchunks):
    i = pl.program_id(1)

    @pl.when(i == 0)
    def _():
        s_ref[...] = s0_ref[0]

    causal = (lax.broadcasted_iota(jnp.int32, (H_G * L, L), 1)
              <= (lax.broadcasted_iota(jnp.int32, (H_G * L, L), 0) & (L - 1)))
    lane_head = lax.broadcasted_iota(jnp.int32, (L, GQ), 1) >> int(math.log2(DK_G))
    tr = lax.broadcasted_iota(jnp.int32, (L, L), 0)
    tc = lax.broadcasted_iota(jnp.int32, (L, L), 1)
    tril = (tc <= tr).astype(BF16)

    def stack_heads(x):
        return jnp.concatenate([jnp.where(lane_head == h, x, 0.0) for h in range(H_G)], axis=0).astype(BF16)

    for c in range(nchunks):
        rows = slice(c * L, (c + 1) * L)
        b = _cumsum_rows(tril, la_ref[0, rows, :])
        b_ref[c] = b
        b_mid = b[L // 2 - 1:L // 2, :]
        q_mid = stack_heads(q_ref[0, rows, :].astype(F32) * jnp.exp(b - b_mid))
        k_mid = (k_ref[0, rows, :].astype(F32) * jnp.exp(b_mid - b)).astype(BF16)
        a_ref[c] = jnp.where(causal, _dot_nt(q_mid, k_mid), 0.0)

    @pl.when(jnp.min(la_ref[0]) < -(2.0 * GLA_EXP_RANGE / L))
    def _():
        for c in range(nchunks):
            _gla_exact_scores(q_ref, k_ref, b_ref, a_ref, c, L)

    s = s_ref[...]
    for c in range(nchunks):
        rows = slice(c * L, (c + 1) * L)
        b = b_ref[c]
        b_end = b[L - 1:L, :]
        q = q_ref[0, rows, :].astype(F32)
        k = k_ref[0, rows, :].astype(F32)
        v = v_ref[0, rows, :]
        q_abs = stack_heads(q * jnp.exp(b))
        k_end = k * jnp.exp(b_end - b)
        decay = jnp.exp(b_end)
        a = a_ref[c].astype(BF16)
        o_state = _dot(q_abs, s.astype(BF16))

        kt = jnp.concatenate([k_end, jnp.broadcast_to(decay, (8, GQ)),
                              jnp.zeros((LANES - L - 8, GQ), F32)], axis=0).T
        kt_b = kt.astype(BF16)
        v_pad = jnp.concatenate([v, jnp.zeros((LANES - L, GV), BF16)], axis=0)
        decay_col = kt[:, L:L + 1]
        s = jnp.concatenate(
            [decay_col[h * DK_G:(h + 1) * DK_G] * s[h * DK_G:(h + 1) * DK_G]
             + _dot(kt_b[h * DK_G:(h + 1) * DK_G], v_pad[:, h * DV_G:(h + 1) * DV_G])
             for h in range(H_G)], axis=0)

        for h in range(H_G):
            cols = slice(h * DV_G, (h + 1) * DV_G)
            o = _dot(a[h * L:(h + 1) * L], v[:, cols]) + o_state[h * L:(h + 1) * L]
            gate = gg_ref[0, rows, cols].astype(F32)
            out_ref[0, rows, cols] = (_rms(o, g_ref[...]) * _silu(gate)).astype(BF16)

    s_ref[...] = s

    @pl.when(i == pl.num_programs(1) - 1)
    def _():
        sfin_ref[0] = s


def _gla_call(gq, gk, gv, la, gg, gla_g, s0, L, rows_per_step):
    nb, seq_len = gq.shape[0], gq.shape[1]
    blk = lambda w: pl.BlockSpec((1, rows_per_step, w), lambda b, i: (b, i, 0))
    state = pl.BlockSpec((1, GQ, DV_G), lambda b, i: (b, 0, 0))
    return pl.pallas_call(
        functools.partial(_gla_kernel, L=L, nchunks=rows_per_step // L),
        grid=(nb, seq_len // rows_per_step),
        in_specs=[blk(GQ), blk(GQ), blk(GV), blk(GQ), blk(GV),
                  pl.BlockSpec((1, DV_G), lambda b, i: (0, 0)), state],
        out_specs=[blk(GV), state],
        out_shape=[jax.ShapeDtypeStruct((nb, seq_len, GV), BF16),
                   jax.ShapeDtypeStruct((nb, GQ, DV_G), F32)],
        scratch_shapes=[pltpu.VMEM((GQ, DV_G), F32), pltpu.VMEM((rows_per_step // L, H_G * L, L), F32),
                        pltpu.VMEM((rows_per_step // L, L, GQ), F32)],
        compiler_params=pltpu.CompilerParams(dimension_semantics=("arbitrary", "arbitrary"),
                                             vmem_limit_bytes=VMEM_LIMIT),
        name="gla",
    )(gq, gk, gv, la, gg, gla_g, s0)


def _lambda(lq1_ref, lk1_ref, lq2_ref, lk2_ref, lam_init):
    s1 = jnp.sum(lq1_ref[...] * lk1_ref[...], axis=-1, keepdims=True)
    s2 = jnp.sum(lq2_ref[...] * lk2_ref[...], axis=-1, keepdims=True)
    return jnp.exp(s1) - jnp.exp(s2) + lam_init


def _diff_prompt_kernel(qt_ref, k_ref, vt_ref, bias_ref, lq1_ref, lk1_ref, lq2_ref, lk2_ref, g_ref, o_ref,
                        rhs_ref, m_ref, acc_ref, s_ref, *, lam_init):
    t = ATT_TILE
    i = pl.program_id(2)
    nq = pl.num_programs(2)

    def half_queries(iq):
        qt = qt_ref[0, iq, 0].astype(F32)
        qrow = lax.broadcasted_iota(jnp.int32, (2 * DH_D, t), 0)
        return [jnp.where((qrow >= c * DH_D) & (qrow < (c + 1) * DH_D), qt, 0.0).astype(BF16) for c in range(2)]

    def keys(j):
        return k_ref[0, pl.ds(pl.multiple_of(j * t, t), t), :]

    def scores(j, c):
        return _dot(keys(j), rhs_ref[c])

    def step(j, parity, diagonal):
        vtj = vt_ref[0, j, 0]
        for c in range(2):
            if not diagonal:
                s_ref[1 - parity, c] = scores(j + 1, c)
            st = s_ref[parity, c]
            if diagonal:
                st = st + bias_ref[...]
            m_old = m_ref[c]
            m_new = jnp.maximum(m_old, jnp.max(st, axis=0, keepdims=True))
            alpha = jnp.exp2(m_old - m_new)
            p = jnp.exp2(st - m_new)
            acc_ref[c] = alpha * acc_ref[c] + _dot(vtj, p.astype(BF16))
            m_ref[c] = m_new

    def finish():
        q_next = half_queries(jnp.minimum(i + 1, nq - 1))
        for c in range(2):
            s_ref[0, c] = _dot(keys(0), q_next[c])
        lam = _lambda(lq1_ref, lk1_ref, lq2_ref, lk2_ref, lam_init)
        a1, a2 = acc_ref[0], acc_ref[1]
        ot = a1[:DV_D] * (1.0 / a1[DV_D:DV_D + 1]) - lam * (a2[:DV_D] * (1.0 / a2[DV_D:DV_D + 1]))
        r = lax.rsqrt(jnp.mean(ot * ot, axis=0, keepdims=True) + EPS)
        ot = ot * r * g_ref[...] * (1.0 - lam_init)
        o_ref[0] = ot.T.astype(BF16)

    def pair(n, carry):
        step(2 * n, 0, False)
        step(2 * n + 1, 1, False)
        return carry

    q_now = half_queries(i)
    for c in range(2):
        rhs_ref[c] = q_now[c]

    @pl.when(i == 0)
    def _():
        for c in range(2):
            s_ref[0, c] = scores(0, c)

    m_ref[...] = jnp.full(m_ref.shape, -jnp.inf, F32)
    acc_ref[...] = jnp.zeros(acc_ref.shape, F32)
    lax.fori_loop(0, i // 2, pair, 0)

    @pl.when(i % 2 == 0)
    def _():
        step(i, 0, True)
        finish()

    @pl.when(i % 2 == 1)
    def _():
        step(i - 1, 0, False)
        step(i, 1, True)
        finish()


def _diff_prompt_call(dqt, dkb, dvt, lams, g_col, lam_init):
    nb, seq_len = dkb.shape[0], dkb.shape[1]
    t = ATT_TILE
    chunk_of = jnp.arange(t, dtype=jnp.int32) // CHUNK
    bias = jnp.where(chunk_of[:, None] <= chunk_of[None, :], 0.0, -jnp.inf).astype(F32)
    vec = pl.BlockSpec((1, DH_D), lambda b, h, i: (0, 0))
    return pl.pallas_call(
        functools.partial(_diff_prompt_kernel, lam_init=lam_init),
        grid=(nb, H_D, seq_len // t),
        in_specs=[pl.BlockSpec((1, seq_len // t, 1, 2 * DH_D, t), lambda b, h, i: (b, 0, h, 0, 0)),
                  pl.BlockSpec((1, seq_len, 2 * DH_D), lambda b, h, i: (b, 0, h)),
                  pl.BlockSpec((1, seq_len // t, 1, VT_ROWS, t), lambda b, h, i: (b, 0, h, 0, 0)),
                  pl.BlockSpec((t, t), lambda b, h, i: (0, 0), pipeline_mode=pl.Buffered(1)),
                  vec, vec, vec, vec,
                  pl.BlockSpec((DV_D, 1), lambda b, h, i: (0, 0))],
        out_specs=pl.BlockSpec((1, t, DV_D), lambda b, h, i: (b, i, h)),
        out_shape=jax.ShapeDtypeStruct((nb, seq_len, DV), BF16),
        scratch_shapes=[pltpu.VMEM((2, 2 * DH_D, t), BF16), pltpu.VMEM((2, 1, t), F32),
                        pltpu.VMEM((2, VT_ROWS, t), F32), pltpu.VMEM((2, 2, t, t), F32)],
        compiler_params=pltpu.CompilerParams(dimension_semantics=("arbitrary", "arbitrary", "arbitrary"),
                                             vmem_limit_bytes=VMEM_LIMIT),
        name="diff_prompt",
    )(dqt, dkb, dvt, bias, *lams, g_col)


def _diff_decode_kernel(q_ref, kc_ref, vc_ref, kn_ref, vn_ref, lq1_ref, lk1_ref, lq2_ref, lk2_ref, g_ref,
                        o_ref, *, lam_init, T, P):
    lam = _lambda(lq1_ref, lk1_ref, lq2_ref, lk2_ref, lam_init)
    lane = lax.broadcasted_iota(jnp.int32, (T, 2 * DH_D), 1)
    new_col = lax.broadcasted_iota(jnp.int32, (2 * T, LANES), 1)
    pad = jnp.zeros((LANES - T, 2 * DH_D), BF16)
    for h in range(H_D):
        cols = slice(h * 2 * DH_D, (h + 1) * 2 * DH_D)
        q = q_ref[0, :, cols].astype(F32)
        qs = jnp.concatenate([jnp.where(lane < DH_D, q, 0.0), jnp.where(lane >= DH_D, q, 0.0)],
                             axis=0).astype(BF16)
        kc = kc_ref[0, pl.ds(h, P, stride=H_D), :].astype(BF16)
        vc = vc_ref[0, pl.ds(h, P, stride=H_D), :].astype(BF16)
        kn = jnp.concatenate([kn_ref[0, pl.ds(h, T, stride=H_D), :].astype(BF16), pad], axis=0)
        vn = jnp.concatenate([vn_ref[0, pl.ds(h, T, stride=H_D), :].astype(BF16), pad], axis=0)
        sc = _dot_nt(qs, kc)
        sn = jnp.where(new_col < T, _dot_nt(qs, kn), -jnp.inf)
        m = jnp.maximum(jnp.max(sc, axis=-1, keepdims=True), jnp.max(sn, axis=-1, keepdims=True))
        pc = jnp.exp(sc - m)
        pn = jnp.exp(sn - m)
        l = jnp.sum(pc, axis=-1, keepdims=True) + jnp.sum(pn, axis=-1, keepdims=True)
        o2 = (_dot(pc.astype(BF16), vc) + _dot(pn.astype(BF16), vn)) / l
        o = o2[:T] - lam * o2[T:]
        o_ref[0, :, cols] = (_rms(o, g_ref[...]) * (1.0 - lam_init)).astype(BF16)


def _diff_decode_call(dq, kc, vc, kn, vn, cache_base, lams, g_row, lam_init):
    nb, T, _ = dq.shape
    P = kc.shape[1] // H_D
    new_q = pl.BlockSpec((1, T, DQ), lambda b: (b, 0, 0))
    new_kv = pl.BlockSpec((1, T * H_D, LANES), lambda b: (b, 0, 0))
    cache = pl.BlockSpec((1, P * H_D, LANES), lambda b: (cache_base + b, 0, 0))
    vec = pl.BlockSpec((1, DH_D), lambda b: (0, 0))
    return pl.pallas_call(
        functools.partial(_diff_decode_kernel, lam_init=lam_init, T=T, P=P),
        grid=(nb,),
        in_specs=[new_q, cache, cache, new_kv, new_kv, vec, vec, vec, vec,
                  pl.BlockSpec((1, DV_D), lambda b: (0, 0))],
        out_specs=new_q,
        out_shape=jax.ShapeDtypeStruct((nb, T, DV), BF16),
        compiler_params=pltpu.CompilerParams(dimension_semantics=("arbitrary",),
                                             vmem_limit_bytes=VMEM_LIMIT),
        name="diff_decode",
    )(dq, kc, vc, kn, vn, *lams, g_row)


def kernel(x_prompt, x_sample, state_gla, cache_diff_k, cache_diff_v, w_in, w_gate_a2, b_gate_a, gla_norm_g, lambda_q1, lambda_k1, lambda_q2, lambda_k2, diff_norm_g, w_out, mix_pre_g, mix_post_g, ffn1_pre_g, ffn1_post_g, ffn1_w_gate, ffn1_w_up, ffn1_w_down, ffn2_pre_g, ffn2_post_g, ffn2_w_gate, ffn2_w_up, ffn2_w_down):
    B, S, _ = x_prompt.shape
    Bs, T, _ = x_sample.shape
    depth = w_in.shape[0]
    P = cache_diff_k.shape[2]
    xp = x_prompt.reshape(B * S, D_MODEL)
    xs = x_sample.reshape(Bs * T, D_MODEL)
    cache_k = cache_diff_k.reshape(depth * Bs, P * H_D, LANES)
    cache_v = cache_diff_v.reshape(depth * Bs, P * H_D, LANES)
    outs = [[] for _ in range(6)]
    for l in range(depth):
        lam_init = 0.8 - 0.6 * math.exp(-0.3 * l)
        row = lambda v: v[l].reshape(1, -1)
        o_gr = 2 * GQ + GV
        o_gg = o_gr + GATE_RANK
        o_dq = o_gg + GV
        wi = w_in[l]
        w = {
            "gla": jnp.concatenate([wi[:, :o_gr], wi[:, o_gg:o_dq]], axis=1).astype(BF16),
            "gr": jnp.pad(wi[:, o_gr:o_gg], ((0, 0), (0, LANES - GATE_RANK))).astype(BF16),
            "a2": jnp.pad(w_gate_a2[l], ((0, LANES - GATE_RANK), (0, 0))).astype(BF16),
            "ba": row(b_gate_a),
            "dq": wi[:, o_dq:o_dq + DQ].astype(BF16),
            "dk": wi[:, o_dq + DQ:o_dq + 2 * DQ].astype(BF16),
            "dv": wi[:, o_dq + 2 * DQ:].astype(BF16),
        }
        ffn1 = (row(ffn1_pre_g), row(ffn1_post_g), ffn1_w_gate[l].astype(BF16),
                ffn1_w_up[l].astype(BF16), ffn1_w_down[l].astype(BF16))
        ffn2 = (row(ffn2_pre_g), row(ffn2_post_g), ffn2_w_gate[l].astype(BF16),
                ffn2_w_up[l].astype(BF16), ffn2_w_down[l].astype(BF16))
        wo = w_out[l].astype(BF16)
        wog, wod = wo[:GV], wo[GV:]
        lams = (row(lambda_q1), row(lambda_k1), row(lambda_q2), row(lambda_k2))
        gla_g = row(gla_norm_g)

        x1 = _ffn_call(xp, *ffn1)
        gq, gk, gv, gg, la, dqt, dkf, dkb, dvf, dvt = _inproj_call(x1, row(mix_pre_g), w, S, True)
        b3 = lambda a: a.reshape(B, S, a.shape[-1])
        g_out, s_p = _gla_call(b3(gq), b3(gk), b3(gv), b3(la), b3(gg), gla_g,
                               jnp.zeros((B, GQ, DV_G), F32), CHUNK, GLA_ROWS)
        d_out = _diff_prompt_call(dqt, b3(dkb), dvt, lams, diff_norm_g[l].reshape(DV_D, 1), lam_init)
        xp = _ffn_call(x1, *ffn2, merge=(g_out.reshape(B * S, GV), d_out.reshape(B * S, DV),
                                         wog, wod, row(mix_post_g)))
        outs[0].append(s_p.reshape(B, H_G, DK_G, DV_G))
        outs[1].append(dkf.reshape(B, S, H_D, 2 * DH_D))
        outs[2].append(dvf.reshape(B, S, H_D, DV_D))

        x1 = _ffn_call(xs, *ffn1)
        gq, gk, gv, gg, la, dq, dkf, dkb, dvf = _inproj_call(x1, row(mix_pre_g), w, T, False)
        b3 = lambda a: a.reshape(Bs, T, a.shape[-1])
        g_out, s_s = _gla_call(b3(gq), b3(gk), b3(gv), b3(la), b3(gg), gla_g,
                               state_gla[l].reshape(Bs, GQ, DV_G), T, T)
        d_out = _diff_decode_call(b3(dq), cache_k, cache_v, dkf.reshape(Bs, T * H_D, LANES),
                                  dvf.reshape(Bs, T * H_D, LANES), l * Bs, lams, row(diff_norm_g), lam_init)
        xs = _ffn_call(x1, *ffn2, merge=(g_out.reshape(Bs * T, GV), d_out.reshape(Bs * T, DV),
                                         wog, wod, row(mix_post_g)))
        outs[3].append(s_s.reshape(Bs, H_G, DK_G, DV_G))
        outs[4].append(dkf.reshape(Bs, T, H_D, 2 * DH_D))
        outs[5].append(dvf.reshape(Bs, T, H_D, DV_D))

    sg_p, k_p, v_p, sg_s, k_s, v_s = (jnp.stack(o) for o in outs)
    return (xp.reshape(B, S, D_MODEL), xs.reshape(Bs, T, D_MODEL), sg_p, k_p, v_p, sg_s, k_s, v_s)
```

```python
import functools
import math

import jax
import jax.numpy as jnp
from jax import lax
from jax.experimental import pallas as pl
from jax.experimental.pallas import tpu as pltpu

F32 = jnp.float32
BF16 = jnp.bfloat16

D_MODEL = 1024
D_FF = 2816
CHUNK = 64
H_G, DK_G, DV_G = 4, 64, 128
GATE_RANK = 16
GATE_TAU = 16.0
H_D, DH_D, DV_D = 4, 64, 128
EPS = 1e-6

GQ = H_G * DK_G
GV = H_G * DV_G
DQ = H_D * 2 * DH_D
DV = H_D * DV_D

LANES = 128
MXU_N = 256
TOKEN_TILE = 512
FFN_TILE = 512
FF_TILE = MXU_N
ATT_TILE = 512
VT_ROWS = DV_D + 16
GLA_ROWS = 256
GLA_PROMPT_BATCH = 4
GLA_DECODE_BATCH = 8
GLA_EXP_RANGE = 80.0
VMEM_LIMIT = 56 * 1024 * 1024


def _dot(a, b):
    return jnp.dot(a, b, preferred_element_type=F32)


def _dot_nt(a, b):
    return lax.dot_general(a, b, (((1,), (1,)), ((), ())), preferred_element_type=F32)


def _rms(x, g):
    r = lax.rsqrt(jnp.mean(x * x, axis=-1, keepdims=True) + EPS)
    return x * r * g


def _silu(x):
    return x * jax.nn.sigmoid(x)


def _const_spec(shape):
    nd = len(shape)
    return pl.BlockSpec(shape, lambda *_: (0,) * nd, pipeline_mode=pl.Buffered(1))


def _swiglu_half(x, pre_ref, post_ref, wg_ref, wu_ref, wd_ref, h_ref, act_ref):
    h_ref[...] = _rms(x, pre_ref[...]).astype(BF16)
    for j in range(D_FF // FF_TILE):
        cols = slice(j * FF_TILE, (j + 1) * FF_TILE)
        h = h_ref[...]
        gate = _dot(h, wg_ref[:, cols])
        up = _dot(h, wu_ref[:, cols])
        act_ref[:, cols] = (_silu(gate) * up).astype(BF16)
    f = _dot(act_ref[...], wd_ref[...])
    return x + 0.5 * _rms(f, post_ref[...])


def _ffn_kernel(x_ref, pre_ref, post_ref, wg_ref, wu_ref, wd_ref, o_ref, h_ref, act_ref):
    o_ref[...] = _swiglu_half(x_ref[...], pre_ref, post_ref, wg_ref, wu_ref, wd_ref, h_ref, act_ref)


def _merge_ffn_kernel(x_ref, g_ref, d_ref, wog_ref, wod_ref, mixg_ref,
                      pre_ref, post_ref, wg_ref, wu_ref, wd_ref, o_ref, h_ref, act_ref, x2_ref):
    y = _dot(g_ref[...], wog_ref[...]) + _dot(d_ref[...], wod_ref[...])
    x2_ref[...] = x_ref[...] + _rms(y, mixg_ref[...])
    o_ref[...] = _swiglu_half(x2_ref[...], pre_ref, post_ref, wg_ref, wu_ref, wd_ref, h_ref, act_ref)


def _ffn_call(x, pre_g, post_g, wg, wu, wd, merge=None):
    m = x.shape[0]
    tm = min(FFN_TILE, m)
    row = lambda w: pl.BlockSpec((tm, w), lambda i: (i, 0))
    ffn_specs = [_const_spec((1, D_MODEL)), _const_spec((1, D_MODEL)),
                 _const_spec((D_MODEL, D_FF)), _const_spec((D_MODEL, D_FF)), _const_spec((D_FF, D_MODEL))]
    scratch = [pltpu.VMEM((tm, D_MODEL), BF16), pltpu.VMEM((tm, D_FF), BF16)]
    if merge is None:
        kern, ins = _ffn_kernel, (x, pre_g, post_g, wg, wu, wd)
        in_specs = [row(D_MODEL)] + ffn_specs
    else:
        g_out, d_out, wog, wod, mix_g = merge
        kern, ins = _merge_ffn_kernel, (x, g_out, d_out, wog, wod, mix_g, pre_g, post_g, wg, wu, wd)
        in_specs = ([row(D_MODEL), row(GV), row(DV), _const_spec((GV, D_MODEL)),
                     _const_spec((DV, D_MODEL)), _const_spec((1, D_MODEL))] + ffn_specs)
        scratch = scratch + [pltpu.VMEM((tm, D_MODEL), F32)]
    return pl.pallas_call(
        kern,
        grid=(m // tm,),
        in_specs=in_specs,
        out_specs=row(D_MODEL),
        out_shape=jax.ShapeDtypeStruct((m, D_MODEL), F32),
        scratch_shapes=scratch,
        compiler_params=pltpu.CompilerParams(dimension_semantics=("arbitrary",),
                                             vmem_limit_bytes=VMEM_LIMIT),
        name="merge_ffn" if merge is not None else "ffn",
    )(*ins)


def _store_heads(ref, x):
    rows = x.shape[0]
    for h in range(H_D):
        ref[pl.ds(h, rows, stride=H_D), :] = x[:, h * LANES:(h + 1) * LANES]


def _inproj_kernel(x_ref, mixpre_ref, wgla_ref, wgr_ref, wa2_ref, ba_ref, wdq_ref, wdk_ref, wdv_ref,
                   gq_ref, gk_ref, gv_ref, gg_ref, la_ref, dq_ref, dkf_ref, dkb_ref, dvf_ref,
                   *maybe_dvt_ref, transposed):
    h = _rms(x_ref[...], mixpre_ref[...]).astype(BF16)
    gq_ref[...] = (_dot(h, wgla_ref[:, 0:GQ]) * (DK_G ** -0.5)).astype(BF16)
    gk_ref[...] = _dot(h, wgla_ref[:, GQ:2 * GQ]).astype(BF16)
    gv_ref[...] = _dot(h, wgla_ref[:, 2 * GQ:2 * GQ + GV]).astype(BF16)
    gg_ref[...] = _dot(h, wgla_ref[:, 2 * GQ + GV:2 * GQ + 2 * GV]).astype(BF16)
    gr = _dot(h, wgr_ref[...]).astype(BF16)
    a = _dot(gr, wa2_ref[...]) + ba_ref[...]
    la_ref[...] = (jnp.minimum(a, 0.0) - jnp.log1p(jnp.exp(-jnp.abs(a)))) / GATE_TAU
    dq = _dot(h, wdq_ref[...])
    dk = _dot(h, wdk_ref[...])
    dv = _dot(h, wdv_ref[...])
    _store_heads(dkf_ref, dk)
    _store_heads(dvf_ref, dv)
    dkb_ref[...] = dk.astype(BF16)
    if transposed:
        (dvt_ref,) = maybe_dvt_ref
        dqt = (dq * (DH_D ** -0.5 * math.log2(math.e))).T.astype(BF16)
        dvt = dv.T.astype(BF16)
        for hd in range(H_D):
            dq_ref[0, 0, hd] = dqt[hd * 2 * DH_D:(hd + 1) * 2 * DH_D]
            dvt_ref[0, 0, hd, :DV_D, :] = dvt[hd * DV_D:(hd + 1) * DV_D]
            dvt_ref[0, 0, hd, DV_D:, :] = jnp.ones((VT_ROWS - DV_D, x_ref.shape[0]), BF16)
    else:
        dq_ref[...] = (dq * (DH_D ** -0.5)).astype(BF16)


def _inproj_call(x, mixpre_g, w, seq_len, transposed):
    m = x.shape[0]
    tm = TOKEN_TILE
    nb = m // seq_len
    row = lambda wd: pl.BlockSpec((tm, wd), lambda i: (i, 0))
    heads = pl.BlockSpec((tm * H_D, LANES), lambda i: (i, 0))
    out_shape = [jax.ShapeDtypeStruct((m, GQ), BF16), jax.ShapeDtypeStruct((m, GQ), BF16),
                 jax.ShapeDtypeStruct((m, GV), BF16), jax.ShapeDtypeStruct((m, GV), BF16),
                 jax.ShapeDtypeStruct((m, GQ), F32)]
    out_specs = [row(GQ), row(GQ), row(GV), row(GV), row(GQ)]
    if transposed:
        assert tm == ATT_TILE
        tpb = seq_len // tm
        out_shape.append(jax.ShapeDtypeStruct((nb, tpb, H_D, 2 * DH_D, tm), BF16))
        out_specs.append(pl.BlockSpec((1, 1, H_D, 2 * DH_D, tm), lambda i: (i // tpb, i % tpb, 0, 0, 0)))
    else:
        out_shape.append(jax.ShapeDtypeStruct((m, DQ), BF16))
        out_specs.append(row(DQ))
    out_shape += [jax.ShapeDtypeStruct((m * H_D, LANES), F32), jax.ShapeDtypeStruct((m, DQ), BF16),
                  jax.ShapeDtypeStruct((m * H_D, LANES), F32)]
    out_specs += [heads, row(DQ), heads]
    if transposed:
        out_shape.append(jax.ShapeDtypeStruct((nb, tpb, H_D, VT_ROWS, tm), BF16))
        out_specs.append(pl.BlockSpec((1, 1, H_D, VT_ROWS, tm), lambda i: (i // tpb, i % tpb, 0, 0, 0)))
    in_specs = [row(D_MODEL), _const_spec((1, D_MODEL)), _const_spec((D_MODEL, 2 * GQ + 2 * GV)),
                _const_spec((D_MODEL, LANES)), _const_spec((LANES, GQ)), _const_spec((1, GQ)),
                _const_spec((D_MODEL, DQ)), _const_spec((D_MODEL, DQ)), _const_spec((D_MODEL, DV))]
    return pl.pallas_call(
        functools.partial(_inproj_kernel, transposed=transposed),
        grid=(m // tm,),
        in_specs=in_specs,
        out_specs=out_specs,
        out_shape=out_shape,
        compiler_params=pltpu.CompilerParams(dimension_semantics=("arbitrary",),
                                             vmem_limit_bytes=VMEM_LIMIT),
        name="inproj_t" if transposed else "inproj",
    )(x, mixpre_g, w["gla"], w["gr"], w["a2"], w["ba"], w["dq"], w["dk"], w["dv"])


def _cumsum_rows(tril, x):
    hi = x.astype(BF16)
    r = x - hi.astype(F32)
    mid = r.astype(BF16)
    lo = (r - mid.astype(F32)).astype(BF16)
    return _dot(tril, hi) + _dot(tril, mid) + _dot(tril, lo)


def _gla_exact_scores(q_ref, k_ref, b_ref, a_ref, bi, c, slot, L):
    rows = slice(c * L, (c + 1) * L)
    q = q_ref[bi, rows, :].astype(F32)
    k = k_ref[bi, rows, :].astype(F32)
    b = b_ref[slot]
    row_id = lax.broadcasted_iota(jnp.int32, (L, GQ), 0)
    col_id = lax.broadcasted_iota(jnp.int32, (16, L), 1)
    head_of_lane = lax.broadcasted_iota(jnp.int32, (16, GQ), 1) >> int(math.log2(DK_G))
    head_ind = (head_of_lane == lax.broadcasted_iota(jnp.int32, (16, GQ), 0)).astype(BF16)

    def one_row(t, carry):
        sel = row_id == t
        qt = jnp.sum(jnp.where(sel, q, 0.0), axis=0, keepdims=True)
        bt = jnp.sum(jnp.where(sel, b, 0.0), axis=0, keepdims=True)
        prod = qt * k * jnp.exp(jnp.minimum(bt - b, 0.0))
        hi = prod.astype(BF16)
        lo = (prod - hi.astype(F32)).astype(BF16)
        per_head = _dot_nt(head_ind, hi) + _dot_nt(head_ind, lo)
        per_head = jnp.where(col_id <= t, per_head, 0.0)
        for h in range(H_G):
            a_ref[slot, pl.ds(h * L + t, 1), :] = per_head[h:h + 1, :]
        return carry

    lax.fori_loop(0, L, one_row, 0)


def _gla_kernel(q_ref, k_ref, v_ref, la_ref, gg_ref, g_ref, s0_ref, out_ref, sfin_ref,
                s_ref, a_ref, b_ref, *, L, nchunks, nbatch):
    i = pl.program_id(1)

    @pl.when(i == 0)
    def _():
        s_ref[...] = s0_ref[...]

    causal = (lax.broadcasted_iota(jnp.int32, (H_G * L, L), 1)
              <= (lax.broadcasted_iota(jnp.int32, (H_G * L, L), 0) & (L - 1)))
    lane_head = lax.broadcasted_iota(jnp.int32, (L, GQ), 1) >> int(math.log2(DK_G))
    tr = lax.broadcasted_iota(jnp.int32, (L, L), 0)
    tc = lax.broadcasted_iota(jnp.int32, (L, L), 1)
    tril = (tc <= tr).astype(BF16)

    def stack_heads(x):
        return jnp.concatenate([jnp.where(lane_head == h, x, 0.0) for h in range(H_G)], axis=0).astype(BF16)

    work = [(c, bi, bi * nchunks + c) for c in range(nchunks) for bi in range(nbatch)]
    for c, bi, slot in work:
        rows = slice(c * L, (c + 1) * L)
        b = _cumsum_rows(tril, la_ref[bi, rows, :])
        b_ref[slot] = b
        b_mid = b[L // 2 - 1:L // 2, :]
        q_mid = stack_heads(q_ref[bi, rows, :].astype(F32) * jnp.exp(b - b_mid))
        k_mid = (k_ref[bi, rows, :].astype(F32) * jnp.exp(b_mid - b)).astype(BF16)
        a_ref[slot] = jnp.where(causal, _dot_nt(q_mid, k_mid), 0.0)

    @pl.when(jnp.min(la_ref[...]) < -(2.0 * GLA_EXP_RANGE / L))
    def _():
        for c, bi, slot in work:
            _gla_exact_scores(q_ref, k_ref, b_ref, a_ref, bi, c, slot, L)

    states = [s_ref[bi] for bi in range(nbatch)]
    for c, bi, slot in work:
        rows = slice(c * L, (c + 1) * L)
        s = states[bi]
        b = b_ref[slot]
        b_end = b[L - 1:L, :]
        q = q_ref[bi, rows, :].astype(F32)
        k = k_ref[bi, rows, :].astype(F32)
        v = v_ref[bi, rows, :]
        q_abs = stack_heads(q * jnp.exp(b))
        k_end = k * jnp.exp(b_end - b)
        decay = jnp.exp(b_end)
        a = a_ref[slot].astype(BF16)
        o_state = _dot(q_abs, s.astype(BF16))

        kt = jnp.concatenate([k_end, jnp.broadcast_to(decay, (8, GQ)),
                              jnp.zeros((LANES - L - 8, GQ), F32)], axis=0).T
        kt_b = kt.astype(BF16)
        v_pad = jnp.concatenate([v, jnp.zeros((LANES - L, GV), BF16)], axis=0)
        decay_col = kt[:, L:L + 1]
        states[bi] = jnp.concatenate(
            [decay_col[h * DK_G:(h + 1) * DK_G] * s[h * DK_G:(h + 1) * DK_G]
             + _dot(kt_b[h * DK_G:(h + 1) * DK_G], v_pad[:, h * DV_G:(h + 1) * DV_G])
             for h in range(H_G)], axis=0)

        for h in range(H_G):
            cols = slice(h * DV_G, (h + 1) * DV_G)
            o = _dot(a[h * L:(h + 1) * L], v[:, cols]) + o_state[h * L:(h + 1) * L]
            gate = gg_ref[bi, rows, cols].astype(F32)
            out_ref[bi, rows, cols] = (_rms(o, g_ref[...]) * _silu(gate)).astype(BF16)

    for bi in range(nbatch):
        s_ref[bi] = states[bi]

    @pl.when(i == pl.num_programs(1) - 1)
    def _():
        for bi in range(nbatch):
            sfin_ref[bi] = states[bi]


def _gla_call(gq, gk, gv, la, gg, gla_g, s0, L, rows_per_step, nbatch):
    nb, seq_len = gq.shape[0], gq.shape[1]
    nchunks = rows_per_step // L
    blk = lambda w: pl.BlockSpec((nbatch, rows_per_step, w), lambda b, i: (b, i, 0))
    state = pl.BlockSpec((nbatch, GQ, DV_G), lambda b, i: (b, 0, 0))
    return pl.pallas_call(
        functools.partial(_gla_kernel, L=L, nchunks=nchunks, nbatch=nbatch),
        grid=(nb // nbatch, seq_len // rows_per_step),
        in_specs=[blk(GQ), blk(GQ), blk(GV), blk(GQ), blk(GV),
                  pl.BlockSpec((1, DV_G), lambda b, i: (0, 0)), state],
        out_specs=[blk(GV), state],
        out_shape=[jax.ShapeDtypeStruct((nb, seq_len, GV), BF16),
                   jax.ShapeDtypeStruct((nb, GQ, DV_G), F32)],
        scratch_shapes=[pltpu.VMEM((nbatch, GQ, DV_G), F32), pltpu.VMEM((nbatch * nchunks, H_G * L, L), F32),
                        pltpu.VMEM((nbatch * nchunks, L, GQ), F32)],
        compiler_params=pltpu.CompilerParams(dimension_semantics=("arbitrary", "arbitrary"),
                                             vmem_limit_bytes=VMEM_LIMIT),
        name="gla",
    )(gq, gk, gv, la, gg, gla_g, s0)


def _lambda(lq1_ref, lk1_ref, lq2_ref, lk2_ref, lam_init):
    s1 = jnp.sum(lq1_ref[...] * lk1_ref[...], axis=-1, keepdims=True)
    s2 = jnp.sum(lq2_ref[...] * lk2_ref[...], axis=-1, keepdims=True)
    return jnp.exp(s1) - jnp.exp(s2) + lam_init


def _diff_prompt_kernel(qt_ref, k_ref, vt_ref, bias_ref, lq1_ref, lk1_ref, lq2_ref, lk2_ref, g_ref, o_ref,
                        rhs_ref, m_ref, acc_ref, s_ref, *, lam_init):
    t = ATT_TILE
    i = pl.program_id(2)
    nq = pl.num_programs(2)

    def half_queries(iq):
        qt = qt_ref[0, iq, 0].astype(F32)
        qrow = lax.broadcasted_iota(jnp.int32, (2 * DH_D, t), 0)
        return [jnp.where((qrow >= c * DH_D) & (qrow < (c + 1) * DH_D), qt, 0.0).astype(BF16) for c in range(2)]

    def keys(j):
        return k_ref[0, pl.ds(pl.multiple_of(j * t, t), t), :]

    def scores(j, c):
        return _dot(keys(j), rhs_ref[c])

    def step(j, parity, diagonal):
        vtj = vt_ref[0, j, 0]
        for c in range(2):
            if not diagonal:
                s_ref[1 - parity, c] = scores(j + 1, c)
            st = s_ref[parity, c]
            if diagonal:
                st = st + bias_ref[...]
            m_old = m_ref[c]
            m_new = jnp.maximum(m_old, jnp.max(st, axis=0, keepdims=True))
            alpha = jnp.exp2(m_old - m_new)
            p = jnp.exp2(st - m_new)
            acc_ref[c] = alpha * acc_ref[c] + _dot(vtj, p.astype(BF16))
            m_ref[c] = m_new

    def finish():
        q_next = half_queries(jnp.minimum(i + 1, nq - 1))
        for c in range(2):
            s_ref[0, c] = _dot(keys(0), q_next[c])
        lam = _lambda(lq1_ref, lk1_ref, lq2_ref, lk2_ref, lam_init)
        a1, a2 = acc_ref[0], acc_ref[1]
        ot = a1[:DV_D] * (1.0 / a1[DV_D:DV_D + 1]) - lam * (a2[:DV_D] * (1.0 / a2[DV_D:DV_D + 1]))
        r = lax.rsqrt(jnp.mean(ot * ot, axis=0, keepdims=True) + EPS)
        ot = ot * r * g_ref[...] * (1.0 - lam_init)
        o_ref[0] = ot.T.astype(BF16)

    def pair(n, carry):
        step(2 * n, 0, False)
        step(2 * n + 1, 1, False)
        return carry

    q_now = half_queries(i)
    for c in range(2):
        rhs_ref[c] = q_now[c]

    @pl.when(i == 0)
    def _():
        for c in range(2):
            s_ref[0, c] = scores(0, c)

    m_ref[...] = jnp.full(m_ref.shape, -jnp.inf, F32)
    acc_ref[...] = jnp.zeros(acc_ref.shape, F32)
    lax.fori_loop(0, i // 2, pair, 0)

    @pl.when(i % 2 == 0)
    def _():
        step(i, 0, True)
        finish()

    @pl.when(i % 2 == 1)
    def _():
        step(i - 1, 0, False)
        step(i, 1, True)
        finish()


def _diff_prompt_call(dqt, dkb, dvt, lams, g_col, lam_init):
    nb, seq_len = dkb.shape[0], dkb.shape[1]
    t = ATT_TILE
    chunk_of = jnp.arange(t, dtype=jnp.int32) // CHUNK
    bias = jnp.where(chunk_of[:, None] <= chunk_of[None, :], 0.0, -jnp.inf).astype(F32)
    vec = pl.BlockSpec((1, DH_D), lambda b, h, i: (0, 0))
    return pl.pallas_call(
        functools.partial(_diff_prompt_kernel, lam_init=lam_init),
        grid=(nb, H_D, seq_len // t),
        in_specs=[pl.BlockSpec((1, seq_len // t, 1, 2 * DH_D, t), lambda b, h, i: (b, 0, h, 0, 0)),
                  pl.BlockSpec((1, seq_len, 2 * DH_D), lambda b, h, i: (b, 0, h)),
                  pl.BlockSpec((1, seq_len // t, 1, VT_ROWS, t), lambda b, h, i: (b, 0, h, 0, 0)),
                  pl.BlockSpec((t, t), lambda b, h, i: (0, 0), pipeline_mode=pl.Buffered(1)),
                  vec, vec, vec, vec,
                  pl.BlockSpec((DV_D, 1), lambda b, h, i: (0, 0))],
        out_specs=pl.BlockSpec((1, t, DV_D), lambda b, h, i: (b, i, h)),
        out_shape=jax.ShapeDtypeStruct((nb, seq_len, DV), BF16),
        scratch_shapes=[pltpu.VMEM((2, 2 * DH_D, t), BF16), pltpu.VMEM((2, 1, t), F32),
                        pltpu.VMEM((2, VT_ROWS, t), F32), pltpu.VMEM((2, 2, t, t), F32)],
        compiler_params=pltpu.CompilerParams(dimension_semantics=("arbitrary", "arbitrary", "arbitrary"),
                                             vmem_limit_bytes=VMEM_LIMIT),
        name="diff_prompt",
    )(dqt, dkb, dvt, bias, *lams, g_col)


def _diff_decode_kernel(q_ref, kc_ref, vc_ref, kn_ref, vn_ref, lq1_ref, lk1_ref, lq2_ref, lk2_ref, g_ref,
                        o_ref, *, lam_init, T, P):
    lam = _lambda(lq1_ref, lk1_ref, lq2_ref, lk2_ref, lam_init)
    lane = lax.broadcasted_iota(jnp.int32, (T, 2 * DH_D), 1)
    new_col = lax.broadcasted_iota(jnp.int32, (2 * T, LANES), 1)
    pad = jnp.zeros((LANES - T, 2 * DH_D), BF16)
    for h in range(H_D):
        cols = slice(h * 2 * DH_D, (h + 1) * 2 * DH_D)
        q = q_ref[0, :, cols].astype(F32)
        qs = jnp.concatenate([jnp.where(lane < DH_D, q, 0.0), jnp.where(lane >= DH_D, q, 0.0)],
                             axis=0).astype(BF16)
        kc = kc_ref[0, pl.ds(h, P, stride=H_D), :].astype(BF16)
        vc = vc_ref[0, pl.ds(h, P, stride=H_D), :].astype(BF16)
        kn = jnp.concatenate([kn_ref[0, pl.ds(h, T, stride=H_D), :].astype(BF16), pad], axis=0)
        vn = jnp.concatenate([vn_ref[0, pl.ds(h, T, stride=H_D), :].astype(BF16), pad], axis=0)
        sc = _dot_nt(qs, kc)
        sn = jnp.where(new_col < T, _dot_nt(qs, kn), -jnp.inf)
        m = jnp.maximum(jnp.max(sc, axis=-1, keepdims=True), jnp.max(sn, axis=-1, keepdims=True))
        pc = jnp.exp(sc - m)
        pn = jnp.exp(sn - m)
        l = jnp.sum(pc, axis=-1, keepdims=True) + jnp.sum(pn, axis=-1, keepdims=True)
        o2 = (_dot(pc.astype(BF16), vc) + _dot(pn.astype(BF16), vn)) / l
        o = o2[:T] - lam * o2[T:]
        o_ref[0, :, cols] = (_rms(o, g_ref[...]) * (1.0 - lam_init)).astype(BF16)


def _diff_decode_call(dq, kc, vc, kn, vn, cache_base, lams, g_row, lam_init):
    nb, T, _ = dq.shape
    P = kc.shape[1] // H_D
    new_q = pl.BlockSpec((1, T, DQ), lambda b: (b, 0, 0))
    new_kv = pl.BlockSpec((1, T * H_D, LANES), lambda b: (b, 0, 0))
    cache = pl.BlockSpec((1, P * H_D, LANES), lambda b: (cache_base + b, 0, 0))
    vec = pl.BlockSpec((1, DH_D), lambda b: (0, 0))
    return pl.pallas_call(
        functools.partial(_diff_decode_kernel, lam_init=lam_init, T=T, P=P),
        grid=(nb,),
        in_specs=[new_q, cache, cache, new_kv, new_kv, vec, vec, vec, vec,
                  pl.BlockSpec((1, DV_D), lambda b: (0, 0))],
        out_specs=new_q,
        out_shape=jax.ShapeDtypeStruct((nb, T, DV), BF16),
        compiler_params=pltpu.CompilerParams(dimension_semantics=("arbitrary",),
                                             vmem_limit_bytes=VMEM_LIMIT),
        name="diff_decode",
    )(dq, kc, vc, kn, vn, *lams, g_row)


def kernel(x_prompt, x_sample, state_gla, cache_diff_k, cache_diff_v, w_in, w_gate_a2, b_gate_a, gla_norm_g, lambda_q1, lambda_k1, lambda_q2, lambda_k2, diff_norm_g, w_out, mix_pre_g, mix_post_g, ffn1_pre_g, ffn1_post_g, ffn1_w_gate, ffn1_w_up, ffn1_w_down, ffn2_pre_g, ffn2_post_g, ffn2_w_gate, ffn2_w_up, ffn2_w_down):
    B, S, _ = x_prompt.shape
    Bs, T, _ = x_sample.shape
    depth = w_in.shape[0]
    P = cache_diff_k.shape[2]
    xp = x_prompt.reshape(B * S, D_MODEL)
    xs = x_sample.reshape(Bs * T, D_MODEL)
    cache_k = cache_diff_k.reshape(depth * Bs, P * H_D, LANES)
    cache_v = cache_diff_v.reshape(depth * Bs, P * H_D, LANES)
    outs = [[] for _ in range(6)]
    for l in range(depth):
        lam_init = 0.8 - 0.6 * math.exp(-0.3 * l)
        row = lambda v: v[l].reshape(1, -1)
        o_gr = 2 * GQ + GV
        o_gg = o_gr + GATE_RANK
        o_dq = o_gg + GV
        wi = w_in[l]
        w = {
            "gla": jnp.concatenate([wi[:, :o_gr], wi[:, o_gg:o_dq]], axis=1).astype(BF16),
            "gr": jnp.pad(wi[:, o_gr:o_gg], ((0, 0), (0, LANES - GATE_RANK))).astype(BF16),
            "a2": jnp.pad(w_gate_a2[l], ((0, LANES - GATE_RANK), (0, 0))).astype(BF16),
            "ba": row(b_gate_a),
            "dq": wi[:, o_dq:o_dq + DQ].astype(BF16),
            "dk": wi[:, o_dq + DQ:o_dq + 2 * DQ].astype(BF16),
            "dv": wi[:, o_dq + 2 * DQ:].astype(BF16),
        }
        ffn1 = (row(ffn1_pre_g), row(ffn1_post_g), ffn1_w_gate[l].astype(BF16),
                ffn1_w_up[l].astype(BF16), ffn1_w_down[l].astype(BF16))
        ffn2 = (row(ffn2_pre_g), row(ffn2_post_g), ffn2_w_gate[l].astype(BF16),
                ffn2_w_up[l].astype(BF16), ffn2_w_down[l].astype(BF16))
        wo = w_out[l].astype(BF16)
        wog, wod = wo[:GV], wo[GV:]
        lams = (row(lambda_q1), row(lambda_k1), row(lambda_q2), row(lambda_k2))
        gla_g = row(gla_norm_g)

        x1 = _ffn_call(xp, *ffn1)
        gq, gk, gv, gg, la, dqt, dkf, dkb, dvf, dvt = _inproj_call(x1, row(mix_pre_g), w, S, True)
        b3 = lambda a: a.reshape(B, S, a.shape[-1])
        g_out, s_p = _gla_call(b3(gq), b3(gk), b3(gv), b3(la), b3(gg), gla_g,
                               jnp.zeros((B, GQ, DV_G), F32), CHUNK, GLA_ROWS, GLA_PROMPT_BATCH)
        d_out = _diff_prompt_call(dqt, b3(dkb), dvt, lams, diff_norm_g[l].reshape(DV_D, 1), lam_init)
        xp = _ffn_call(x1, *ffn2, merge=(g_out.reshape(B * S, GV), d_out.reshape(B * S, DV),
                                         wog, wod, row(mix_post_g)))
        outs[0].append(s_p.reshape(B, H_G, DK_G, DV_G))
        outs[1].append(dkf.reshape(B, S, H_D, 2 * DH_D))
        outs[2].append(dvf.reshape(B, S, H_D, DV_D))

        x1 = _ffn_call(xs, *ffn1)
        gq, gk, gv, gg, la, dq, dkf, dkb, dvf = _inproj_call(x1, row(mix_pre_g), w, T, False)
        b3 = lambda a: a.reshape(Bs, T, a.shape[-1])
        g_out, s_s = _gla_call(b3(gq), b3(gk), b3(gv), b3(la), b3(gg), gla_g,
                               state_gla[l].reshape(Bs, GQ, DV_G), T, T, GLA_DECODE_BATCH)
        d_out = _diff_decode_call(b3(dq), cache_k, cache_v, dkf.reshape(Bs, T * H_D, LANES),
                                  dvf.reshape(Bs, T * H_D, LANES), l * Bs, lams, row(diff_norm_g), lam_init)
        xs = _ffn_call(x1, *ffn2, merge=(g_out.reshape(Bs * T, GV), d_out.reshape(Bs * T, DV),
                                         wog, wod, row(mix_post_g)))
        outs[3].append(s_s.reshape(Bs, H_G, DK_G, DV_G))
        outs[4].append(dkf.reshape(Bs, T, H_D, 2 * DH_D))
        outs[5].append(dvf.reshape(Bs, T, H_D, DV_D))

    sg_p, k_p, v_p, sg_s, k_s, v_s = (jnp.stack(o) for o in outs)
    return (xp.reshape(B, S, D_MODEL), xs.reshape(Bs, T, D_MODEL), sg_p, k_p, v_p, sg_s, k_s, v_s)
```

```python
import functools
import math

import jax
import jax.numpy as jnp
from jax import lax
from jax.experimental import pallas as pl
from jax.experimental.pallas import tpu as pltpu

F32 = jnp.float32
BF16 = jnp.bfloat16

D_MODEL = 1024
D_FF = 2816
CHUNK = 64
H_G, DK_G, DV_G = 4, 64, 128
GATE_RANK = 16
GATE_TAU = 16.0
H_D, DH_D, DV_D = 4, 64, 128
EPS = 1e-6

GQ = H_G * DK_G
GV = H_G * DV_G
DQ = H_D * 2 * DH_D
DV = H_D * DV_D

LANES = 128
MXU_N = 256
TOKEN_TILE = 512
FFN_TILE = 1024
FFN_SUB = 512
FF_TILE = MXU_N
ATT_TILE = 512
VT_ROWS = DV_D + 16
GLA_ROWS = 256
GLA_PROMPT_BATCH = 8
GLA_DECODE_BATCH = 8
GLA_EXP_RANGE = 80.0
VMEM_LIMIT = 56 * 1024 * 1024


def _dot(a, b):
    return jnp.dot(a, b, preferred_element_type=F32)


def _dot_nt(a, b):
    return lax.dot_general(a, b, (((1,), (1,)), ((), ())), preferred_element_type=F32)


def _rms(x, g):
    r = lax.rsqrt(jnp.mean(x * x, axis=-1, keepdims=True) + EPS)
    return x * r * g


def _silu(x):
    return x * jax.nn.sigmoid(x)


def _const_spec(shape):
    nd = len(shape)
    return pl.BlockSpec(shape, lambda *_: (0,) * nd, pipeline_mode=pl.Buffered(1))


def _swiglu_half(x, pre_ref, post_ref, wg_ref, wu_ref, wd_ref, h_ref, act_ref):
    h_ref[...] = _rms(x, pre_ref[...]).astype(BF16)
    for j in range(D_FF // FF_TILE):
        cols = slice(j * FF_TILE, (j + 1) * FF_TILE)
        h = h_ref[...]
        gate = _dot(h, wg_ref[:, cols])
        up = _dot(h, wu_ref[:, cols])
        act_ref[:, cols] = (_silu(gate) * up).astype(BF16)
    f = _dot(act_ref[...], wd_ref[...])
    return x + 0.5 * _rms(f, post_ref[...])


def _sub_tiles(ref):
    return [slice(s * FFN_SUB, (s + 1) * FFN_SUB) for s in range(ref.shape[0] // FFN_SUB)]


def _ffn_kernel(x_ref, pre_ref, post_ref, wg_ref, wu_ref, wd_ref, o_ref, h_ref, act_ref):
    for s, rows in enumerate(_sub_tiles(x_ref)):
        o_ref[rows, :] = _swiglu_half(x_ref[rows, :], pre_ref, post_ref, wg_ref, wu_ref, wd_ref,
                                      h_ref.at[s], act_ref.at[s])


def _merge_ffn_kernel(x_ref, g_ref, d_ref, wog_ref, wod_ref, mixg_ref,
                      pre_ref, post_ref, wg_ref, wu_ref, wd_ref, o_ref, h_ref, act_ref, x2_ref):
    for s, rows in enumerate(_sub_tiles(x_ref)):
        y = _dot(g_ref[rows, :], wog_ref[...]) + _dot(d_ref[rows, :], wod_ref[...])
        x2_ref[s] = x_ref[rows, :] + _rms(y, mixg_ref[...])
    for s, rows in enumerate(_sub_tiles(x_ref)):
        o_ref[rows, :] = _swiglu_half(x2_ref[s], pre_ref, post_ref, wg_ref, wu_ref, wd_ref,
                                      h_ref.at[s], act_ref.at[s])


def _ffn_call(x, pre_g, post_g, wg, wu, wd, merge=None):
    m = x.shape[0]
    tm = min(FFN_TILE, m)
    nsub = tm // FFN_SUB
    row = lambda w: pl.BlockSpec((tm, w), lambda i: (i, 0))
    ffn_specs = [_const_spec((1, D_MODEL)), _const_spec((1, D_MODEL)),
                 _const_spec((D_MODEL, D_FF)), _const_spec((D_MODEL, D_FF)), _const_spec((D_FF, D_MODEL))]
    scratch = [pltpu.VMEM((nsub, FFN_SUB, D_MODEL), BF16), pltpu.VMEM((nsub, FFN_SUB, D_FF), BF16)]
    if merge is None:
        kern, ins = _ffn_kernel, (x, pre_g, post_g, wg, wu, wd)
        in_specs = [row(D_MODEL)] + ffn_specs
    else:
        g_out, d_out, wog, wod, mix_g = merge
        kern, ins = _merge_ffn_kernel, (x, g_out, d_out, wog, wod, mix_g, pre_g, post_g, wg, wu, wd)
        in_specs = ([row(D_MODEL), row(GV), row(DV), _const_spec((GV, D_MODEL)),
                     _const_spec((DV, D_MODEL)), _const_spec((1, D_MODEL))] + ffn_specs)
        scratch = scratch + [pltpu.VMEM((nsub, FFN_SUB, D_MODEL), F32)]
    return pl.pallas_call(
        kern,
        grid=(m // tm,),
        in_specs=in_specs,
        out_specs=row(D_MODEL),
        out_shape=jax.ShapeDtypeStruct((m, D_MODEL), F32),
        scratch_shapes=scratch,
        compiler_params=pltpu.CompilerParams(dimension_semantics=("arbitrary",),
                                             vmem_limit_bytes=VMEM_LIMIT),
        name="merge_ffn" if merge is not None else "ffn",
    )(*ins)


def _store_heads(ref, x, row0):
    rows = x.shape[0]
    for h in range(H_D):
        ref[pl.ds(row0 * H_D + h, rows, stride=H_D), :] = x[:, h * LANES:(h + 1) * LANES]


def _inproj_kernel(x_ref, mixpre_ref, wgla_ref, wgr_ref, wa2_ref, ba_ref, wdq_ref, wdk_ref, wdv_ref,
                   gq_ref, gk_ref, gv_ref, gg_ref, la_ref, dq_ref, dkf_ref, dkb_ref, dvf_ref,
                   *maybe_dvt_ref, transposed):
    t = ATT_TILE
    for s in range(x_ref.shape[0] // t):
        rows = slice(s * t, (s + 1) * t)
        h = _rms(x_ref[rows, :], mixpre_ref[...]).astype(BF16)
        dv = _dot(h, wdv_ref[...])
        _store_heads(dvf_ref, dv, s * t)
        dq = _dot(h, wdq_ref[...])
        if transposed:
            (dvt_ref,) = maybe_dvt_ref
            dvt = dv.T.astype(BF16)
            dqt = (dq * (DH_D ** -0.5 * math.log2(math.e))).T.astype(BF16)
            for hd in range(H_D):
                dq_ref[0, s, hd] = dqt[hd * 2 * DH_D:(hd + 1) * 2 * DH_D]
                dvt_ref[0, s, hd, :DV_D, :] = dvt[hd * DV_D:(hd + 1) * DV_D]
                dvt_ref[0, s, hd, DV_D:, :] = jnp.ones((VT_ROWS - DV_D, t), BF16)
        else:
            dq_ref[rows, :] = (dq * (DH_D ** -0.5)).astype(BF16)
        dk = _dot(h, wdk_ref[...])
        _store_heads(dkf_ref, dk, s * t)
        dkb_ref[rows, :] = dk.astype(BF16)
        gr = _dot(h, wgr_ref[...]).astype(BF16)
        a = _dot(gr, wa2_ref[...]) + ba_ref[...]
        la_ref[rows, :] = (jnp.minimum(a, 0.0) - jnp.log1p(jnp.exp(-jnp.abs(a)))) / GATE_TAU
        gq_ref[rows, :] = (_dot(h, wgla_ref[:, 0:GQ]) * (DK_G ** -0.5)).astype(BF16)
        gk_ref[rows, :] = _dot(h, wgla_ref[:, GQ:2 * GQ]).astype(BF16)
        gv_ref[rows, :] = _dot(h, wgla_ref[:, 2 * GQ:2 * GQ + GV]).astype(BF16)
        gg_ref[rows, :] = _dot(h, wgla_ref[:, 2 * GQ + GV:2 * GQ + 2 * GV]).astype(BF16)


def _inproj_call(x, mixpre_g, w, seq_len, transposed):
    m = x.shape[0]
    tm = min(TOKEN_TILE, m)
    nb = m // seq_len
    row = lambda wd: pl.BlockSpec((tm, wd), lambda i: (i, 0))
    heads = pl.BlockSpec((tm * H_D, LANES), lambda i: (i, 0))
    out_shape = [jax.ShapeDtypeStruct((m, GQ), BF16), jax.ShapeDtypeStruct((m, GQ), BF16),
                 jax.ShapeDtypeStruct((m, GV), BF16), jax.ShapeDtypeStruct((m, GV), BF16),
                 jax.ShapeDtypeStruct((m, GQ), F32)]
    out_specs = [row(GQ), row(GQ), row(GV), row(GV), row(GQ)]
    if transposed:
        t = ATT_TILE
        nq, per_step = seq_len // t, tm // t
        spb = nq // per_step
        out_shape.append(jax.ShapeDtypeStruct((nb, nq, H_D, 2 * DH_D, t), BF16))
        out_specs.append(pl.BlockSpec((1, per_step, H_D, 2 * DH_D, t), lambda i: (i // spb, i % spb, 0, 0, 0)))
    else:
        out_shape.append(jax.ShapeDtypeStruct((m, DQ), BF16))
        out_specs.append(row(DQ))
    out_shape += [jax.ShapeDtypeStruct((m * H_D, LANES), F32), jax.ShapeDtypeStruct((m, DQ), BF16),
                  jax.ShapeDtypeStruct((m * H_D, LANES), F32)]
    out_specs += [heads, row(DQ), heads]
    if transposed:
        out_shape.append(jax.ShapeDtypeStruct((nb, nq, H_D, VT_ROWS, t), BF16))
        out_specs.append(pl.BlockSpec((1, per_step, H_D, VT_ROWS, t), lambda i: (i // spb, i % spb, 0, 0, 0)))
    in_specs = [row(D_MODEL), _const_spec((1, D_MODEL)), _const_spec((D_MODEL, 2 * GQ + 2 * GV)),
                _const_spec((D_MODEL, LANES)), _const_spec((LANES, GQ)), _const_spec((1, GQ)),
                _const_spec((D_MODEL, DQ)), _const_spec((D_MODEL, DQ)), _const_spec((D_MODEL, DV))]
    return pl.pallas_call(
        functools.partial(_inproj_kernel, transposed=transposed),
        grid=(m // tm,),
        in_specs=in_specs,
        out_specs=out_specs,
        out_shape=out_shape,
        compiler_params=pltpu.CompilerParams(dimension_semantics=("arbitrary",),
                                             vmem_limit_bytes=VMEM_LIMIT),
        name="inproj_t" if transposed else "inproj",
    )(x, mixpre_g, w["gla"], w["gr"], w["a2"], w["ba"], w["dq"], w["dk"], w["dv"])


def _cumsum_rows(tril, x):
    hi = x.astype(BF16)
    r = x - hi.astype(F32)
    mid = r.astype(BF16)
    lo = (r - mid.astype(F32)).astype(BF16)
    return _dot(tril, hi) + _dot(tril, mid) + _dot(tril, lo)


def _gla_exact_scores(q_ref, k_ref, b_ref, a_ref, bi, c, slot, L):
    rows = slice(c * L, (c + 1) * L)
    q = q_ref[bi, rows, :].astype(F32)
    k = k_ref[bi, rows, :].astype(F32)
    b = b_ref[slot]
    row_id = lax.broadcasted_iota(jnp.int32, (L, GQ), 0)
    col_id = lax.broadcasted_iota(jnp.int32, (16, L), 1)
    head_of_lane = lax.broadcasted_iota(jnp.int32, (16, GQ), 1) >> int(math.log2(DK_G))
    head_ind = (head_of_lane == lax.broadcasted_iota(jnp.int32, (16, GQ), 0)).astype(BF16)

    def one_row(t, carry):
        sel = row_id == t
        qt = jnp.sum(jnp.where(sel, q, 0.0), axis=0, keepdims=True)
        bt = jnp.sum(jnp.where(sel, b, 0.0), axis=0, keepdims=True)
        prod = qt * k * jnp.exp(jnp.minimum(bt - b, 0.0))
        hi = prod.astype(BF16)
        lo = (prod - hi.astype(F32)).astype(BF16)
        per_head = _dot_nt(head_ind, hi) + _dot_nt(head_ind, lo)
        per_head = jnp.where(col_id <= t, per_head, 0.0)
        for h in range(H_G):
            a_ref[slot, pl.ds(h * L + t, 1), :] = per_head[h:h + 1, :]
        return carry

    lax.fori_loop(0, L, one_row, 0)


def _gla_kernel(q_ref, k_ref, v_ref, la_ref, gg_ref, g_ref, s0_ref, out_ref, sfin_ref,
                s_ref, a_ref, b_ref, *, L, nchunks, nbatch):
    i = pl.program_id(1)

    @pl.when(i == 0)
    def _():
        s_ref[...] = s0_ref[...]

    causal = (lax.broadcasted_iota(jnp.int32, (H_G * L, L), 1)
              <= (lax.broadcasted_iota(jnp.int32, (H_G * L, L), 0) & (L - 1)))
    lane_head = lax.broadcasted_iota(jnp.int32, (L, GQ), 1) >> int(math.log2(DK_G))
    tr = lax.broadcasted_iota(jnp.int32, (L, L), 0)
    tc = lax.broadcasted_iota(jnp.int32, (L, L), 1)
    tril = (tc <= tr).astype(BF16)

    def stack_heads(x):
        return jnp.concatenate([jnp.where(lane_head == h, x, 0.0) for h in range(H_G)], axis=0).astype(BF16)

    work = [(c, bi, bi * nchunks + c) for c in range(nchunks) for bi in range(nbatch)]
    for c, bi, slot in work:
        rows = slice(c * L, (c + 1) * L)
        b = _cumsum_rows(tril, la_ref[bi, rows, :])
        b_ref[slot] = b
        b_mid = b[L // 2 - 1:L // 2, :]
        q_mid = stack_heads(q_ref[bi, rows, :].astype(F32) * jnp.exp(b - b_mid))
        k_mid = (k_ref[bi, rows, :].astype(F32) * jnp.exp(b_mid - b)).astype(BF16)
        a_ref[slot] = jnp.where(causal, _dot_nt(q_mid, k_mid), 0.0)

    @pl.when(jnp.min(la_ref[...]) < -(2.0 * GLA_EXP_RANGE / L))
    def _():
        for c, bi, slot in work:
            _gla_exact_scores(q_ref, k_ref, b_ref, a_ref, bi, c, slot, L)

    states = [s_ref[bi] for bi in range(nbatch)]
    for c, bi, slot in work:
        rows = slice(c * L, (c + 1) * L)
        s = states[bi]
        b = b_ref[slot]
        b_end = b[L - 1:L, :]
        q = q_ref[bi, rows, :].astype(F32)
        k = k_ref[bi, rows, :].astype(F32)
        v = v_ref[bi, rows, :]
        q_abs = stack_heads(q * jnp.exp(b))
        k_end = k * jnp.exp(b_end - b)
        decay = jnp.exp(b_end)
        a = a_ref[slot].astype(BF16)
        o_state = _dot(q_abs, s.astype(BF16))

        kt = jnp.concatenate([k_end, jnp.broadcast_to(decay, (8, GQ)),
                              jnp.zeros((LANES - L - 8, GQ), F32)], axis=0).T
        kt_b = kt.astype(BF16)
        v_pad = jnp.concatenate([v, jnp.zeros((LANES - L, GV), BF16)], axis=0)
        decay_col = kt[:, L:L + 1]
        states[bi] = jnp.concatenate(
            [decay_col[h * DK_G:(h + 1) * DK_G] * s[h * DK_G:(h + 1) * DK_G]
             + _dot(kt_b[h * DK_G:(h + 1) * DK_G], v_pad[:, h * DV_G:(h + 1) * DV_G])
             for h in range(H_G)], axis=0)

        for h in range(H_G):
            cols = slice(h * DV_G, (h + 1) * DV_G)
            o = _dot(a[h * L:(h + 1) * L], v[:, cols]) + o_state[h * L:(h + 1) * L]
            gate = gg_ref[bi, rows, cols].astype(F32)
            out_ref[bi, rows, cols] = (_rms(o, g_ref[...]) * _silu(gate)).astype(BF16)

    for bi in range(nbatch):
        s_ref[bi] = states[bi]

    @pl.when(i == pl.num_programs(1) - 1)
    def _():
        for bi in range(nbatch):
            sfin_ref[bi] = states[bi]


def _gla_call(gq, gk, gv, la, gg, gla_g, s0, L, rows_per_step, nbatch):
    nb, seq_len = gq.shape[0], gq.shape[1]
    nchunks = rows_per_step // L
    blk = lambda w: pl.BlockSpec((nbatch, rows_per_step, w), lambda b, i: (b, i, 0))
    state = pl.BlockSpec((nbatch, GQ, DV_G), lambda b, i: (b, 0, 0))
    return pl.pallas_call(
        functools.partial(_gla_kernel, L=L, nchunks=nchunks, nbatch=nbatch),
        grid=(nb // nbatch, seq_len // rows_per_step),
        in_specs=[blk(GQ), blk(GQ), blk(GV), blk(GQ), blk(GV),
                  pl.BlockSpec((1, DV_G), lambda b, i: (0, 0)), state],
        out_specs=[blk(GV), state],
        out_shape=[jax.ShapeDtypeStruct((nb, seq_len, GV), BF16),
                   jax.ShapeDtypeStruct((nb, GQ, DV_G), F32)],
        scratch_shapes=[pltpu.VMEM((nbatch, GQ, DV_G), F32), pltpu.VMEM((nbatch * nchunks, H_G * L, L), F32),
                        pltpu.VMEM((nbatch * nchunks, L, GQ), F32)],
        compiler_params=pltpu.CompilerParams(dimension_semantics=("arbitrary", "arbitrary"),
                                             vmem_limit_bytes=VMEM_LIMIT),
        name="gla",
    )(gq, gk, gv, la, gg, gla_g, s0)


def _lambda(lq1_ref, lk1_ref, lq2_ref, lk2_ref, lam_init):
    s1 = jnp.sum(lq1_ref[...] * lk1_ref[...], axis=-1, keepdims=True)
    s2 = jnp.sum(lq2_ref[...] * lk2_ref[...], axis=-1, keepdims=True)
    return jnp.exp(s1) - jnp.exp(s2) + lam_init


def _diff_prompt_kernel(qt_ref, k_ref, vt_ref, bias_ref, lq1_ref, lk1_ref, lq2_ref, lk2_ref, g_ref, o_ref,
                        rhs_ref, m_ref, acc_ref, s_ref, *, lam_init):
    t = ATT_TILE
    i = pl.program_id(2)
    nq = pl.num_programs(2)

    def half_queries(iq):
        qt = qt_ref[0, iq, 0].astype(F32)
        qrow = lax.broadcasted_iota(jnp.int32, (2 * DH_D, t), 0)
        return [jnp.where((qrow >= c * DH_D) & (qrow < (c + 1) * DH_D), qt, 0.0).astype(BF16) for c in range(2)]

    def keys(j):
        return k_ref[0, pl.ds(pl.multiple_of(j * t, t), t), :]

    def scores(j, c):
        return _dot(keys(j), rhs_ref[c])

    def step(j, parity, diagonal):
        vtj = vt_ref[0, j, 0]
        for c in range(2):
            if not diagonal:
                s_ref[1 - parity, c] = scores(j + 1, c)
            st = s_ref[parity, c]
            if diagonal:
                st = st + bias_ref[...]
            m_old = m_ref[c]
            m_new = jnp.maximum(m_old, jnp.max(st, axis=0, keepdims=True))
            alpha = jnp.exp2(m_old - m_new)
            p = jnp.exp2(st - m_new)
            acc_ref[c] = alpha * acc_ref[c] + _dot(vtj, p.astype(BF16))
            m_ref[c] = m_new

    def finish():
        q_next = half_queries(jnp.minimum(i + 1, nq - 1))
        for c in range(2):
            s_ref[0, c] = _dot(keys(0), q_next[c])
        lam = _lambda(lq1_ref, lk1_ref, lq2_ref, lk2_ref, lam_init)
        a1, a2 = acc_ref[0], acc_ref[1]
        ot = a1[:DV_D] * (1.0 / a1[DV_D:DV_D + 1]) - lam * (a2[:DV_D] * (1.0 / a2[DV_D:DV_D + 1]))
        r = lax.rsqrt(jnp.mean(ot * ot, axis=0, keepdims=True) + EPS)
        ot = ot * r * g_ref[...] * (1.0 - lam_init)
        o_ref[0] = ot.T.astype(BF16)

    def pair(n, carry):
        step(2 * n, 0, False)
        step(2 * n + 1, 1, False)
        return carry

    q_now = half_queries(i)
    for c in range(2):
        rhs_ref[c] = q_now[c]

    @pl.when(i == 0)
    def _():
        for c in range(2):
            s_ref[0, c] = scores(0, c)

    m_ref[...] = jnp.full(m_ref.shape, -jnp.inf, F32)
    acc_ref[...] = jnp.zeros(acc_ref.shape, F32)
    lax.fori_loop(0, i // 2, pair, 0)

    @pl.when(i % 2 == 0)
    def _():
        step(i, 0, True)
        finish()

    @pl.when(i % 2 == 1)
    def _():
        step(i - 1, 0, False)
        step(i, 1, True)
        finish()


def _diff_prompt_call(dqt, dkb, dvt, lams, g_col, lam_init):
    nb, seq_len = dkb.shape[0], dkb.shape[1]
    t = ATT_TILE
    chunk_of = jnp.arange(t, dtype=jnp.int32) // CHUNK
    bias = jnp.where(chunk_of[:, None] <= chunk_of[None, :], 0.0, -jnp.inf).astype(F32)
    vec = pl.BlockSpec((1, DH_D), lambda b, h, i: (0, 0))
    return pl.pallas_call(
        functools.partial(_diff_prompt_kernel, lam_init=lam_init),
        grid=(nb, H_D, seq_len // t),
        in_specs=[pl.BlockSpec((1, seq_len // t, 1, 2 * DH_D, t), lambda b, h, i: (b, 0, h, 0, 0)),
                  pl.BlockSpec((1, seq_len, 2 * DH_D), lambda b, h, i: (b, 0, h)),
                  pl.BlockSpec((1, seq_len // t, 1, VT_ROWS, t), lambda b, h, i: (b, 0, h, 0, 0)),
                  pl.BlockSpec((t, t), lambda b, h, i: (0, 0), pipeline_mode=pl.Buffered(1)),
                  vec, vec, vec, vec,
                  pl.BlockSpec((DV_D, 1), lambda b, h, i: (0, 0))],
        out_specs=pl.BlockSpec((1, t, DV_D), lambda b, h, i: (b, i, h)),
        out_shape=jax.ShapeDtypeStruct((nb, seq_len, DV), BF16),
        scratch_shapes=[pltpu.VMEM((2, 2 * DH_D, t), BF16), pltpu.VMEM((2, 1, t), F32),
                        pltpu.VMEM((2, VT_ROWS, t), F32), pltpu.VMEM((2, 2, t, t), F32)],
        compiler_params=pltpu.CompilerParams(dimension_semantics=("arbitrary", "arbitrary", "arbitrary"),
                                             vmem_limit_bytes=VMEM_LIMIT),
        name="diff_prompt",
    )(dqt, dkb, dvt, bias, *lams, g_col)


def _diff_decode_kernel(q_ref, kc_ref, vc_ref, kn_ref, vn_ref, lq1_ref, lk1_ref, lq2_ref, lk2_ref, g_ref,
                        o_ref, *, lam_init, T, P):
    lam = _lambda(lq1_ref, lk1_ref, lq2_ref, lk2_ref, lam_init)
    lane = lax.broadcasted_iota(jnp.int32, (T, 2 * DH_D), 1)
    new_col = lax.broadcasted_iota(jnp.int32, (2 * T, LANES), 1)
    pad = jnp.zeros((LANES - T, 2 * DH_D), BF16)
    for h in range(H_D):
        cols = slice(h * 2 * DH_D, (h + 1) * 2 * DH_D)
        q = q_ref[0, :, cols].astype(F32)
        qs = jnp.concatenate([jnp.where(lane < DH_D, q, 0.0), jnp.where(lane >= DH_D, q, 0.0)],
                             axis=0).astype(BF16)
        kc = kc_ref[0, pl.ds(h, P, stride=H_D), :].astype(BF16)
        vc = vc_ref[0, pl.ds(h, P, stride=H_D), :].astype(BF16)
        kn = jnp.concatenate([kn_ref[0, pl.ds(h, T, stride=H_D), :].astype(BF16), pad], axis=0)
        vn = jnp.concatenate([vn_ref[0, pl.ds(h, T, stride=H_D), :].astype(BF16), pad], axis=0)
        sc = _dot_nt(qs, kc)
        sn = jnp.where(new_col < T, _dot_nt(qs, kn), -jnp.inf)
        m = jnp.maximum(jnp.max(sc, axis=-1, keepdims=True), jnp.max(sn, axis=-1, keepdims=True))
        pc = jnp.exp(sc - m)
        pn = jnp.exp(sn - m)
        l = jnp.sum(pc, axis=-1, keepdims=True) + jnp.sum(pn, axis=-1, keepdims=True)
        o2 = (_dot(pc.astype(BF16), vc) + _dot(pn.astype(BF16), vn)) / l
        o = o2[:T] - lam * o2[T:]
        o_ref[0, :, cols] = (_rms(o, g_ref[...]) * (1.0 - lam_init)).astype(BF16)


def _diff_decode_call(dq, kc, vc, kn, vn, cache_base, lams, g_row, lam_init):
    nb, T, _ = dq.shape
    P = kc.shape[1] // H_D
    new_q = pl.BlockSpec((1, T, DQ), lambda b: (b, 0, 0))
    new_kv = pl.BlockSpec((1, T * H_D, LANES), lambda b: (b, 0, 0))
    cache = pl.BlockSpec((1, P * H_D, LANES), lambda b: (cache_base + b, 0, 0))
    vec = pl.BlockSpec((1, DH_D), lambda b: (0, 0))
    return pl.pallas_call(
        functools.partial(_diff_decode_kernel, lam_init=lam_init, T=T, P=P),
        grid=(nb,),
        in_specs=[new_q, cache, cache, new_kv, new_kv, vec, vec, vec, vec,
                  pl.BlockSpec((1, DV_D), lambda b: (0, 0))],
        out_specs=new_q,
        out_shape=jax.ShapeDtypeStruct((nb, T, DV), BF16),
        compiler_params=pltpu.CompilerParams(dimension_semantics=("arbitrary",),
                                             vmem_limit_bytes=VMEM_LIMIT),
        name="diff_decode",
    )(dq, kc, vc, kn, vn, *lams, g_row)


def kernel(x_prompt, x_sample, state_gla, cache_diff_k, cache_diff_v, w_in, w_gate_a2, b_gate_a, gla_norm_g, lambda_q1, lambda_k1, lambda_q2, lambda_k2, diff_norm_g, w_out, mix_pre_g, mix_post_g, ffn1_pre_g, ffn1_post_g, ffn1_w_gate, ffn1_w_up, ffn1_w_down, ffn2_pre_g, ffn2_post_g, ffn2_w_gate, ffn2_w_up, ffn2_w_down):
    B, S, _ = x_prompt.shape
    Bs, T, _ = x_sample.shape
    depth = w_in.shape[0]
    P = cache_diff_k.shape[2]
    xp = x_prompt.reshape(B * S, D_MODEL)
    xs = x_sample.reshape(Bs * T, D_MODEL)
    cache_k = cache_diff_k.reshape(depth * Bs, P * H_D, LANES)
    cache_v = cache_diff_v.reshape(depth * Bs, P * H_D, LANES)
    outs = [[] for _ in range(6)]
    for l in range(depth):
        lam_init = 0.8 - 0.6 * math.exp(-0.3 * l)
        row = lambda v: v[l].reshape(1, -1)
        o_gr = 2 * GQ + GV
        o_gg = o_gr + GATE_RANK
        o_dq = o_gg + GV
        wi = w_in[l]
        w = {
            "gla": jnp.concatenate([wi[:, :o_gr], wi[:, o_gg:o_dq]], axis=1).astype(BF16),
            "gr": jnp.pad(wi[:, o_gr:o_gg], ((0, 0), (0, LANES - GATE_RANK))).astype(BF16),
            "a2": jnp.pad(w_gate_a2[l], ((0, LANES - GATE_RANK), (0, 0))).astype(BF16),
            "ba": row(b_gate_a),
            "dq": wi[:, o_dq:o_dq + DQ].astype(BF16),
            "dk": wi[:, o_dq + DQ:o_dq + 2 * DQ].astype(BF16),
            "dv": wi[:, o_dq + 2 * DQ:].astype(BF16),
        }
        ffn1 = (row(ffn1_pre_g), row(ffn1_post_g), ffn1_w_gate[l].astype(BF16),
                ffn1_w_up[l].astype(BF16), ffn1_w_down[l].astype(BF16))
        ffn2 = (row(ffn2_pre_g), row(ffn2_post_g), ffn2_w_gate[l].astype(BF16),
                ffn2_w_up[l].astype(BF16), ffn2_w_down[l].astype(BF16))
        wo = w_out[l].astype(BF16)
        wog, wod = wo[:GV], wo[GV:]
        lams = (row(lambda_q1), row(lambda_k1), row(lambda_q2), row(lambda_k2))
        gla_g = row(gla_norm_g)

        x1 = _ffn_call(xp, *ffn1)
        gq, gk, gv, gg, la, dqt, dkf, dkb, dvf, dvt = _inproj_call(x1, row(mix_pre_g), w, S, True)
        b3 = lambda a: a.reshape(B, S, a.shape[-1])
        g_out, s_p = _gla_call(b3(gq), b3(gk), b3(gv), b3(la), b3(gg), gla_g,
                               jnp.zeros((B, GQ, DV_G), F32), CHUNK, GLA_ROWS, GLA_PROMPT_BATCH)
        d_out = _diff_prompt_call(dqt, b3(dkb), dvt, lams, diff_norm_g[l].reshape(DV_D, 1), lam_init)
        xp = _ffn_call(x1, *ffn2, merge=(g_out.reshape(B * S, GV), d_out.reshape(B * S, DV),
                                         wog, wod, row(mix_post_g)))
        outs[0].append(s_p.reshape(B, H_G, DK_G, DV_G))
        outs[1].append(dkf.reshape(B, S, H_D, 2 * DH_D))
        outs[2].append(dvf.reshape(B, S, H_D, DV_D))

        x1 = _ffn_call(xs, *ffn1)
        gq, gk, gv, gg, la, dq, dkf, dkb, dvf = _inproj_call(x1, row(mix_pre_g), w, T, False)
        b3 = lambda a: a.reshape(Bs, T, a.shape[-1])
        g_out, s_s = _gla_call(b3(gq), b3(gk), b3(gv), b3(la), b3(gg), gla_g,
                               state_gla[l].reshape(Bs, GQ, DV_G), T, T, GLA_DECODE_BATCH)
        d_out = _diff_decode_call(b3(dq), cache_k, cache_v, dkf.reshape(Bs, T * H_D, LANES),
                                  dvf.reshape(Bs, T * H_D, LANES), l * Bs, lams, row(diff_norm_g), lam_init)
        xs = _ffn_call(x1, *ffn2, merge=(g_out.reshape(Bs * T, GV), d_out.reshape(Bs * T, DV),
                                         wog, wod, row(mix_post_g)))
        outs[3].append(s_s.reshape(Bs, H_G, DK_G, DV_G))
        outs[4].append(dkf.reshape(Bs, T, H_D, 2 * DH_D))
        outs[5].append(dvf.reshape(Bs, T, H_D, DV_D))

    sg_p, k_p, v_p, sg_s, k_s, v_s = (jnp.stack(o) for o in outs)
    return (xp.reshape(B, S, D_MODEL), xs.reshape(Bs, T, D_MODEL), sg_p, k_p, v_p, sg_s, k_s, v_s)
```

```python
import functools
import math

import jax
import jax.numpy as jnp
from jax import lax
from jax.experimental import pallas as pl
from jax.experimental.pallas import tpu as pltpu

F32 = jnp.float32
BF16 = jnp.bfloat16

D_MODEL = 1024
D_FF = 2816
CHUNK = 64
H_G, DK_G, DV_G = 4, 64, 128
GATE_RANK = 16
GATE_TAU = 16.0
H_D, DH_D, DV_D = 4, 64, 128
EPS = 1e-6

GQ = H_G * DK_G
GV = H_G * DV_G
DQ = H_D * 2 * DH_D
DV = H_D * DV_D

LANES = 128
MXU_N = 256
TOKEN_TILE = 512
FFN_TILE = 1024
FFN_SUB = 512
FF_TILE = MXU_N
ATT_TILE = 512
ATT_HEADS = 2
VT_ROWS = DV_D + 16
GLA_ROWS = 256
GLA_PROMPT_BATCH = 8
GLA_DECODE_BATCH = 8
GLA_EXP_RANGE = 80.0
VMEM_LIMIT = 56 * 1024 * 1024


def _dot(a, b):
    return jnp.dot(a, b, preferred_element_type=F32)


def _dot_nt(a, b):
    return lax.dot_general(a, b, (((1,), (1,)), ((), ())), preferred_element_type=F32)


def _rms(x, g):
    r = lax.rsqrt(jnp.mean(x * x, axis=-1, keepdims=True) + EPS)
    return x * r * g


def _silu(x):
    return x * jax.nn.sigmoid(x)


def _const_spec(shape):
    nd = len(shape)
    return pl.BlockSpec(shape, lambda *_: (0,) * nd, pipeline_mode=pl.Buffered(1))


def _swiglu_half(x, pre_ref, post_ref, wg_ref, wu_ref, wd_ref, h_ref, act_ref):
    h_ref[...] = _rms(x, pre_ref[...]).astype(BF16)
    for j in range(D_FF // FF_TILE):
        cols = slice(j * FF_TILE, (j + 1) * FF_TILE)
        h = h_ref[...]
        gate = _dot(h, wg_ref[:, cols])
        up = _dot(h, wu_ref[:, cols])
        act_ref[:, cols] = (_silu(gate) * up).astype(BF16)
    f = _dot(act_ref[...], wd_ref[...])
    return x + 0.5 * _rms(f, post_ref[...])


def _sub_tiles(ref):
    return [slice(s * FFN_SUB, (s + 1) * FFN_SUB) for s in range(ref.shape[0] // FFN_SUB)]


def _ffn_kernel(x_ref, pre_ref, post_ref, wg_ref, wu_ref, wd_ref, o_ref, h_ref, act_ref):
    for s, rows in enumerate(_sub_tiles(x_ref)):
        o_ref[rows, :] = _swiglu_half(x_ref[rows, :], pre_ref, post_ref, wg_ref, wu_ref, wd_ref,
                                      h_ref.at[s], act_ref.at[s])


def _merge_ffn_kernel(x_ref, g_ref, d_ref, wog_ref, wod_ref, mixg_ref,
                      pre_ref, post_ref, wg_ref, wu_ref, wd_ref, o_ref, h_ref, act_ref, x2_ref):
    for s, rows in enumerate(_sub_tiles(x_ref)):
        y = _dot(g_ref[rows, :], wog_ref[...]) + _dot(d_ref[rows, :], wod_ref[...])
        x2_ref[s] = x_ref[rows, :] + _rms(y, mixg_ref[...])
    for s, rows in enumerate(_sub_tiles(x_ref)):
        o_ref[rows, :] = _swiglu_half(x2_ref[s], pre_ref, post_ref, wg_ref, wu_ref, wd_ref,
                                      h_ref.at[s], act_ref.at[s])


def _ffn_call(x, pre_g, post_g, wg, wu, wd, merge=None):
    m = x.shape[0]
    tm = min(FFN_TILE, m)
    nsub = tm // FFN_SUB
    row = lambda w: pl.BlockSpec((tm, w), lambda i: (i, 0))
    ffn_specs = [_const_spec((1, D_MODEL)), _const_spec((1, D_MODEL)),
                 _const_spec((D_MODEL, D_FF)), _const_spec((D_MODEL, D_FF)), _const_spec((D_FF, D_MODEL))]
    scratch = [pltpu.VMEM((nsub, FFN_SUB, D_MODEL), BF16), pltpu.VMEM((nsub, FFN_SUB, D_FF), BF16)]
    if merge is None:
        kern, ins = _ffn_kernel, (x, pre_g, post_g, wg, wu, wd)
        in_specs = [row(D_MODEL)] + ffn_specs
    else:
        g_out, d_out, wog, wod, mix_g = merge
        kern, ins = _merge_ffn_kernel, (x, g_out, d_out, wog, wod, mix_g, pre_g, post_g, wg, wu, wd)
        in_specs = ([row(D_MODEL), row(GV), row(DV), _const_spec((GV, D_MODEL)),
                     _const_spec((DV, D_MODEL)), _const_spec((1, D_MODEL))] + ffn_specs)
        scratch = scratch + [pltpu.VMEM((nsub, FFN_SUB, D_MODEL), F32)]
    return pl.pallas_call(
        kern,
        grid=(m // tm,),
        in_specs=in_specs,
        out_specs=row(D_MODEL),
        out_shape=jax.ShapeDtypeStruct((m, D_MODEL), F32),
        scratch_shapes=scratch,
        compiler_params=pltpu.CompilerParams(dimension_semantics=("arbitrary",),
                                             vmem_limit_bytes=VMEM_LIMIT),
        name="merge_ffn" if merge is not None else "ffn",
    )(*ins)


def _store_heads(ref, x, row0):
    rows = x.shape[0]
    for h in range(H_D):
        ref[pl.ds(row0 * H_D + h, rows, stride=H_D), :] = x[:, h * LANES:(h + 1) * LANES]


def _inproj_kernel(x_ref, mixpre_ref, wgla_ref, wgr_ref, wa2_ref, ba_ref, wdq_ref, wdk_ref, wdv_ref,
                   gq_ref, gk_ref, gv_ref, gg_ref, la_ref, dq_ref, dkf_ref, dkb_ref, dvf_ref,
                   *maybe_dvt_ref, transposed):
    t = ATT_TILE
    for s in range(x_ref.shape[0] // t):
        rows = slice(s * t, (s + 1) * t)
        h = _rms(x_ref[rows, :], mixpre_ref[...]).astype(BF16)
        dv = _dot(h, wdv_ref[...])
        _store_heads(dvf_ref, dv, s * t)
        dq = _dot(h, wdq_ref[...])
        if transposed:
            (dvt_ref,) = maybe_dvt_ref
            dvt = dv.T.astype(BF16)
            dqt = (dq * (DH_D ** -0.5 * math.log2(math.e))).T.astype(BF16)
            for hd in range(H_D):
                dq_ref[0, s, hd] = dqt[hd * 2 * DH_D:(hd + 1) * 2 * DH_D]
                dvt_ref[0, s, hd, :DV_D, :] = dvt[hd * DV_D:(hd + 1) * DV_D]
                dvt_ref[0, s, hd, DV_D:, :] = jnp.ones((VT_ROWS - DV_D, t), BF16)
        else:
            dq_ref[rows, :] = (dq * (DH_D ** -0.5)).astype(BF16)
        dk = _dot(h, wdk_ref[...])
        _store_heads(dkf_ref, dk, s * t)
        dkb_ref[rows, :] = dk.astype(BF16)
        gr = _dot(h, wgr_ref[...]).astype(BF16)
        a = _dot(gr, wa2_ref[...]) + ba_ref[...]
        la_ref[rows, :] = (jnp.minimum(a, 0.0) - jnp.log1p(jnp.exp(-jnp.abs(a)))) / GATE_TAU
        gq_ref[rows, :] = (_dot(h, wgla_ref[:, 0:GQ]) * (DK_G ** -0.5)).astype(BF16)
        gk_ref[rows, :] = _dot(h, wgla_ref[:, GQ:2 * GQ]).astype(BF16)
        gv_ref[rows, :] = _dot(h, wgla_ref[:, 2 * GQ:2 * GQ + GV]).astype(BF16)
        gg_ref[rows, :] = _dot(h, wgla_ref[:, 2 * GQ + GV:2 * GQ + 2 * GV]).astype(BF16)


def _inproj_call(x, mixpre_g, w, seq_len, transposed):
    m = x.shape[0]
    tm = min(TOKEN_TILE, m)
    nb = m // seq_len
    row = lambda wd: pl.BlockSpec((tm, wd), lambda i: (i, 0))
    heads = pl.BlockSpec((tm * H_D, LANES), lambda i: (i, 0))
    out_shape = [jax.ShapeDtypeStruct((m, GQ), BF16), jax.ShapeDtypeStruct((m, GQ), BF16),
                 jax.ShapeDtypeStruct((m, GV), BF16), jax.ShapeDtypeStruct((m, GV), BF16),
                 jax.ShapeDtypeStruct((m, GQ), F32)]
    out_specs = [row(GQ), row(GQ), row(GV), row(GV), row(GQ)]
    if transposed:
        t = ATT_TILE
        nq, per_step = seq_len // t, tm // t
        spb = nq // per_step
        out_shape.append(jax.ShapeDtypeStruct((nb, nq, H_D, 2 * DH_D, t), BF16))
        out_specs.append(pl.BlockSpec((1, per_step, H_D, 2 * DH_D, t), lambda i: (i // spb, i % spb, 0, 0, 0)))
    else:
        out_shape.append(jax.ShapeDtypeStruct((m, DQ), BF16))
        out_specs.append(row(DQ))
    out_shape += [jax.ShapeDtypeStruct((m * H_D, LANES), F32), jax.ShapeDtypeStruct((m, DQ), BF16),
                  jax.ShapeDtypeStruct((m * H_D, LANES), F32)]
    out_specs += [heads, row(DQ), heads]
    if transposed:
        out_shape.append(jax.ShapeDtypeStruct((nb, nq, H_D, VT_ROWS, t), BF16))
        out_specs.append(pl.BlockSpec((1, per_step, H_D, VT_ROWS, t), lambda i: (i // spb, i % spb, 0, 0, 0)))
    in_specs = [row(D_MODEL), _const_spec((1, D_MODEL)), _const_spec((D_MODEL, 2 * GQ + 2 * GV)),
                _const_spec((D_MODEL, LANES)), _const_spec((LANES, GQ)), _const_spec((1, GQ)),
                _const_spec((D_MODEL, DQ)), _const_spec((D_MODEL, DQ)), _const_spec((D_MODEL, DV))]
    return pl.pallas_call(
        functools.partial(_inproj_kernel, transposed=transposed),
        grid=(m // tm,),
        in_specs=in_specs,
        out_specs=out_specs,
        out_shape=out_shape,
        compiler_params=pltpu.CompilerParams(dimension_semantics=("arbitrary",),
                                             vmem_limit_bytes=VMEM_LIMIT),
        name="inproj_t" if transposed else "inproj",
    )(x, mixpre_g, w["gla"], w["gr"], w["a2"], w["ba"], w["dq"], w["dk"], w["dv"])


def _cumsum_rows(tril, x):
    hi = x.astype(BF16)
    r = x - hi.astype(F32)
    mid = r.astype(BF16)
    lo = (r - mid.astype(F32)).astype(BF16)
    return _dot(tril, hi) + _dot(tril, mid) + _dot(tril, lo)


def _gla_exact_scores(q_ref, k_ref, b_ref, a_ref, bi, c, slot, L):
    rows = slice(c * L, (c + 1) * L)
    q = q_ref[bi, rows, :].astype(F32)
    k = k_ref[bi, rows, :].astype(F32)
    b = b_ref[slot]
    row_id = lax.broadcasted_iota(jnp.int32, (L, GQ), 0)
    col_id = lax.broadcasted_iota(jnp.int32, (16, L), 1)
    head_of_lane = lax.broadcasted_iota(jnp.int32, (16, GQ), 1) >> int(math.log2(DK_G))
    head_ind = (head_of_lane == lax.broadcasted_iota(jnp.int32, (16, GQ), 0)).astype(BF16)

    def one_row(t, carry):
        sel = row_id == t
        qt = jnp.sum(jnp.where(sel, q, 0.0), axis=0, keepdims=True)
        bt = jnp.sum(jnp.where(sel, b, 0.0), axis=0, keepdims=True)
        prod = qt * k * jnp.exp(jnp.minimum(bt - b, 0.0))
        hi = prod.astype(BF16)
        lo = (prod - hi.astype(F32)).astype(BF16)
        per_head = _dot_nt(head_ind, hi) + _dot_nt(head_ind, lo)
        per_head = jnp.where(col_id <= t, per_head, 0.0)
        for h in range(H_G):
            a_ref[slot, pl.ds(h * L + t, 1), :] = per_head[h:h + 1, :]
        return carry

    lax.fori_loop(0, L, one_row, 0)


def _gla_kernel(q_ref, k_ref, v_ref, la_ref, gg_ref, g_ref, s0_ref, out_ref, sfin_ref,
                s_ref, a_ref, b_ref, *, L, nchunks, nbatch):
    i = pl.program_id(1)

    @pl.when(i == 0)
    def _():
        s_ref[...] = s0_ref[...]

    causal = (lax.broadcasted_iota(jnp.int32, (H_G * L, L), 1)
              <= (lax.broadcasted_iota(jnp.int32, (H_G * L, L), 0) & (L - 1)))
    lane_head = lax.broadcasted_iota(jnp.int32, (L, GQ), 1) >> int(math.log2(DK_G))
    tr = lax.broadcasted_iota(jnp.int32, (L, L), 0)
    tc = lax.broadcasted_iota(jnp.int32, (L, L), 1)
    tril = (tc <= tr).astype(BF16)

    def stack_heads(x):
        return jnp.concatenate([jnp.where(lane_head == h, x, 0.0) for h in range(H_G)], axis=0).astype(BF16)

    work = [(c, bi, bi * nchunks + c) for c in range(nchunks) for bi in range(nbatch)]
    for c, bi, slot in work:
        rows = slice(c * L, (c + 1) * L)
        b = _cumsum_rows(tril, la_ref[bi, rows, :])
        b_ref[slot] = b
        b_mid = b[L // 2 - 1:L // 2, :]
        q_mid = stack_heads(q_ref[bi, rows, :].astype(F32) * jnp.exp(b - b_mid))
        k_mid = (k_ref[bi, rows, :].astype(F32) * jnp.exp(b_mid - b)).astype(BF16)
        a_ref[slot] = jnp.where(causal, _dot_nt(q_mid, k_mid), 0.0)

    @pl.when(jnp.min(la_ref[...]) < -(2.0 * GLA_EXP_RANGE / L))
    def _():
        for c, bi, slot in work:
            _gla_exact_scores(q_ref, k_ref, b_ref, a_ref, bi, c, slot, L)

    states = [s_ref[bi] for bi in range(nbatch)]
    for c, bi, slot in work:
        rows = slice(c * L, (c + 1) * L)
        s = states[bi]
        b = b_ref[slot]
        b_end = b[L - 1:L, :]
        q = q_ref[bi, rows, :].astype(F32)
        k = k_ref[bi, rows, :].astype(F32)
        v = v_ref[bi, rows, :]
        q_abs = stack_heads(q * jnp.exp(b))
        k_end = k * jnp.exp(b_end - b)
        decay = jnp.exp(b_end)
        a = a_ref[slot].astype(BF16)
        o_state = _dot(q_abs, s.astype(BF16))

        kt = jnp.concatenate([k_end, jnp.broadcast_to(decay, (8, GQ)),
                              jnp.zeros((LANES - L - 8, GQ), F32)], axis=0).T
        kt_b = kt.astype(BF16)
        v_pad = jnp.concatenate([v, jnp.zeros((LANES - L, GV), BF16)], axis=0)
        decay_col = kt[:, L:L + 1]
        states[bi] = jnp.concatenate(
            [decay_col[h * DK_G:(h + 1) * DK_G] * s[h * DK_G:(h + 1) * DK_G]
             + _dot(kt_b[h * DK_G:(h + 1) * DK_G], v_pad[:, h * DV_G:(h + 1) * DV_G])
             for h in range(H_G)], axis=0)

        for h in range(H_G):
            cols = slice(h * DV_G, (h + 1) * DV_G)
            o = _dot(a[h * L:(h + 1) * L], v[:, cols]) + o_state[h * L:(h + 1) * L]
            gate = gg_ref[bi, rows, cols].astype(F32)
            out_ref[bi, rows, cols] = (_rms(o, g_ref[...]) * _silu(gate)).astype(BF16)

    for bi in range(nbatch):
        s_ref[bi] = states[bi]

    @pl.when(i == pl.num_programs(1) - 1)
    def _():
        for bi in range(nbatch):
            sfin_ref[bi] = states[bi]


def _gla_call(gq, gk, gv, la, gg, gla_g, s0, L, rows_per_step, nbatch):
    nb, seq_len = gq.shape[0], gq.shape[1]
    nchunks = rows_per_step // L
    blk = lambda w: pl.BlockSpec((nbatch, rows_per_step, w), lambda b, i: (b, i, 0))
    state = pl.BlockSpec((nbatch, GQ, DV_G), lambda b, i: (b, 0, 0))
    return pl.pallas_call(
        functools.partial(_gla_kernel, L=L, nchunks=nchunks, nbatch=nbatch),
        grid=(nb // nbatch, seq_len // rows_per_step),
        in_specs=[blk(GQ), blk(GQ), blk(GV), blk(GQ), blk(GV),
                  pl.BlockSpec((1, DV_G), lambda b, i: (0, 0)), state],
        out_specs=[blk(GV), state],
        out_shape=[jax.ShapeDtypeStruct((nb, seq_len, GV), BF16),
                   jax.ShapeDtypeStruct((nb, GQ, DV_G), F32)],
        scratch_shapes=[pltpu.VMEM((nbatch, GQ, DV_G), F32), pltpu.VMEM((nbatch * nchunks, H_G * L, L), F32),
                        pltpu.VMEM((nbatch * nchunks, L, GQ), F32)],
        compiler_params=pltpu.CompilerParams(dimension_semantics=("arbitrary", "arbitrary"),
                                             vmem_limit_bytes=VMEM_LIMIT),
        name="gla",
    )(gq, gk, gv, la, gg, gla_g, s0)


def _lambda(lq1_ref, lk1_ref, lq2_ref, lk2_ref, lam_init):
    s1 = jnp.sum(lq1_ref[...] * lk1_ref[...], axis=-1, keepdims=True)
    s2 = jnp.sum(lq2_ref[...] * lk2_ref[...], axis=-1, keepdims=True)
    return jnp.exp(s1) - jnp.exp(s2) + lam_init


def _diff_prompt_kernel(qt_ref, k_ref, vt_ref, bias_ref, lq1_ref, lk1_ref, lq2_ref, lk2_ref, g_ref, o_ref,
                        rhs_ref, m_ref, acc_ref, s_ref, *, lam_init):
    t = ATT_TILE
    i = pl.program_id(2)
    nq = pl.num_programs(2)

    chains = range(2 * ATT_HEADS)

    def half_queries(iq):
        qrow = lax.broadcasted_iota(jnp.int32, (2 * DH_D, t), 0)
        out = []
        for hh in range(ATT_HEADS):
            qt = qt_ref[0, iq, hh].astype(F32)
            out += [jnp.where((qrow >= c * DH_D) & (qrow < (c + 1) * DH_D), qt, 0.0).astype(BF16) for c in range(2)]
        return out

    def keys(j, hh):
        return k_ref[0, pl.ds(pl.multiple_of(j * t, t), t), hh * 2 * DH_D:(hh + 1) * 2 * DH_D]

    def scores(j, ch):
        return _dot(keys(j, ch // 2), rhs_ref[ch])

    def step(j, parity, diagonal):
        for ch in chains:
            if not diagonal:
                s_ref[1 - parity, ch] = scores(j + 1, ch)
            st = s_ref[parity, ch]
            if diagonal:
                st = st + bias_ref[...]
            m_old = m_ref[ch]
            m_new = jnp.maximum(m_old, jnp.max(st, axis=0, keepdims=True))
            alpha = jnp.exp2(m_old - m_new)
            p = jnp.exp2(st - m_new)
            acc_ref[ch] = alpha * acc_ref[ch] + _dot(vt_ref[0, j, ch // 2], p.astype(BF16))
            m_ref[ch] = m_new

    def finish():
        q_next = half_queries(jnp.minimum(i + 1, nq - 1))
        for ch in chains:
            s_ref[0, ch] = _dot(keys(0, ch // 2), q_next[ch])
        lam = _lambda(lq1_ref, lk1_ref, lq2_ref, lk2_ref, lam_init)
        for hh in range(ATT_HEADS):
            a1, a2 = acc_ref[2 * hh], acc_ref[2 * hh + 1]
            ot = a1[:DV_D] * (1.0 / a1[DV_D:DV_D + 1]) - lam * (a2[:DV_D] * (1.0 / a2[DV_D:DV_D + 1]))
            r = lax.rsqrt(jnp.mean(ot * ot, axis=0, keepdims=True) + EPS)
            ot = ot * r * g_ref[...] * (1.0 - lam_init)
            o_ref[0, :, hh * DV_D:(hh + 1) * DV_D] = ot.T.astype(BF16)

    def pair(n, carry):
        step(2 * n, 0, False)
        step(2 * n + 1, 1, False)
        return carry

    q_now = half_queries(i)
    for ch in chains:
        rhs_ref[ch] = q_now[ch]

    @pl.when(i == 0)
    def _():
        for ch in chains:
            s_ref[0, ch] = scores(0, ch)

    m_ref[...] = jnp.full(m_ref.shape, -jnp.inf, F32)
    acc_ref[...] = jnp.zeros(acc_ref.shape, F32)
    lax.fori_loop(0, i // 2, pair, 0)

    @pl.when(i % 2 == 0)
    def _():
        step(i, 0, True)
        finish()

    @pl.when(i % 2 == 1)
    def _():
        step(i - 1, 0, False)
        step(i, 1, True)
        finish()


def _diff_prompt_call(dqt, dkb, dvt, lams, g_col, lam_init):
    nb, seq_len = dkb.shape[0], dkb.shape[1]
    t, nh = ATT_TILE, ATT_HEADS
    chunk_of = jnp.arange(t, dtype=jnp.int32) // CHUNK
    bias = jnp.where(chunk_of[:, None] <= chunk_of[None, :], 0.0, -jnp.inf).astype(F32)
    vec = pl.BlockSpec((1, DH_D), lambda b, h, i: (0, 0))
    return pl.pallas_call(
        functools.partial(_diff_prompt_kernel, lam_init=lam_init),
        grid=(nb, H_D // nh, seq_len // t),
        in_specs=[pl.BlockSpec((1, seq_len // t, nh, 2 * DH_D, t), lambda b, h, i: (b, 0, h, 0, 0)),
                  pl.BlockSpec((1, seq_len, nh * 2 * DH_D), lambda b, h, i: (b, 0, h)),
                  pl.BlockSpec((1, seq_len // t, nh, VT_ROWS, t), lambda b, h, i: (b, 0, h, 0, 0)),
                  pl.BlockSpec((t, t), lambda b, h, i: (0, 0), pipeline_mode=pl.Buffered(1)),
                  vec, vec, vec, vec,
                  pl.BlockSpec((DV_D, 1), lambda b, h, i: (0, 0))],
        out_specs=pl.BlockSpec((1, t, nh * DV_D), lambda b, h, i: (b, i, h)),
        out_shape=jax.ShapeDtypeStruct((nb, seq_len, DV), BF16),
        scratch_shapes=[pltpu.VMEM((2 * nh, 2 * DH_D, t), BF16), pltpu.VMEM((2 * nh, 1, t), F32),
                        pltpu.VMEM((2 * nh, VT_ROWS, t), F32), pltpu.VMEM((2, 2 * nh, t, t), F32)],
        compiler_params=pltpu.CompilerParams(dimension_semantics=("arbitrary", "arbitrary", "arbitrary"),
                                             vmem_limit_bytes=VMEM_LIMIT),
        name="diff_prompt",
    )(dqt, dkb, dvt, bias, *lams, g_col)


def _diff_decode_kernel(q_ref, kc_ref, vc_ref, kn_ref, vn_ref, lq1_ref, lk1_ref, lq2_ref, lk2_ref, g_ref,
                        o_ref, *, lam_init, T, P):
    lam = _lambda(lq1_ref, lk1_ref, lq2_ref, lk2_ref, lam_init)
    lane = lax.broadcasted_iota(jnp.int32, (T, 2 * DH_D), 1)
    new_col = lax.broadcasted_iota(jnp.int32, (2 * T, LANES), 1)
    pad = jnp.zeros((LANES - T, 2 * DH_D), BF16)
    for h in range(H_D):
        cols = slice(h * 2 * DH_D, (h + 1) * 2 * DH_D)
        q = q_ref[0, :, cols].astype(F32)
        qs = jnp.concatenate([jnp.where(lane < DH_D, q, 0.0), jnp.where(lane >= DH_D, q, 0.0)],
                             axis=0).astype(BF16)
        kc = kc_ref[0, pl.ds(h, P, stride=H_D), :].astype(BF16)
        vc = vc_ref[0, pl.ds(h, P, stride=H_D), :].astype(BF16)
        kn = jnp.concatenate([kn_ref[0, pl.ds(h, T, stride=H_D), :].astype(BF16), pad], axis=0)
        vn = jnp.concatenate([vn_ref[0, pl.ds(h, T, stride=H_D), :].astype(BF16), pad], axis=0)
        sc = _dot_nt(qs, kc)
        sn = jnp.where(new_col < T, _dot_nt(qs, kn), -jnp.inf)
        m = jnp.maximum(jnp.max(sc, axis=-1, keepdims=True), jnp.max(sn, axis=-1, keepdims=True))
        pc = jnp.exp(sc - m)
        pn = jnp.exp(sn - m)
        l = jnp.sum(pc, axis=-1, keepdims=True) + jnp.sum(pn, axis=-1, keepdims=True)
        o2 = (_dot(pc.astype(BF16), vc) + _dot(pn.astype(BF16), vn)) / l
        o = o2[:T] - lam * o2[T:]
        o_ref[0, :, cols] = (_rms(o, g_ref[...]) * (1.0 - lam_init)).astype(BF16)


def _diff_decode_call(dq, kc, vc, kn, vn, cache_base, lams, g_row, lam_init):
    nb, T, _ = dq.shape
    P = kc.shape[1] // H_D
    new_q = pl.BlockSpec((1, T, DQ), lambda b: (b, 0, 0))
    new_kv = pl.BlockSpec((1, T * H_D, LANES), lambda b: (b, 0, 0))
    cache = pl.BlockSpec((1, P * H_D, LANES), lambda b: (cache_base + b, 0, 0))
    vec = pl.BlockSpec((1, DH_D), lambda b: (0, 0))
    return pl.pallas_call(
        functools.partial(_diff_decode_kernel, lam_init=lam_init, T=T, P=P),
        grid=(nb,),
        in_specs=[new_q, cache, cache, new_kv, new_kv, vec, vec, vec, vec,
                  pl.BlockSpec((1, DV_D), lambda b: (0, 0))],
        out_specs=new_q,
        out_shape=jax.ShapeDtypeStruct((nb, T, DV), BF16),
        compiler_params=pltpu.CompilerParams(dimension_semantics=("arbitrary",),
                                             vmem_limit_bytes=VMEM_LIMIT),
        name="diff_decode",
    )(dq, kc, vc, kn, vn, *lams, g_row)


def kernel(x_prompt, x_sample, state_gla, cache_diff_k, cache_diff_v, w_in, w_gate_a2, b_gate_a, gla_norm_g, lambda_q1, lambda_k1, lambda_q2, lambda_k2, diff_norm_g, w_out, mix_pre_g, mix_post_g, ffn1_pre_g, ffn1_post_g, ffn1_w_gate, ffn1_w_up, ffn1_w_down, ffn2_pre_g, ffn2_post_g, ffn2_w_gate, ffn2_w_up, ffn2_w_down):
    B, S, _ = x_prompt.shape
    Bs, T, _ = x_sample.shape
    depth = w_in.shape[0]
    P = cache_diff_k.shape[2]
    xp = x_prompt.reshape(B * S, D_MODEL)
    xs = x_sample.reshape(Bs * T, D_MODEL)
    cache_k = cache_diff_k.reshape(depth * Bs, P * H_D, LANES)
    cache_v = cache_diff_v.reshape(depth * Bs, P * H_D, LANES)
    outs = [[] for _ in range(6)]
    for l in range(depth):
        lam_init = 0.8 - 0.6 * math.exp(-0.3 * l)
        row = lambda v: v[l].reshape(1, -1)
        o_gr = 2 * GQ + GV
        o_gg = o_gr + GATE_RANK
        o_dq = o_gg + GV
        wi = w_in[l]
        w = {
            "gla": jnp.concatenate([wi[:, :o_gr], wi[:, o_gg:o_dq]], axis=1).astype(BF16),
            "gr": jnp.pad(wi[:, o_gr:o_gg], ((0, 0), (0, LANES - GATE_RANK))).astype(BF16),
            "a2": jnp.pad(w_gate_a2[l], ((0, LANES - GATE_RANK), (0, 0))).astype(BF16),
            "ba": row(b_gate_a),
            "dq": wi[:, o_dq:o_dq + DQ].astype(BF16),
            "dk": wi[:, o_dq + DQ:o_dq + 2 * DQ].astype(BF16),
            "dv": wi[:, o_dq + 2 * DQ:].astype(BF16),
        }
        ffn1 = (row(ffn1_pre_g), row(ffn1_post_g), ffn1_w_gate[l].astype(BF16),
                ffn1_w_up[l].astype(BF16), ffn1_w_down[l].astype(BF16))
        ffn2 = (row(ffn2_pre_g), row(ffn2_post_g), ffn2_w_gate[l].astype(BF16),
                ffn2_w_up[l].astype(BF16), ffn2_w_down[l].astype(BF16))
        wo = w_out[l].astype(BF16)
        wog, wod = wo[:GV], wo[GV:]
        lams = (row(lambda_q1), row(lambda_k1), row(lambda_q2), row(lambda_k2))
        gla_g = row(gla_norm_g)

        x1 = _ffn_call(xp, *ffn1)
        gq, gk, gv, gg, la, dqt, dkf, dkb, dvf, dvt = _inproj_call(x1, row(mix_pre_g), w, S, True)
        b3 = lambda a: a.reshape(B, S, a.shape[-1])
        g_out, s_p = _gla_call(b3(gq), b3(gk), b3(gv), b3(la), b3(gg), gla_g,
                               jnp.zeros((B, GQ, DV_G), F32), CHUNK, GLA_ROWS, GLA_PROMPT_BATCH)
        d_out = _diff_prompt_call(dqt, b3(dkb), dvt, lams, diff_norm_g[l].reshape(DV_D, 1), lam_init)
        xp = _ffn_call(x1, *ffn2, merge=(g_out.reshape(B * S, GV), d_out.reshape(B * S, DV),
                                         wog, wod, row(mix_post_g)))
        outs[0].append(s_p.reshape(B, H_G, DK_G, DV_G))
        outs[1].append(dkf.reshape(B, S, H_D, 2 * DH_D))
        outs[2].append(dvf.reshape(B, S, H_D, DV_D))

        x1 = _ffn_call(xs, *ffn1)
        gq, gk, gv, gg, la, dq, dkf, dkb, dvf = _inproj_call(x1, row(mix_pre_g), w, T, False)
        b3 = lambda a: a.reshape(Bs, T, a.shape[-1])
        g_out, s_s = _gla_call(b3(gq), b3(gk), b3(gv), b3(la), b3(gg), gla_g,
                               state_gla[l].reshape(Bs, GQ, DV_G), T, T, GLA_DECODE_BATCH)
        d_out = _diff_decode_call(b3(dq), cache_k, cache_v, dkf.reshape(Bs, T * H_D, LANES),
                                  dvf.reshape(Bs, T * H_D, LANES), l * Bs, lams, row(diff_norm_g), lam_init)
        xs = _ffn_call(x1, *ffn2, merge=(g_out.reshape(Bs * T, GV), d_out.reshape(Bs * T, DV),
                                         wog, wod, row(mix_post_g)))
        outs[3].append(s_s.reshape(Bs, H_G, DK_G, DV_G))
        outs[4].append(dkf.reshape(Bs, T, H_D, 2 * DH_D))
        outs[5].append(dvf.reshape(Bs, T, H_D, DV_D))

    sg_p, k_p, v_p, sg_s, k_s, v_s = (jnp.stack(o) for o in outs)
    return (xp.reshape(B, S, D_MODEL), xs.reshape(Bs, T, D_MODEL), sg_p, k_p, v_p, sg_s, k_s, v_s)
```

```python
import functools
import math

import jax
import jax.numpy as jnp
from jax import lax
from jax.experimental import pallas as pl
from jax.experimental.pallas import tpu as pltpu

F32 = jnp.float32
BF16 = jnp.bfloat16

D_MODEL = 1024
D_FF = 2816
CHUNK = 64
H_G, DK_G, DV_G = 4, 64, 128
GATE_RANK = 16
GATE_TAU = 16.0
H_D, DH_D, DV_D = 4, 64, 128
EPS = 1e-6

GQ = H_G * DK_G
GV = H_G * DV_G
DQ = H_D * 2 * DH_D
DV = H_D * DV_D

LANES = 128
MXU_N = 256
TOKEN_TILE = 512
FFN_TILE = 1024
FFN_SUB = 512
FF_TILE = MXU_N
ATT_TILE = 512
ATT_HEADS = 2
VT_ROWS = DV_D + 16
GLA_ROWS = 256
GLA_PROMPT_BATCH = 8
GLA_DECODE_BATCH = 8
GLA_EXP_RANGE = 80.0
VMEM_LIMIT = 56 * 1024 * 1024


def _dot(a, b):
    return jnp.dot(a, b, preferred_element_type=F32)


def _dot_nt(a, b):
    return lax.dot_general(a, b, (((1,), (1,)), ((), ())), preferred_element_type=F32)


def _rms(x, g):
    r = lax.rsqrt(jnp.mean(x * x, axis=-1, keepdims=True) + EPS)
    return x * r * g


def _silu(x):
    return x * jax.nn.sigmoid(x)


def _const_spec(shape):
    nd = len(shape)
    return pl.BlockSpec(shape, lambda *_: (0,) * nd, pipeline_mode=pl.Buffered(1))


def _swiglu_half(x, pre_ref, post_ref, wg_ref, wu_ref, wd_ref, h_ref, act_ref):
    h_ref[...] = _rms(x, pre_ref[...]).astype(BF16)
    for j in range(D_FF // FF_TILE):
        cols = slice(j * FF_TILE, (j + 1) * FF_TILE)
        h = h_ref[...]
        gate = _dot(h, wg_ref[:, cols])
        up = _dot(h, wu_ref[:, cols])
        act_ref[:, cols] = (_silu(gate) * up).astype(BF16)
    f = _dot(act_ref[...], wd_ref[...])
    return x + 0.5 * _rms(f, post_ref[...])


def _sub_tiles(ref):
    return [slice(s * FFN_SUB, (s + 1) * FFN_SUB) for s in range(ref.shape[0] // FFN_SUB)]


def _ffn_kernel(x_ref, pre_ref, post_ref, wg_ref, wu_ref, wd_ref, o_ref, h_ref, act_ref):
    for s, rows in enumerate(_sub_tiles(x_ref)):
        o_ref[rows, :] = _swiglu_half(x_ref[rows, :], pre_ref, post_ref, wg_ref, wu_ref, wd_ref,
                                      h_ref.at[s], act_ref.at[s])


def _ffn_decode_attn_kernel(x_ref, pre_ref, post_ref, wg_ref, wu_ref, wd_ref, *rest, lam_init):
    att_refs, (o_ref, att_o_ref, h_ref, act_ref) = rest[:-4], rest[-4:]
    _ffn_kernel(x_ref, pre_ref, post_ref, wg_ref, wu_ref, wd_ref, o_ref, h_ref, act_ref)
    _decode_attention(*att_refs, att_o_ref, lam_init=lam_init)


def _merge_ffn_kernel(x_ref, g_ref, d_ref, wog_ref, wod_ref, mixg_ref,
                      pre_ref, post_ref, wg_ref, wu_ref, wd_ref, o_ref, h_ref, act_ref, x2_ref):
    for s, rows in enumerate(_sub_tiles(x_ref)):
        y = _dot(g_ref[rows, :], wog_ref[...]) + _dot(d_ref[rows, :], wod_ref[...])
        x2_ref[s] = x_ref[rows, :] + _rms(y, mixg_ref[...])
    for s, rows in enumerate(_sub_tiles(x_ref)):
        o_ref[rows, :] = _swiglu_half(x2_ref[s], pre_ref, post_ref, wg_ref, wu_ref, wd_ref,
                                      h_ref.at[s], act_ref.at[s])


def _ffn_call(x, pre_g, post_g, wg, wu, wd, merge=None, decode_attn=None):
    m = x.shape[0]
    tm = min(FFN_TILE, m)
    nsub = tm // FFN_SUB
    row = lambda w: pl.BlockSpec((tm, w), lambda i: (i, 0))
    ffn_specs = [_const_spec((1, D_MODEL)), _const_spec((1, D_MODEL)),
                 _const_spec((D_MODEL, D_FF)), _const_spec((D_MODEL, D_FF)), _const_spec((D_FF, D_MODEL))]
    scratch = [pltpu.VMEM((nsub, FFN_SUB, D_MODEL), BF16), pltpu.VMEM((nsub, FFN_SUB, D_FF), BF16)]
    out_specs, out_shape = row(D_MODEL), jax.ShapeDtypeStruct((m, D_MODEL), F32)
    if decode_attn is not None:
        dq, kc, vc, kn, vn, cache_base, lams, g_row, lam_init = decode_attn
        nb, T, _ = dq.shape
        assert merge is None and nb == m // tm, "one decode batch row per FFN grid step"
        new_q = pl.BlockSpec((1, T, DQ), lambda b: (b, 0, 0))
        new_kv = pl.BlockSpec((1, T * H_D, LANES), lambda b: (b, 0, 0))
        cache = pl.BlockSpec((1, kc.shape[1], LANES), lambda b: (cache_base + b, 0, 0))
        vec = pl.BlockSpec((1, DH_D), lambda b: (0, 0))
        kern = functools.partial(_ffn_decode_attn_kernel, lam_init=lam_init)
        ins = (x, pre_g, post_g, wg, wu, wd, dq, kc, vc, kn, vn, *lams, g_row)
        in_specs = ([row(D_MODEL)] + ffn_specs + [new_q, cache, cache, new_kv, new_kv, vec, vec, vec, vec,
                                                 pl.BlockSpec((1, DV_D), lambda b: (0, 0))])
        out_specs = [out_specs, new_q]
        out_shape = [out_shape, jax.ShapeDtypeStruct((nb, T, DV), BF16)]
    elif merge is None:
        kern, ins = _ffn_kernel, (x, pre_g, post_g, wg, wu, wd)
        in_specs = [row(D_MODEL)] + ffn_specs
    else:
        g_out, d_out, wog, wod, mix_g = merge
        kern, ins = _merge_ffn_kernel, (x, g_out, d_out, wog, wod, mix_g, pre_g, post_g, wg, wu, wd)
        in_specs = ([row(D_MODEL), row(GV), row(DV), _const_spec((GV, D_MODEL)),
                     _const_spec((DV, D_MODEL)), _const_spec((1, D_MODEL))] + ffn_specs)
        scratch = scratch + [pltpu.VMEM((nsub, FFN_SUB, D_MODEL), F32)]
    return pl.pallas_call(
        kern,
        grid=(m // tm,),
        in_specs=in_specs,
        out_specs=out_specs,
        out_shape=out_shape,
        scratch_shapes=scratch,
        compiler_params=pltpu.CompilerParams(dimension_semantics=("arbitrary",),
                                             vmem_limit_bytes=VMEM_LIMIT),
        name="merge_ffn" if merge is not None else ("ffn_decode_attn" if decode_attn is not None else "ffn"),
    )(*ins)


def _store_heads(ref, x, row0):
    rows = x.shape[0]
    for h in range(H_D):
        ref[pl.ds(row0 * H_D + h, rows, stride=H_D), :] = x[:, h * LANES:(h + 1) * LANES]


def _inproj_kernel(x_ref, mixpre_ref, wgla_ref, wgr_ref, wa2_ref, ba_ref, wdq_ref, wdk_ref, wdv_ref,
                   gq_ref, gk_ref, gv_ref, gg_ref, la_ref, dq_ref, dkf_ref, dkb_ref, dvf_ref,
                   *maybe_dvt_ref, transposed):
    t = ATT_TILE
    for s in range(x_ref.shape[0] // t):
        rows = slice(s * t, (s + 1) * t)
        h = _rms(x_ref[rows, :], mixpre_ref[...]).astype(BF16)
        dv = _dot(h, wdv_ref[...])
        _store_heads(dvf_ref, dv, s * t)
        dq = _dot(h, wdq_ref[...])
        if transposed:
            (dvt_ref,) = maybe_dvt_ref
            dvt = dv.T.astype(BF16)
            dqt = (dq * (DH_D ** -0.5 * math.log2(math.e))).T.astype(BF16)
            for hd in range(H_D):
                dq_ref[0, s, hd] = dqt[hd * 2 * DH_D:(hd + 1) * 2 * DH_D]
                dvt_ref[0, s, hd, :DV_D, :] = dvt[hd * DV_D:(hd + 1) * DV_D]
                dvt_ref[0, s, hd, DV_D:, :] = jnp.ones((VT_ROWS - DV_D, t), BF16)
        else:
            dq_ref[rows, :] = (dq * (DH_D ** -0.5)).astype(BF16)
        dk = _dot(h, wdk_ref[...])
        _store_heads(dkf_ref, dk, s * t)
        dkb_ref[rows, :] = dk.astype(BF16)
        gr = _dot(h, wgr_ref[...]).astype(BF16)
        gq_ref[rows, :] = (_dot(h, wgla_ref[:, 0:GQ]) * (DK_G ** -0.5)).astype(BF16)
        a = _dot(gr, wa2_ref[...]) + ba_ref[...]
        gk_ref[rows, :] = _dot(h, wgla_ref[:, GQ:2 * GQ]).astype(BF16)
        la_ref[rows, :] = (jnp.minimum(a, 0.0) - jnp.log1p(jnp.exp(-jnp.abs(a)))) / GATE_TAU
        gv_ref[rows, :] = _dot(h, wgla_ref[:, 2 * GQ:2 * GQ + GV]).astype(BF16)
        gg_ref[rows, :] = _dot(h, wgla_ref[:, 2 * GQ + GV:2 * GQ + 2 * GV]).astype(BF16)


def _inproj_call(x, mixpre_g, w, seq_len, transposed):
    m = x.shape[0]
    tm = min(TOKEN_TILE, m)
    nb = m // seq_len
    row = lambda wd: pl.BlockSpec((tm, wd), lambda i: (i, 0))
    heads = pl.BlockSpec((tm * H_D, LANES), lambda i: (i, 0))
    out_shape = [jax.ShapeDtypeStruct((m, GQ), BF16), jax.ShapeDtypeStruct((m, GQ), BF16),
                 jax.ShapeDtypeStruct((m, GV), BF16), jax.ShapeDtypeStruct((m, GV), BF16),
                 jax.ShapeDtypeStruct((m, GQ), F32)]
    out_specs = [row(GQ), row(GQ), row(GV), row(GV), row(GQ)]
    if transposed:
        t = ATT_TILE
        nq, per_step = seq_len // t, tm // t
        spb = nq // per_step
        out_shape.append(jax.ShapeDtypeStruct((nb, nq, H_D, 2 * DH_D, t), BF16))
        out_specs.append(pl.BlockSpec((1, per_step, H_D, 2 * DH_D, t), lambda i: (i // spb, i % spb, 0, 0, 0)))
    else:
        out_shape.append(jax.ShapeDtypeStruct((m, DQ), BF16))
        out_specs.append(row(DQ))
    out_shape += [jax.ShapeDtypeStruct((m * H_D, LANES), F32), jax.ShapeDtypeStruct((m, DQ), BF16),
                  jax.ShapeDtypeStruct((m * H_D, LANES), F32)]
    out_specs += [heads, row(DQ), heads]
    if transposed:
        out_shape.append(jax.ShapeDtypeStruct((nb, nq, H_D, VT_ROWS, t), BF16))
        out_specs.append(pl.BlockSpec((1, per_step, H_D, VT_ROWS, t), lambda i: (i // spb, i % spb, 0, 0, 0)))
    in_specs = [row(D_MODEL), _const_spec((1, D_MODEL)), _const_spec((D_MODEL, 2 * GQ + 2 * GV)),
                _const_spec((D_MODEL, LANES)), _const_spec((LANES, GQ)), _const_spec((1, GQ)),
                _const_spec((D_MODEL, DQ)), _const_spec((D_MODEL, DQ)), _const_spec((D_MODEL, DV))]
    return pl.pallas_call(
        functools.partial(_inproj_kernel, transposed=transposed),
        grid=(m // tm,),
        in_specs=in_specs,
        out_specs=out_specs,
        out_shape=out_shape,
        compiler_params=pltpu.CompilerParams(dimension_semantics=("arbitrary",),
                                             vmem_limit_bytes=VMEM_LIMIT),
        name="inproj_t" if transposed else "inproj",
    )(x, mixpre_g, w["gla"], w["gr"], w["a2"], w["ba"], w["dq"], w["dk"], w["dv"])


def _cumsum_rows(tril, x):
    hi = x.astype(BF16)
    r = x - hi.astype(F32)
    mid = r.astype(BF16)
    lo = (r - mid.astype(F32)).astype(BF16)
    return _dot(tril, hi) + _dot(tril, mid) + _dot(tril, lo)


def _gla_exact_scores(q_ref, k_ref, b_ref, a_ref, bi, c, slot, L):
    rows = slice(c * L, (c + 1) * L)
    q = q_ref[bi, rows, :].astype(F32)
    k = k_ref[bi, rows, :].astype(F32)
    b = b_ref[slot]
    row_id = lax.broadcasted_iota(jnp.int32, (L, GQ), 0)
    col_id = lax.broadcasted_iota(jnp.int32, (16, L), 1)
    head_of_lane = lax.broadcasted_iota(jnp.int32, (16, GQ), 1) >> int(math.log2(DK_G))
    head_ind = (head_of_lane == lax.broadcasted_iota(jnp.int32, (16, GQ), 0)).astype(BF16)

    def one_row(t, carry):
        sel = row_id == t
        qt = jnp.sum(jnp.where(sel, q, 0.0), axis=0, keepdims=True)
        bt = jnp.sum(jnp.where(sel, b, 0.0), axis=0, keepdims=True)
        prod = qt * k * jnp.exp(jnp.minimum(bt - b, 0.0))
        hi = prod.astype(BF16)
        lo = (prod - hi.astype(F32)).astype(BF16)
        per_head = _dot_nt(head_ind, hi) + _dot_nt(head_ind, lo)
        per_head = jnp.where(col_id <= t, per_head, 0.0)
        for h in range(H_G):
            a_ref[slot, pl.ds(h * L + t, 1), :] = per_head[h:h + 1, :]
        return carry

    lax.fori_loop(0, L, one_row, 0)


def _gla_kernel(q_ref, k_ref, v_ref, la_ref, gg_ref, g_ref, s0_ref, out_ref, sfin_ref,
                s_ref, a_ref, b_ref, *, L, nchunks, nbatch):
    i = pl.program_id(1)

    @pl.when(i == 0)
    def _():
        s_ref[...] = s0_ref[...]

    causal = (lax.broadcasted_iota(jnp.int32, (H_G * L, L), 1)
              <= (lax.broadcasted_iota(jnp.int32, (H_G * L, L), 0) & (L - 1)))
    lane_head = lax.broadcasted_iota(jnp.int32, (L, GQ), 1) >> int(math.log2(DK_G))
    tr = lax.broadcasted_iota(jnp.int32, (L, L), 0)
    tc = lax.broadcasted_iota(jnp.int32, (L, L), 1)
    tril = (tc <= tr).astype(BF16)

    def stack_heads(x):
        return jnp.concatenate([jnp.where(lane_head == h, x, 0.0) for h in range(H_G)], axis=0).astype(BF16)

    work = [(c, bi, bi * nchunks + c) for c in range(nchunks) for bi in range(nbatch)]
    for c, bi, slot in work:
        rows = slice(c * L, (c + 1) * L)
        b = _cumsum_rows(tril, la_ref[bi, rows, :])
        b_ref[slot] = b
        b_mid = b[L // 2 - 1:L // 2, :]
        q_mid = stack_heads(q_ref[bi, rows, :].astype(F32) * jnp.exp(b - b_mid))
        k_mid = (k_ref[bi, rows, :].astype(F32) * jnp.exp(b_mid - b)).astype(BF16)
        a_ref[slot] = jnp.where(causal, _dot_nt(q_mid, k_mid), 0.0)

    @pl.when(jnp.min(la_ref[...]) < -(2.0 * GLA_EXP_RANGE / L))
    def _():
        for c, bi, slot in work:
            _gla_exact_scores(q_ref, k_ref, b_ref, a_ref, bi, c, slot, L)

    states = [s_ref[bi] for bi in range(nbatch)]
    for c, bi, slot in work:
        rows = slice(c * L, (c + 1) * L)
        s = states[bi]
        b = b_ref[slot]
        b_end = b[L - 1:L, :]
        q = q_ref[bi, rows, :].astype(F32)
        k = k_ref[bi, rows, :].astype(F32)
        v = v_ref[bi, rows, :]
        q_abs = stack_heads(q * jnp.exp(b))
        k_end = k * jnp.exp(b_end - b)
        decay = jnp.exp(b_end)
        a = a_ref[slot].astype(BF16)
        o_state = _dot(q_abs, s.astype(BF16))

        kt = jnp.concatenate([k_end, jnp.broadcast_to(decay, (8, GQ)),
                              jnp.zeros((LANES - L - 8, GQ), F32)], axis=0).T
        kt_b = kt.astype(BF16)
        v_pad = jnp.concatenate([v, jnp.zeros((LANES - L, GV), BF16)], axis=0)
        decay_col = kt[:, L:L + 1]
        states[bi] = jnp.concatenate(
            [decay_col[h * DK_G:(h + 1) * DK_G] * s[h * DK_G:(h + 1) * DK_G]
             + _dot(kt_b[h * DK_G:(h + 1) * DK_G], v_pad[:, h * DV_G:(h + 1) * DV_G])
             for h in range(H_G)], axis=0)

        for h in range(H_G):
            cols = slice(h * DV_G, (h + 1) * DV_G)
            o = _dot(a[h * L:(h + 1) * L], v[:, cols]) + o_state[h * L:(h + 1) * L]
            gate = gg_ref[bi, rows, cols].astype(F32)
            out_ref[bi, rows, cols] = (_rms(o, g_ref[...]) * _silu(gate)).astype(BF16)

    for bi in range(nbatch):
        s_ref[bi] = states[bi]

    @pl.when(i == pl.num_programs(1) - 1)
    def _():
        for bi in range(nbatch):
            sfin_ref[bi] = states[bi]


def _gla_call(gq, gk, gv, la, gg, gla_g, s0, L, rows_per_step, nbatch):
    nb, seq_len = gq.shape[0], gq.shape[1]
    nchunks = rows_per_step // L
    blk = lambda w: pl.BlockSpec((nbatch, rows_per_step, w), lambda b, i: (b, i, 0))
    state = pl.BlockSpec((nbatch, GQ, DV_G), lambda b, i: (b, 0, 0))
    return pl.pallas_call(
        functools.partial(_gla_kernel, L=L, nchunks=nchunks, nbatch=nbatch),
        grid=(nb // nbatch, seq_len // rows_per_step),
        in_specs=[blk(GQ), blk(GQ), blk(GV), blk(GQ), blk(GV),
                  pl.BlockSpec((1, DV_G), lambda b, i: (0, 0)), state],
        out_specs=[blk(GV), state],
        out_shape=[jax.ShapeDtypeStruct((nb, seq_len, GV), BF16),
                   jax.ShapeDtypeStruct((nb, GQ, DV_G), F32)],
        scratch_shapes=[pltpu.VMEM((nbatch, GQ, DV_G), F32), pltpu.VMEM((nbatch * nchunks, H_G * L, L), F32),
                        pltpu.VMEM((nbatch * nchunks, L, GQ), F32)],
        compiler_params=pltpu.CompilerParams(dimension_semantics=("arbitrary", "arbitrary"),
                                             vmem_limit_bytes=VMEM_LIMIT),
        name="gla",
    )(gq, gk, gv, la, gg, gla_g, s0)


def _lambda(lq1_ref, lk1_ref, lq2_ref, lk2_ref, lam_init):
    s1 = jnp.sum(lq1_ref[...] * lk1_ref[...], axis=-1, keepdims=True)
    s2 = jnp.sum(lq2_ref[...] * lk2_ref[...], axis=-1, keepdims=True)
    return jnp.exp(s1) - jnp.exp(s2) + lam_init


def _diff_prompt_kernel(qt_ref, k_ref, vt_ref, bias_ref, lq1_ref, lk1_ref, lq2_ref, lk2_ref, g_ref, o_ref,
                        rhs_ref, m_ref, acc_ref, s_ref, *, lam_init):
    t = ATT_TILE
    i = pl.program_id(2)
    nq = pl.num_programs(2)

    chains = range(2 * ATT_HEADS)

    def half_queries(iq):
        qrow = lax.broadcasted_iota(jnp.int32, (2 * DH_D, t), 0)
        out = []
        for hh in range(ATT_HEADS):
            qt = qt_ref[0, iq, hh].astype(F32)
            out += [jnp.where((qrow >= c * DH_D) & (qrow < (c + 1) * DH_D), qt, 0.0).astype(BF16) for c in range(2)]
        return out

    def keys(j, hh):
        return k_ref[0, pl.ds(pl.multiple_of(j * t, t), t), hh * 2 * DH_D:(hh + 1) * 2 * DH_D]

    def scores(j, ch):
        return _dot(keys(j, ch // 2), rhs_ref[ch])

    def step(j, parity, diagonal):
        for ch in chains:
            if not diagonal:
                s_ref[1 - parity, ch] = scores(j + 1, ch)
            st = s_ref[parity, ch]
            if diagonal:
                st = st + bias_ref[...]
            m_old = m_ref[ch]
            m_new = jnp.maximum(m_old, jnp.max(st, axis=0, keepdims=True))
            alpha = jnp.exp2(m_old - m_new)
            p = jnp.exp2(st - m_new)
            acc_ref[ch] = alpha * acc_ref[ch] + _dot(vt_ref[0, j, ch // 2], p.astype(BF16))
            m_ref[ch] = m_new

    def finish():
        q_next = half_queries(jnp.minimum(i + 1, nq - 1))
        for ch in chains:
            s_ref[0, ch] = _dot(keys(0, ch // 2), q_next[ch])
        lam = _lambda(lq1_ref, lk1_ref, lq2_ref, lk2_ref, lam_init)
        for hh in range(ATT_HEADS):
            a1, a2 = acc_ref[2 * hh], acc_ref[2 * hh + 1]
            ot = a1[:DV_D] * (1.0 / a1[DV_D:DV_D + 1]) - lam * (a2[:DV_D] * (1.0 / a2[DV_D:DV_D + 1]))
            r = lax.rsqrt(jnp.mean(ot * ot, axis=0, keepdims=True) + EPS)
            ot = ot * r * g_ref[...] * (1.0 - lam_init)
            o_ref[0, :, hh * DV_D:(hh + 1) * DV_D] = ot.T.astype(BF16)

    def pair(n, carry):
        step(2 * n, 0, False)
        step(2 * n + 1, 1, False)
        return carry

    q_now = half_queries(i)
    for ch in chains:
        rhs_ref[ch] = q_now[ch]

    @pl.when(i == 0)
    def _():
        for ch in chains:
            s_ref[0, ch] = scores(0, ch)

    m_ref[...] = jnp.full(m_ref.shape, -jnp.inf, F32)
    acc_ref[...] = jnp.zeros(acc_ref.shape, F32)
    lax.fori_loop(0, i // 2, pair, 0)

    @pl.when(i % 2 == 0)
    def _():
        step(i, 0, True)
        finish()

    @pl.when(i % 2 == 1)
    def _():
        step(i - 1, 0, False)
        step(i, 1, True)
        finish()


def _diff_prompt_call(dqt, dkb, dvt, lams, g_col, lam_init):
    nb, seq_len = dkb.shape[0], dkb.shape[1]
    t, nh = ATT_TILE, ATT_HEADS
    chunk_of = jnp.arange(t, dtype=jnp.int32) // CHUNK
    bias = jnp.where(chunk_of[:, None] <= chunk_of[None, :], 0.0, -jnp.inf).astype(F32)
    vec = pl.BlockSpec((1, DH_D), lambda b, h, i: (0, 0))
    return pl.pallas_call(
        functools.partial(_diff_prompt_kernel, lam_init=lam_init),
        grid=(nb, H_D // nh, seq_len // t),
        in_specs=[pl.BlockSpec((1, seq_len // t, nh, 2 * DH_D, t), lambda b, h, i: (b, 0, h, 0, 0)),
                  pl.BlockSpec((1, seq_len, nh * 2 * DH_D), lambda b, h, i: (b, 0, h)),
                  pl.BlockSpec((1, seq_len // t, nh, VT_ROWS, t), lambda b, h, i: (b, 0, h, 0, 0)),
                  pl.BlockSpec((t, t), lambda b, h, i: (0, 0), pipeline_mode=pl.Buffered(1)),
                  vec, vec, vec, vec,
                  pl.BlockSpec((DV_D, 1), lambda b, h, i: (0, 0))],
        out_specs=pl.BlockSpec((1, t, nh * DV_D), lambda b, h, i: (b, i, h)),
        out_shape=jax.ShapeDtypeStruct((nb, seq_len, DV), BF16),
        scratch_shapes=[pltpu.VMEM((2 * nh, 2 * DH_D, t), BF16), pltpu.VMEM((2 * nh, 1, t), F32),
                        pltpu.VMEM((2 * nh, VT_ROWS, t), F32), pltpu.VMEM((2, 2 * nh, t, t), F32)],
        compiler_params=pltpu.CompilerParams(dimension_semantics=("arbitrary", "arbitrary", "arbitrary"),
                                             vmem_limit_bytes=VMEM_LIMIT),
        name="diff_prompt",
    )(dqt, dkb, dvt, bias, *lams, g_col)


def _decode_attention(q_ref, kc_ref, vc_ref, kn_ref, vn_ref, lq1_ref, lk1_ref, lq2_ref, lk2_ref, g_ref,
                      o_ref, *, lam_init):
    T = q_ref.shape[1]
    P = kc_ref.shape[1] // H_D
    lam = _lambda(lq1_ref, lk1_ref, lq2_ref, lk2_ref, lam_init)
    lane = lax.broadcasted_iota(jnp.int32, (T, 2 * DH_D), 1)
    new_col = lax.broadcasted_iota(jnp.int32, (2 * T, LANES), 1)
    pad = jnp.zeros((LANES - T, 2 * DH_D), BF16)
    for h in range(H_D):
        cols = slice(h * 2 * DH_D, (h + 1) * 2 * DH_D)
        q = q_ref[0, :, cols].astype(F32)
        qs = jnp.concatenate([jnp.where(lane < DH_D, q, 0.0), jnp.where(lane >= DH_D, q, 0.0)],
                             axis=0).astype(BF16)
        kc = kc_ref[0, pl.ds(h, P, stride=H_D), :].astype(BF16)
        vc = vc_ref[0, pl.ds(h, P, stride=H_D), :].astype(BF16)
        kn = jnp.concatenate([kn_ref[0, pl.ds(h, T, stride=H_D), :].astype(BF16), pad], axis=0)
        vn = jnp.concatenate([vn_ref[0, pl.ds(h, T, stride=H_D), :].astype(BF16), pad], axis=0)
        sc = _dot_nt(qs, kc)
        sn = jnp.where(new_col < T, _dot_nt(qs, kn), -jnp.inf)
        m = jnp.maximum(jnp.max(sc, axis=-1, keepdims=True), jnp.max(sn, axis=-1, keepdims=True))
        pc = jnp.exp(sc - m)
        pn = jnp.exp(sn - m)
        l = jnp.sum(pc, axis=-1, keepdims=True) + jnp.sum(pn, axis=-1, keepdims=True)
        o2 = (_dot(pc.astype(BF16), vc) + _dot(pn.astype(BF16), vn)) / l
        o = o2[:T] - lam * o2[T:]
        o_ref[0, :, cols] = (_rms(o, g_ref[...]) * (1.0 - lam_init)).astype(BF16)


def kernel(x_prompt, x_sample, state_gla, cache_diff_k, cache_diff_v, w_in, w_gate_a2, b_gate_a, gla_norm_g, lambda_q1, lambda_k1, lambda_q2, lambda_k2, diff_norm_g, w_out, mix_pre_g, mix_post_g, ffn1_pre_g, ffn1_post_g, ffn1_w_gate, ffn1_w_up, ffn1_w_down, ffn2_pre_g, ffn2_post_g, ffn2_w_gate, ffn2_w_up, ffn2_w_down):
    B, S, _ = x_prompt.shape
    Bs, T, _ = x_sample.shape
    depth = w_in.shape[0]
    P = cache_diff_k.shape[2]
    xp = x_prompt.reshape(B * S, D_MODEL)
    xs = x_sample.reshape(Bs * T, D_MODEL)
    cache_k = cache_diff_k.reshape(depth * Bs, P * H_D, LANES)
    cache_v = cache_diff_v.reshape(depth * Bs, P * H_D, LANES)
    outs = [[] for _ in range(6)]
    for l in range(depth):
        lam_init = 0.8 - 0.6 * math.exp(-0.3 * l)
        row = lambda v: v[l].reshape(1, -1)
        o_gr = 2 * GQ + GV
        o_gg = o_gr + GATE_RANK
        o_dq = o_gg + GV
        wi = w_in[l]
        w = {
            "gla": jnp.concatenate([wi[:, :o_gr], wi[:, o_gg:o_dq]], axis=1).astype(BF16),
            "gr": jnp.pad(wi[:, o_gr:o_gg], ((0, 0), (0, LANES - GATE_RANK))).astype(BF16),
            "a2": jnp.pad(w_gate_a2[l], ((0, LANES - GATE_RANK), (0, 0))).astype(BF16),
            "ba": row(b_gate_a),
            "dq": wi[:, o_dq:o_dq + DQ].astype(BF16),
            "dk": wi[:, o_dq + DQ:o_dq + 2 * DQ].astype(BF16),
            "dv": wi[:, o_dq + 2 * DQ:].astype(BF16),
        }
        ffn1 = (row(ffn1_pre_g), row(ffn1_post_g), ffn1_w_gate[l].astype(BF16),
                ffn1_w_up[l].astype(BF16), ffn1_w_down[l].astype(BF16))
        ffn2 = (row(ffn2_pre_g), row(ffn2_post_g), ffn2_w_gate[l].astype(BF16),
                ffn2_w_up[l].astype(BF16), ffn2_w_down[l].astype(BF16))
        wo = w_out[l].astype(BF16)
        wog, wod = wo[:GV], wo[GV:]
        lams = (row(lambda_q1), row(lambda_k1), row(lambda_q2), row(lambda_k2))
        gla_g = row(gla_norm_g)

        x1s = _ffn_call(xs, *ffn1)
        gq_s, gk_s, gv_s, gg_s, la_s, dq_s, dkf_s, _, dvf_s = _inproj_call(x1s, row(mix_pre_g), w, T, False)
        s3 = lambda a: a.reshape(Bs, T, a.shape[-1])
        decode_attn = (s3(dq_s), cache_k, cache_v, dkf_s.reshape(Bs, T * H_D, LANES),
                       dvf_s.reshape(Bs, T * H_D, LANES), l * Bs, lams, row(diff_norm_g), lam_init)

        x1, d_out_s = _ffn_call(xp, *ffn1, decode_attn=decode_attn)
        gq, gk, gv, gg, la, dqt, dkf, dkb, dvf, dvt = _inproj_call(x1, row(mix_pre_g), w, S, True)
        b3 = lambda a: a.reshape(B, S, a.shape[-1])
        g_out, s_p = _gla_call(b3(gq), b3(gk), b3(gv), b3(la), b3(gg), gla_g,
                               jnp.zeros((B, GQ, DV_G), F32), CHUNK, GLA_ROWS, GLA_PROMPT_BATCH)
        d_out = _diff_prompt_call(dqt, b3(dkb), dvt, lams, diff_norm_g[l].reshape(DV_D, 1), lam_init)
        xp = _ffn_call(x1, *ffn2, merge=(g_out.reshape(B * S, GV), d_out.reshape(B * S, DV),
                                         wog, wod, row(mix_post_g)))
        outs[0].append(s_p.reshape(B, H_G, DK_G, DV_G))
        outs[1].append(dkf.reshape(B, S, H_D, 2 * DH_D))
        outs[2].append(dvf.reshape(B, S, H_D, DV_D))

        g_out_s, s_s = _gla_call(s3(gq_s), s3(gk_s), s3(gv_s), s3(la_s), s3(gg_s), gla_g,
                                 state_gla[l].reshape(Bs, GQ, DV_G), T, T, GLA_DECODE_BATCH)
        xs = _ffn_call(x1s, *ffn2, merge=(g_out_s.reshape(Bs * T, GV), d_out_s.reshape(Bs * T, DV),
                                          wog, wod, row(mix_post_g)))
        outs[3].append(s_s.reshape(Bs, H_G, DK_G, DV_G))
        outs[4].append(dkf_s.reshape(Bs, T, H_D, 2 * DH_D))
        outs[5].append(dvf_s.reshape(Bs, T, H_D, DV_D))

    sg_p, k_p, v_p, sg_s, k_s, v_s = (jnp.stack(o) for o in outs)
    return (xp.reshape(B, S, D_MODEL), xs.reshape(Bs, T, D_MODEL), sg_p, k_p, v_p, sg_s, k_s, v_s)
```

```python
import functools
import math

import jax
import jax.numpy as jnp
from jax import lax
from jax.experimental import pallas as pl
from jax.experimental.pallas import tpu as pltpu

F32 = jnp.float32
BF16 = jnp.bfloat16

D_MODEL = 1024
D_FF = 2816
CHUNK = 64
H_G, DK_G, DV_G = 4, 64, 128
GATE_RANK = 16
GATE_TAU = 16.0
H_D, DH_D, DV_D = 4, 64, 128
EPS = 1e-6

GQ = H_G * DK_G
GV = H_G * DV_G
DQ = H_D * 2 * DH_D
DV = H_D * DV_D

LANES = 128
MXU_N = 256
TOKEN_TILE = 512
FFN_TILE = 1024
FFN_SUB = 512
FF_TILE = MXU_N
ATT_TILE = 512
ATT_HEADS = 4
VT_ROWS = DV_D + 16
GLA_ROWS = 256
GLA_PROMPT_BATCH = 8
GLA_DECODE_BATCH = 8
GLA_EXP_RANGE = 80.0
VMEM_LIMIT = 56 * 1024 * 1024


def _dot(a, b):
    return jnp.dot(a, b, preferred_element_type=F32)


def _dot_nt(a, b):
    return lax.dot_general(a, b, (((1,), (1,)), ((), ())), preferred_element_type=F32)


def _rms(x, g):
    r = lax.rsqrt(jnp.mean(x * x, axis=-1, keepdims=True) + EPS)
    return x * r * g


def _silu(x):
    return x * jax.nn.sigmoid(x)


def _const_spec(shape):
    nd = len(shape)
    return pl.BlockSpec(shape, lambda *_: (0,) * nd, pipeline_mode=pl.Buffered(1))


def _swiglu_half(x, pre_ref, post_ref, wg_ref, wu_ref, wd_ref, h_ref, act_ref):
    h_ref[...] = _rms(x, pre_ref[...]).astype(BF16)
    for j in range(D_FF // FF_TILE):
        cols = slice(j * FF_TILE, (j + 1) * FF_TILE)
        h = h_ref[...]
        gate = _dot(h, wg_ref[:, cols])
        up = _dot(h, wu_ref[:, cols])
        act_ref[:, cols] = (_silu(gate) * up).astype(BF16)
    f = _dot(act_ref[...], wd_ref[...])
    return x + 0.5 * _rms(f, post_ref[...])


def _sub_tiles(ref):
    return [slice(s * FFN_SUB, (s + 1) * FFN_SUB) for s in range(ref.shape[0] // FFN_SUB)]


def _ffn_kernel(x_ref, pre_ref, post_ref, wg_ref, wu_ref, wd_ref, o_ref, h_ref, act_ref):
    for s, rows in enumerate(_sub_tiles(x_ref)):
        o_ref[rows, :] = _swiglu_half(x_ref[rows, :], pre_ref, post_ref, wg_ref, wu_ref, wd_ref,
                                      h_ref.at[s], act_ref.at[s])


def _ffn_decode_attn_kernel(x_ref, pre_ref, post_ref, wg_ref, wu_ref, wd_ref, *rest, lam_init):
    att_refs, (o_ref, att_o_ref, h_ref, act_ref) = rest[:-4], rest[-4:]
    _ffn_kernel(x_ref, pre_ref, post_ref, wg_ref, wu_ref, wd_ref, o_ref, h_ref, act_ref)
    _decode_attention(*att_refs, att_o_ref, lam_init=lam_init)


def _merge_ffn_kernel(x_ref, g_ref, d_ref, wog_ref, wod_ref, mixg_ref,
                      pre_ref, post_ref, wg_ref, wu_ref, wd_ref, o_ref, h_ref, act_ref, x2_ref):
    for s, rows in enumerate(_sub_tiles(x_ref)):
        y = _dot(g_ref[rows, :], wog_ref[...]) + _dot(d_ref[rows, :], wod_ref[...])
        x2_ref[s] = x_ref[rows, :] + _rms(y, mixg_ref[...])
    for s, rows in enumerate(_sub_tiles(x_ref)):
        o_ref[rows, :] = _swiglu_half(x2_ref[s], pre_ref, post_ref, wg_ref, wu_ref, wd_ref,
                                      h_ref.at[s], act_ref.at[s])


def _ffn_call(x, pre_g, post_g, wg, wu, wd, merge=None, decode_attn=None):
    m = x.shape[0]
    tm = min(FFN_TILE, m)
    nsub = tm // FFN_SUB
    row = lambda w: pl.BlockSpec((tm, w), lambda i: (i, 0))
    ffn_specs = [_const_spec((1, D_MODEL)), _const_spec((1, D_MODEL)),
                 _const_spec((D_MODEL, D_FF)), _const_spec((D_MODEL, D_FF)), _const_spec((D_FF, D_MODEL))]
    scratch = [pltpu.VMEM((nsub, FFN_SUB, D_MODEL), BF16), pltpu.VMEM((nsub, FFN_SUB, D_FF), BF16)]
    out_specs, out_shape = row(D_MODEL), jax.ShapeDtypeStruct((m, D_MODEL), F32)
    if decode_attn is not None:
        dq, kc, vc, kn, vn, cache_base, lams, g_row, lam_init = decode_attn
        nb, T, _ = dq.shape
        assert merge is None and nb == m // tm, "one decode batch row per FFN grid step"
        new_q = pl.BlockSpec((1, T, DQ), lambda b: (b, 0, 0))
        new_kv = pl.BlockSpec((1, T * H_D, LANES), lambda b: (b, 0, 0))
        cache = pl.BlockSpec((1, kc.shape[1], LANES), lambda b: (cache_base + b, 0, 0))
        vec = pl.BlockSpec((1, DH_D), lambda b: (0, 0))
        kern = functools.partial(_ffn_decode_attn_kernel, lam_init=lam_init)
        ins = (x, pre_g, post_g, wg, wu, wd, dq, kc, vc, kn, vn, *lams, g_row)
        in_specs = ([row(D_MODEL)] + ffn_specs + [new_q, cache, cache, new_kv, new_kv, vec, vec, vec, vec,
                                                 pl.BlockSpec((1, DV_D), lambda b: (0, 0))])
        out_specs = [out_specs, new_q]
        out_shape = [out_shape, jax.ShapeDtypeStruct((nb, T, DV), BF16)]
    elif merge is None:
        kern, ins = _ffn_kernel, (x, pre_g, post_g, wg, wu, wd)
        in_specs = [row(D_MODEL)] + ffn_specs
    else:
        g_out, d_out, wog, wod, mix_g = merge
        kern, ins = _merge_ffn_kernel, (x, g_out, d_out, wog, wod, mix_g, pre_g, post_g, wg, wu, wd)
        in_specs = ([row(D_MODEL), row(GV), row(DV), _const_spec((GV, D_MODEL)),
                     _const_spec((DV, D_MODEL)), _const_spec((1, D_MODEL))] + ffn_specs)
        scratch = scratch + [pltpu.VMEM((nsub, FFN_SUB, D_MODEL), F32)]
    return pl.pallas_call(
        kern,
        grid=(m // tm,),
        in_specs=in_specs,
        out_specs=out_specs,
        out_shape=out_shape,
        scratch_shapes=scratch,
        compiler_params=pltpu.CompilerParams(dimension_semantics=("arbitrary",),
                                             vmem_limit_bytes=VMEM_LIMIT),
        name="merge_ffn" if merge is not None else ("ffn_decode_attn" if decode_attn is not None else "ffn"),
    )(*ins)


def _store_heads(ref, x, row0):
    rows = x.shape[0]
    for h in range(H_D):
        ref[pl.ds(row0 * H_D + h, rows, stride=H_D), :] = x[:, h * LANES:(h + 1) * LANES]


def _inproj_kernel(x_ref, mixpre_ref, wgla_ref, wgr_ref, wa2_ref, ba_ref, wdq_ref, wdk_ref, wdv_ref,
                   gq_ref, gk_ref, gv_ref, gg_ref, la_ref, dq_ref, dkf_ref, dkb_ref, dvf_ref,
                   *maybe_dvt_ref, transposed):
    t = ATT_TILE
    for s in range(x_ref.shape[0] // t):
        rows = slice(s * t, (s + 1) * t)
        h = _rms(x_ref[rows, :], mixpre_ref[...]).astype(BF16)
        dv = _dot(h, wdv_ref[...])
        _store_heads(dvf_ref, dv, s * t)
        dq = _dot(h, wdq_ref[...])
        if transposed:
            (dvt_ref,) = maybe_dvt_ref
            dvt = dv.T.astype(BF16)
            dqt = (dq * (DH_D ** -0.5 * math.log2(math.e))).T.astype(BF16)
            for hd in range(H_D):
                dq_ref[0, s, hd] = dqt[hd * 2 * DH_D:(hd + 1) * 2 * DH_D]
                dvt_ref[0, s, hd, :DV_D, :] = dvt[hd * DV_D:(hd + 1) * DV_D]
                dvt_ref[0, s, hd, DV_D:, :] = jnp.ones((VT_ROWS - DV_D, t), BF16)
        else:
            dq_ref[rows, :] = (dq * (DH_D ** -0.5)).astype(BF16)
        dk = _dot(h, wdk_ref[...])
        _store_heads(dkf_ref, dk, s * t)
        dkb_ref[rows, :] = dk.astype(BF16)
        gr = _dot(h, wgr_ref[...]).astype(BF16)
        gq_ref[rows, :] = (_dot(h, wgla_ref[:, 0:GQ]) * (DK_G ** -0.5)).astype(BF16)
        a = _dot(gr, wa2_ref[...]) + ba_ref[...]
        gk_ref[rows, :] = _dot(h, wgla_ref[:, GQ:2 * GQ]).astype(BF16)
        la_ref[rows, :] = (jnp.minimum(a, 0.0) - jnp.log1p(jnp.exp(-jnp.abs(a)))) / GATE_TAU
        gv_ref[rows, :] = _dot(h, wgla_ref[:, 2 * GQ:2 * GQ + GV]).astype(BF16)
        gg_ref[rows, :] = _dot(h, wgla_ref[:, 2 * GQ + GV:2 * GQ + 2 * GV]).astype(BF16)


def _inproj_call(x, mixpre_g, w, seq_len, transposed):
    m = x.shape[0]
    tm = min(TOKEN_TILE, m)
    nb = m // seq_len
    row = lambda wd: pl.BlockSpec((tm, wd), lambda i: (i, 0))
    heads = pl.BlockSpec((tm * H_D, LANES), lambda i: (i, 0))
    out_shape = [jax.ShapeDtypeStruct((m, GQ), BF16), jax.ShapeDtypeStruct((m, GQ), BF16),
                 jax.ShapeDtypeStruct((m, GV), BF16), jax.ShapeDtypeStruct((m, GV), BF16),
                 jax.ShapeDtypeStruct((m, GQ), F32)]
    out_specs = [row(GQ), row(GQ), row(GV), row(GV), row(GQ)]
    if transposed:
        t = ATT_TILE
        nq, per_step = seq_len // t, tm // t
        spb = nq // per_step
        out_shape.append(jax.ShapeDtypeStruct((nb, nq, H_D, 2 * DH_D, t), BF16))
        out_specs.append(pl.BlockSpec((1, per_step, H_D, 2 * DH_D, t), lambda i: (i // spb, i % spb, 0, 0, 0)))
    else:
        out_shape.append(jax.ShapeDtypeStruct((m, DQ), BF16))
        out_specs.append(row(DQ))
    out_shape += [jax.ShapeDtypeStruct((m * H_D, LANES), F32), jax.ShapeDtypeStruct((m, DQ), BF16),
                  jax.ShapeDtypeStruct((m * H_D, LANES), F32)]
    out_specs += [heads, row(DQ), heads]
    if transposed:
        out_shape.append(jax.ShapeDtypeStruct((nb, nq, H_D, VT_ROWS, t), BF16))
        out_specs.append(pl.BlockSpec((1, per_step, H_D, VT_ROWS, t), lambda i: (i // spb, i % spb, 0, 0, 0)))
    in_specs = [row(D_MODEL), _const_spec((1, D_MODEL)), _const_spec((D_MODEL, 2 * GQ + 2 * GV)),
                _const_spec((D_MODEL, LANES)), _const_spec((LANES, GQ)), _const_spec((1, GQ)),
                _const_spec((D_MODEL, DQ)), _const_spec((D_MODEL, DQ)), _const_spec((D_MODEL, DV))]
    return pl.pallas_call(
        functools.partial(_inproj_kernel, transposed=transposed),
        grid=(m // tm,),
        in_specs=in_specs,
        out_specs=out_specs,
        out_shape=out_shape,
        compiler_params=pltpu.CompilerParams(dimension_semantics=("arbitrary",),
                                             vmem_limit_bytes=VMEM_LIMIT),
        name="inproj_t" if transposed else "inproj",
    )(x, mixpre_g, w["gla"], w["gr"], w["a2"], w["ba"], w["dq"], w["dk"], w["dv"])


def _cumsum_rows(tril, x):
    hi = x.astype(BF16)
    r = x - hi.astype(F32)
    mid = r.astype(BF16)
    lo = (r - mid.astype(F32)).astype(BF16)
    return _dot(tril, hi) + _dot(tril, mid) + _dot(tril, lo)


def _gla_exact_scores(q_ref, k_ref, b_ref, a_ref, bi, c, slot, L):
    rows = slice(c * L, (c + 1) * L)
    q = q_ref[bi, rows, :].astype(F32)
    k = k_ref[bi, rows, :].astype(F32)
    b = b_ref[slot]
    row_id = lax.broadcasted_iota(jnp.int32, (L, GQ), 0)
    col_id = lax.broadcasted_iota(jnp.int32, (16, L), 1)
    head_of_lane = lax.broadcasted_iota(jnp.int32, (16, GQ), 1) >> int(math.log2(DK_G))
    head_ind = (head_of_lane == lax.broadcasted_iota(jnp.int32, (16, GQ), 0)).astype(BF16)

    def one_row(t, carry):
        sel = row_id == t
        qt = jnp.sum(jnp.where(sel, q, 0.0), axis=0, keepdims=True)
        bt = jnp.sum(jnp.where(sel, b, 0.0), axis=0, keepdims=True)
        prod = qt * k * jnp.exp(jnp.minimum(bt - b, 0.0))
        hi = prod.astype(BF16)
        lo = (prod - hi.astype(F32)).astype(BF16)
        per_head = _dot_nt(head_ind, hi) + _dot_nt(head_ind, lo)
        per_head = jnp.where(col_id <= t, per_head, 0.0)
        for h in range(H_G):
            a_ref[slot, pl.ds(h * L + t, 1), :] = per_head[h:h + 1, :]
        return carry

    lax.fori_loop(0, L, one_row, 0)


def _gla_kernel(q_ref, k_ref, v_ref, la_ref, gg_ref, g_ref, s0_ref, out_ref, sfin_ref,
                s_ref, a_ref, b_ref, *, L, nchunks, nbatch):
    i = pl.program_id(1)

    @pl.when(i == 0)
    def _():
        s_ref[...] = s0_ref[...]

    causal = (lax.broadcasted_iota(jnp.int32, (H_G * L, L), 1)
              <= (lax.broadcasted_iota(jnp.int32, (H_G * L, L), 0) & (L - 1)))
    lane_head = lax.broadcasted_iota(jnp.int32, (L, GQ), 1) >> int(math.log2(DK_G))
    tr = lax.broadcasted_iota(jnp.int32, (L, L), 0)
    tc = lax.broadcasted_iota(jnp.int32, (L, L), 1)
    tril = (tc <= tr).astype(BF16)

    def stack_heads(x):
        return jnp.concatenate([jnp.where(lane_head == h, x, 0.0) for h in range(H_G)], axis=0).astype(BF16)

    work = [(c, bi, bi * nchunks + c) for c in range(nchunks) for bi in range(nbatch)]
    for c, bi, slot in work:
        rows = slice(c * L, (c + 1) * L)
        b = _cumsum_rows(tril, la_ref[bi, rows, :])
        b_ref[slot] = b
        b_mid = b[L // 2 - 1:L // 2, :]
        q_mid = stack_heads(q_ref[bi, rows, :].astype(F32) * jnp.exp(b - b_mid))
        k_mid = (k_ref[bi, rows, :].astype(F32) * jnp.exp(b_mid - b)).astype(BF16)
        a_ref[slot] = jnp.where(causal, _dot_nt(q_mid, k_mid), 0.0)

    @pl.when(jnp.min(la_ref[...]) < -(2.0 * GLA_EXP_RANGE / L))
    def _():
        for c, bi, slot in work:
            _gla_exact_scores(q_ref, k_ref, b_ref, a_ref, bi, c, slot, L)

    states = [s_ref[bi] for bi in range(nbatch)]
    for c, bi, slot in work:
        rows = slice(c * L, (c + 1) * L)
        s = states[bi]
        b = b_ref[slot]
        b_end = b[L - 1:L, :]
        q = q_ref[bi, rows, :].astype(F32)
        k = k_ref[bi, rows, :].astype(F32)
        v = v_ref[bi, rows, :]
        q_abs = stack_heads(q * jnp.exp(b))
        k_end = k * jnp.exp(b_end - b)
        decay = jnp.exp(b_end)
        a = a_ref[slot].astype(BF16)
        o_state = _dot(q_abs, s.astype(BF16))

        kt = jnp.concatenate([k_end, jnp.broadcast_to(decay, (8, GQ)),
                              jnp.zeros((LANES - L - 8, GQ), F32)], axis=0).T
        kt_b = kt.astype(BF16)
        v_pad = jnp.concatenate([v, jnp.zeros((LANES - L, GV), BF16)], axis=0)
        decay_col = kt[:, L:L + 1]
        states[bi] = jnp.concatenate(
            [decay_col[h * DK_G:(h + 1) * DK_G] * s[h * DK_G:(h + 1) * DK_G]
             + _dot(kt_b[h * DK_G:(h + 1) * DK_G], v_pad[:, h * DV_G:(h + 1) * DV_G])
             for h in range(H_G)], axis=0)

        for h in range(H_G):
            cols = slice(h * DV_G, (h + 1) * DV_G)
            o = _dot(a[h * L:(h + 1) * L], v[:, cols]) + o_state[h * L:(h + 1) * L]
            gate = gg_ref[bi, rows, cols].astype(F32)
            out_ref[bi, rows, cols] = (_rms(o, g_ref[...]) * _silu(gate)).astype(BF16)

    for bi in range(nbatch):
        s_ref[bi] = states[bi]

    @pl.when(i == pl.num_programs(1) - 1)
    def _():
        for bi in range(nbatch):
            sfin_ref[bi] = states[bi]


def _gla_call(gq, gk, gv, la, gg, gla_g, s0, L, rows_per_step, nbatch):
    nb, seq_len = gq.shape[0], gq.shape[1]
    nchunks = rows_per_step // L
    blk = lambda w: pl.BlockSpec((nbatch, rows_per_step, w), lambda b, i: (b, i, 0))
    state = pl.BlockSpec((nbatch, GQ, DV_G), lambda b, i: (b, 0, 0))
    return pl.pallas_call(
        functools.partial(_gla_kernel, L=L, nchunks=nchunks, nbatch=nbatch),
        grid=(nb // nbatch, seq_len // rows_per_step),
        in_specs=[blk(GQ), blk(GQ), blk(GV), blk(GQ), blk(GV),
                  pl.BlockSpec((1, DV_G), lambda b, i: (0, 0)), state],
        out_specs=[blk(GV), state],
        out_shape=[jax.ShapeDtypeStruct((nb, seq_len, GV), BF16),
                   jax.ShapeDtypeStruct((nb, GQ, DV_G), F32)],
        scratch_shapes=[pltpu.VMEM((nbatch, GQ, DV_G), F32), pltpu.VMEM((nbatch * nchunks, H_G * L, L), F32),
                        pltpu.VMEM((nbatch * nchunks, L, GQ), F32)],
        compiler_params=pltpu.CompilerParams(dimension_semantics=("arbitrary", "arbitrary"),
                                             vmem_limit_bytes=VMEM_LIMIT),
        name="gla",
    )(gq, gk, gv, la, gg, gla_g, s0)


def _lambda(lq1_ref, lk1_ref, lq2_ref, lk2_ref, lam_init):
    s1 = jnp.sum(lq1_ref[...] * lk1_ref[...], axis=-1, keepdims=True)
    s2 = jnp.sum(lq2_ref[...] * lk2_ref[...], axis=-1, keepdims=True)
    return jnp.exp(s1) - jnp.exp(s2) + lam_init


def _diff_prompt_kernel(qt_ref, k_ref, vt_ref, bias_ref, lq1_ref, lk1_ref, lq2_ref, lk2_ref, g_ref, o_ref,
                        rhs_ref, m_ref, acc_ref, s_ref, *, lam_init):
    t = ATT_TILE
    i = pl.program_id(2)
    nq = pl.num_programs(2)

    chains = range(2 * ATT_HEADS)

    def half_queries(iq):
        qrow = lax.broadcasted_iota(jnp.int32, (2 * DH_D, t), 0)
        out = []
        for hh in range(ATT_HEADS):
            qt = qt_ref[0, iq, hh].astype(F32)
            out += [jnp.where((qrow >= c * DH_D) & (qrow < (c + 1) * DH_D), qt, 0.0).astype(BF16) for c in range(2)]
        return out

    def keys(j, hh):
        return k_ref[0, pl.ds(pl.multiple_of(j * t, t), t), hh * 2 * DH_D:(hh + 1) * 2 * DH_D]

    def scores(j, ch):
        return _dot(keys(j, ch // 2), rhs_ref[ch])

    def step(j, parity, diagonal):
        for ch in chains:
            if not diagonal:
                s_ref[1 - parity, ch] = scores(j + 1, ch)
            st = s_ref[parity, ch]
            if diagonal:
                st = st + bias_ref[...]
            m_old = m_ref[ch]
            m_new = jnp.maximum(m_old, jnp.max(st, axis=0, keepdims=True))
            alpha = jnp.exp2(m_old - m_new)
            p = jnp.exp2(st - m_new)
            acc_ref[ch] = alpha * acc_ref[ch] + _dot(vt_ref[0, j, ch // 2], p.astype(BF16))
            m_ref[ch] = m_new

    def finish():
        q_next = half_queries(jnp.minimum(i + 1, nq - 1))
        for ch in chains:
            s_ref[0, ch] = _dot(keys(0, ch // 2), q_next[ch])
        lam = _lambda(lq1_ref, lk1_ref, lq2_ref, lk2_ref, lam_init)
        for hh in range(ATT_HEADS):
            a1, a2 = acc_ref[2 * hh], acc_ref[2 * hh + 1]
            ot = a1[:DV_D] * (1.0 / a1[DV_D:DV_D + 1]) - lam * (a2[:DV_D] * (1.0 / a2[DV_D:DV_D + 1]))
            r = lax.rsqrt(jnp.mean(ot * ot, axis=0, keepdims=True) + EPS)
            ot = ot * r * g_ref[...] * (1.0 - lam_init)
            o_ref[0, :, hh * DV_D:(hh + 1) * DV_D] = ot.T.astype(BF16)

    def pair(n, carry):
        step(2 * n, 0, False)
        step(2 * n + 1, 1, False)
        return carry

    q_now = half_queries(i)
    for ch in chains:
        rhs_ref[ch] = q_now[ch]

    @pl.when(i == 0)
    def _():
        for ch in chains:
            s_ref[0, ch] = scores(0, ch)

    m_ref[...] = jnp.full(m_ref.shape, -jnp.inf, F32)
    acc_ref[...] = jnp.zeros(acc_ref.shape, F32)
    lax.fori_loop(0, i // 2, pair, 0)

    @pl.when(i % 2 == 0)
    def _():
        step(i, 0, True)
        finish()

    @pl.when(i % 2 == 1)
    def _():
        step(i - 1, 0, False)
        step(i, 1, True)
        finish()


def _diff_prompt_call(dqt, dkb, dvt, lams, g_col, lam_init):
    nb, seq_len = dkb.shape[0], dkb.shape[1]
    t, nh = ATT_TILE, ATT_HEADS
    chunk_of = jnp.arange(t, dtype=jnp.int32) // CHUNK
    bias = jnp.where(chunk_of[:, None] <= chunk_of[None, :], 0.0, -jnp.inf).astype(F32)
    vec = pl.BlockSpec((1, DH_D), lambda b, h, i: (0, 0))
    return pl.pallas_call(
        functools.partial(_diff_prompt_kernel, lam_init=lam_init),
        grid=(nb, H_D // nh, seq_len // t),
        in_specs=[pl.BlockSpec((1, seq_len // t, nh, 2 * DH_D, t), lambda b, h, i: (b, 0, h, 0, 0)),
                  pl.BlockSpec((1, seq_len, nh * 2 * DH_D), lambda b, h, i: (b, 0, h)),
                  pl.BlockSpec((1, seq_len // t, nh, VT_ROWS, t), lambda b, h, i: (b, 0, h, 0, 0)),
                  pl.BlockSpec((t, t), lambda b, h, i: (0, 0), pipeline_mode=pl.Buffered(1)),
                  vec, vec, vec, vec,
                  pl.BlockSpec((DV_D, 1), lambda b, h, i: (0, 0))],
        out_specs=pl.BlockSpec((1, t, nh * DV_D), lambda b, h, i: (b, i, h)),
        out_shape=jax.ShapeDtypeStruct((nb, seq_len, DV), BF16),
        scratch_shapes=[pltpu.VMEM((2 * nh, 2 * DH_D, t), BF16), pltpu.VMEM((2 * nh, 1, t), F32),
                        pltpu.VMEM((2 * nh, VT_ROWS, t), F32), pltpu.VMEM((2, 2 * nh, t, t), F32)],
        compiler_params=pltpu.CompilerParams(dimension_semantics=("arbitrary", "arbitrary", "arbitrary"),
                                             vmem_limit_bytes=VMEM_LIMIT),
        name="diff_prompt",
    )(dqt, dkb, dvt, bias, *lams, g_col)


def _decode_attention(q_ref, kc_ref, vc_ref, kn_ref, vn_ref, lq1_ref, lk1_ref, lq2_ref, lk2_ref, g_ref,
                      o_ref, *, lam_init):
    T = q_ref.shape[1]
    P = kc_ref.shape[1] // H_D
    lam = _lambda(lq1_ref, lk1_ref, lq2_ref, lk2_ref, lam_init)
    lane = lax.broadcasted_iota(jnp.int32, (T, 2 * DH_D), 1)
    new_col = lax.broadcasted_iota(jnp.int32, (2 * T, LANES), 1)
    pad = jnp.zeros((LANES - T, 2 * DH_D), BF16)
    for h in range(H_D):
        cols = slice(h * 2 * DH_D, (h + 1) * 2 * DH_D)
        q = q_ref[0, :, cols].astype(F32)
        qs = jnp.concatenate([jnp.where(lane < DH_D, q, 0.0), jnp.where(lane >= DH_D, q, 0.0)],
                             axis=0).astype(BF16)
        kc = kc_ref[0, pl.ds(h, P, stride=H_D), :].astype(BF16)
        vc = vc_ref[0, pl.ds(h, P, stride=H_D), :].astype(BF16)
        kn = jnp.concatenate([kn_ref[0, pl.ds(h, T, stride=H_D), :].astype(BF16), pad], axis=0)
        vn = jnp.concatenate([vn_ref[0, pl.ds(h, T, stride=H_D), :].astype(BF16), pad], axis=0)
        sc = _dot_nt(qs, kc)
        sn = jnp.where(new_col < T, _dot_nt(qs, kn), -jnp.inf)
        m = jnp.maximum(jnp.max(sc, axis=-1, keepdims=True), jnp.max(sn, axis=-1, keepdims=True))
        pc = jnp.exp(sc - m)
        pn = jnp.exp(sn - m)
        l = jnp.sum(pc, axis=-1, keepdims=True) + jnp.sum(pn, axis=-1, keepdims=True)
        o2 = (_dot(pc.astype(BF16), vc) + _dot(pn.astype(BF16), vn)) / l
        o = o2[:T] - lam * o2[T:]
        o_ref[0, :, cols] = (_rms(o, g_ref[...]) * (1.0 - lam_init)).astype(BF16)


def kernel(x_prompt, x_sample, state_gla, cache_diff_k, cache_diff_v, w_in, w_gate_a2, b_gate_a, gla_norm_g, lambda_q1, lambda_k1, lambda_q2, lambda_k2, diff_norm_g, w_out, mix_pre_g, mix_post_g, ffn1_pre_g, ffn1_post_g, ffn1_w_gate, ffn1_w_up, ffn1_w_down, ffn2_pre_g, ffn2_post_g, ffn2_w_gate, ffn2_w_up, ffn2_w_down):
    B, S, _ = x_prompt.shape
    Bs, T, _ = x_sample.shape
    depth = w_in.shape[0]
    P = cache_diff_k.shape[2]
    xp = x_prompt.reshape(B * S, D_MODEL)
    xs = x_sample.reshape(Bs * T, D_MODEL)
    cache_k = cache_diff_k.reshape(depth * Bs, P * H_D, LANES)
    cache_v = cache_diff_v.reshape(depth * Bs, P * H_D, LANES)
    outs = [[] for _ in range(6)]
    for l in range(depth):
        lam_init = 0.8 - 0.6 * math.exp(-0.3 * l)
        row = lambda v: v[l].reshape(1, -1)
        o_gr = 2 * GQ + GV
        o_gg = o_gr + GATE_RANK
        o_dq = o_gg + GV
        wi = w_in[l]
        w = {
            "gla": jnp.concatenate([wi[:, :o_gr], wi[:, o_gg:o_dq]], axis=1).astype(BF16),
            "gr": jnp.pad(wi[:, o_gr:o_gg], ((0, 0), (0, LANES - GATE_RANK))).astype(BF16),
            "a2": jnp.pad(w_gate_a2[l], ((0, LANES - GATE_RANK), (0, 0))).astype(BF16),
            "ba": row(b_gate_a),
            "dq": wi[:, o_dq:o_dq + DQ].astype(BF16),
            "dk": wi[:, o_dq + DQ:o_dq + 2 * DQ].astype(BF16),
            "dv": wi[:, o_dq + 2 * DQ:].astype(BF16),
        }
        ffn1 = (row(ffn1_pre_g), row(ffn1_post_g), ffn1_w_gate[l].astype(BF16),
                ffn1_w_up[l].astype(BF16), ffn1_w_down[l].astype(BF16))
        ffn2 = (row(ffn2_pre_g), row(ffn2_post_g), ffn2_w_gate[l].astype(BF16),
                ffn2_w_up[l].astype(BF16), ffn2_w_down[l].astype(BF16))
        wo = w_out[l].astype(BF16)
        wog, wod = wo[:GV], wo[GV:]
        lams = (row(lambda_q1), row(lambda_k1), row(lambda_q2), row(lambda_k2))
        gla_g = row(gla_norm_g)

        x1s = _ffn_call(xs, *ffn1)
        gq_s, gk_s, gv_s, gg_s, la_s, dq_s, dkf_s, _, dvf_s = _inproj_call(x1s, row(mix_pre_g), w, T, False)
        s3 = lambda a: a.reshape(Bs, T, a.shape[-1])
        decode_attn = (s3(dq_s), cache_k, cache_v, dkf_s.reshape(Bs, T * H_D, LANES),
                       dvf_s.reshape(Bs, T * H_D, LANES), l * Bs, lams, row(diff_norm_g), lam_init)

        x1, d_out_s = _ffn_call(xp, *ffn1, decode_attn=decode_attn)
        gq, gk, gv, gg, la, dqt, dkf, dkb, dvf, dvt = _inproj_call(x1, row(mix_pre_g), w, S, True)
        b3 = lambda a: a.reshape(B, S, a.shape[-1])
        g_out, s_p = _gla_call(b3(gq), b3(gk), b3(gv), b3(la), b3(gg), gla_g,
                               jnp.zeros((B, GQ, DV_G), F32), CHUNK, GLA_ROWS, GLA_PROMPT_BATCH)
        d_out = _diff_prompt_call(dqt, b3(dkb), dvt, lams, diff_norm_g[l].reshape(DV_D, 1), lam_init)
        xp = _ffn_call(x1, *ffn2, merge=(g_out.reshape(B * S, GV), d_out.reshape(B * S, DV),
                                         wog, wod, row(mix_post_g)))
        outs[0].append(s_p.reshape(B, H_G, DK_G, DV_G))
        outs[1].append(dkf.reshape(B, S, H_D, 2 * DH_D))
        outs[2].append(dvf.reshape(B, S, H_D, DV_D))

        g_out_s, s_s = _gla_call(s3(gq_s), s3(gk_s), s3(gv_s), s3(la_s), s3(gg_s), gla_g,
                                 state_gla[l].reshape(Bs, GQ, DV_G), T, T, GLA_DECODE_BATCH)
        xs = _ffn_call(x1s, *ffn2, merge=(g_out_s.reshape(Bs * T, GV), d_out_s.reshape(Bs * T, DV),
                                          wog, wod, row(mix_post_g)))
        outs[3].append(s_s.reshape(Bs, H_G, DK_G, DV_G))
        outs[4].append(dkf_s.reshape(Bs, T, H_D, 2 * DH_D))
        outs[5].append(dvf_s.reshape(Bs, T, H_D, DV_D))

    sg_p, k_p, v_p, sg_s, k_s, v_s = (jnp.stack(o) for o in outs)
    return (xp.reshape(B, S, D_MODEL), xs.reshape(Bs, T, D_MODEL), sg_p, k_p, v_p, sg_s, k_s, v_s)
```

```python
import functools
import math

import jax
import jax.numpy as jnp
from jax import lax
from jax.experimental import pallas as pl
from jax.experimental.pallas import tpu as pltpu

F32 = jnp.float32
BF16 = jnp.bfloat16

D_MODEL = 1024
D_FF = 2816
CHUNK = 64
H_G, DK_G, DV_G = 4, 64, 128
GATE_RANK = 16
GATE_TAU = 16.0
H_D, DH_D, DV_D = 4, 64, 128
EPS = 1e-6

GQ = H_G * DK_G
GV = H_G * DV_G
DQ = H_D * 2 * DH_D
DV = H_D * DV_D

LANES = 128
MXU_N = 256
TOKEN_TILE = 1024
FFN_TILE = 1024
FFN_SUB = 512
FF_TILE = MXU_N
ATT_TILE = 512
ATT_HEADS = 4
VT_ROWS = DV_D + 16
GLA_ROWS = 256
GLA_PROMPT_BATCH = 8
GLA_DECODE_BATCH = 8
GLA_EXP_RANGE = 115.0
VMEM_LIMIT = 56 * 1024 * 1024


def _dot(a, b):
    return jnp.dot(a, b, preferred_element_type=F32)


def _dot_nt(a, b):
    return lax.dot_general(a, b, (((1,), (1,)), ((), ())), preferred_element_type=F32)


def _rms(x, g):
    r = lax.rsqrt(jnp.mean(x * x, axis=-1, keepdims=True) + EPS)
    return x * r * g


def _silu(x):
    return x * jax.nn.sigmoid(x)


def _const_spec(shape):
    nd = len(shape)
    return pl.BlockSpec(shape, lambda *_: (0,) * nd, pipeline_mode=pl.Buffered(1))


def _swiglu_half(x, pre_ref, post_ref, wg_ref, wu_ref, wd_ref, h_ref, act_ref):
    h_ref[...] = _rms(x, pre_ref[...]).astype(BF16)
    for j in range(D_FF // FF_TILE):
        cols = slice(j * FF_TILE, (j + 1) * FF_TILE)
        h = h_ref[...]
        gate = _dot(h, wg_ref[:, cols])
        up = _dot(h, wu_ref[:, cols])
        act_ref[:, cols] = (_silu(gate) * up).astype(BF16)
    f = _dot(act_ref[...], wd_ref[...])
    return x + 0.5 * _rms(f, post_ref[...])


def _sub_tiles(ref):
    return [slice(s * FFN_SUB, (s + 1) * FFN_SUB) for s in range(ref.shape[0] // FFN_SUB)]


def _ffn_kernel(x_ref, pre_ref, post_ref, wg_ref, wu_ref, wd_ref, o_ref, h_ref, act_ref):
    for s, rows in enumerate(_sub_tiles(x_ref)):
        o_ref[rows, :] = _swiglu_half(x_ref[rows, :], pre_ref, post_ref, wg_ref, wu_ref, wd_ref,
                                      h_ref.at[s], act_ref.at[s])


def _ffn_decode_attn_kernel(x_ref, pre_ref, post_ref, wg_ref, wu_ref, wd_ref, *rest, lam_init):
    att_refs, (o_ref, att_o_ref, h_ref, act_ref) = rest[:-4], rest[-4:]
    _ffn_kernel(x_ref, pre_ref, post_ref, wg_ref, wu_ref, wd_ref, o_ref, h_ref, act_ref)
    _decode_attention(*att_refs, att_o_ref, lam_init=lam_init)


def _merge_ffn_kernel(x_ref, g_ref, d_ref, wog_ref, wod_ref, mixg_ref,
                      pre_ref, post_ref, wg_ref, wu_ref, wd_ref, o_ref, h_ref, act_ref, x2_ref):
    for s, rows in enumerate(_sub_tiles(x_ref)):
        y = _dot(g_ref[rows, :], wog_ref[...]) + _dot(d_ref[rows, :], wod_ref[...])
        x2_ref[s] = x_ref[rows, :] + _rms(y, mixg_ref[...])
    for s, rows in enumerate(_sub_tiles(x_ref)):
        o_ref[rows, :] = _swiglu_half(x2_ref[s], pre_ref, post_ref, wg_ref, wu_ref, wd_ref,
                                      h_ref.at[s], act_ref.at[s])


def _ffn_call(x, pre_g, post_g, wg, wu, wd, merge=None, decode_attn=None):
    m = x.shape[0]
    tm = min(FFN_TILE, m)
    nsub = tm // FFN_SUB
    row = lambda w: pl.BlockSpec((tm, w), lambda i: (i, 0))
    ffn_specs = [_const_spec((1, D_MODEL)), _const_spec((1, D_MODEL)),
                 _const_spec((D_MODEL, D_FF)), _const_spec((D_MODEL, D_FF)), _const_spec((D_FF, D_MODEL))]
    scratch = [pltpu.VMEM((nsub, FFN_SUB, D_MODEL), BF16), pltpu.VMEM((nsub, FFN_SUB, D_FF), BF16)]
    out_specs, out_shape = row(D_MODEL), jax.ShapeDtypeStruct((m, D_MODEL), F32)
    if decode_attn is not None:
        dq, kc, vc, kn, vn, cache_base, lams, g_row, lam_init = decode_attn
        nb, T, _ = dq.shape
        assert merge is None and nb == m // tm, "one decode batch row per FFN grid step"
        new_q = pl.BlockSpec((1, T, DQ), lambda b: (b, 0, 0))
        new_kv = pl.BlockSpec((1, T * H_D, LANES), lambda b: (b, 0, 0))
        cache = pl.BlockSpec((1, kc.shape[1], LANES), lambda b: (cache_base + b, 0, 0))
        vec = pl.BlockSpec((1, DH_D), lambda b: (0, 0))
        kern = functools.partial(_ffn_decode_attn_kernel, lam_init=lam_init)
        ins = (x, pre_g, post_g, wg, wu, wd, dq, kc, vc, kn, vn, *lams, g_row)
        in_specs = ([row(D_MODEL)] + ffn_specs + [new_q, cache, cache, new_kv, new_kv, vec, vec, vec, vec,
                                                 pl.BlockSpec((1, DV_D), lambda b: (0, 0))])
        out_specs = [out_specs, new_q]
        out_shape = [out_shape, jax.ShapeDtypeStruct((nb, T, DV), BF16)]
    elif merge is None:
        kern, ins = _ffn_kernel, (x, pre_g, post_g, wg, wu, wd)
        in_specs = [row(D_MODEL)] + ffn_specs
    else:
        g_out, d_out, wog, wod, mix_g = merge
        kern, ins = _merge_ffn_kernel, (x, g_out, d_out, wog, wod, mix_g, pre_g, post_g, wg, wu, wd)
        in_specs = ([row(D_MODEL), row(GV), row(DV), _const_spec((GV, D_MODEL)),
                     _const_spec((DV, D_MODEL)), _const_spec((1, D_MODEL))] + ffn_specs)
        scratch = scratch + [pltpu.VMEM((nsub, FFN_SUB, D_MODEL), F32)]
    return pl.pallas_call(
        kern,
        grid=(m // tm,),
        in_specs=in_specs,
        out_specs=out_specs,
        out_shape=out_shape,
        scratch_shapes=scratch,
        compiler_params=pltpu.CompilerParams(dimension_semantics=("arbitrary",),
                                             vmem_limit_bytes=VMEM_LIMIT),
        name="merge_ffn" if merge is not None else ("ffn_decode_attn" if decode_attn is not None else "ffn"),
    )(*ins)


def _store_heads(ref, x, row0):
    rows = x.shape[0]
    for h in range(H_D):
        ref[pl.ds(row0 * H_D + h, rows, stride=H_D), :] = x[:, h * LANES:(h + 1) * LANES]


def _inproj_kernel(x_ref, mixpre_ref, wgla_ref, wgr_ref, wa2_ref, ba_ref, wdq_ref, wdk_ref, wdv_ref,
                   gq_ref, gk_ref, gv_ref, gg_ref, la_ref, dq_ref, dkf_ref, dkb_ref, dvf_ref,
                   *maybe_dvt_ref, transposed):
    t = ATT_TILE
    for s in range(x_ref.shape[0] // t):
        rows = slice(s * t, (s + 1) * t)
        h = _rms(x_ref[rows, :], mixpre_ref[...]).astype(BF16)
        dv = _dot(h, wdv_ref[...])
        _store_heads(dvf_ref, dv, s * t)
        dq = _dot(h, wdq_ref[...])
        if transposed:
            (dvt_ref,) = maybe_dvt_ref
            dvt = dv.T.astype(BF16)
            dqt = (dq * (DH_D ** -0.5 * math.log2(math.e))).T.astype(BF16)
            for hd in range(H_D):
                dq_ref[0, s, hd] = dqt[hd * 2 * DH_D:(hd + 1) * 2 * DH_D]
                dvt_ref[0, s, hd, :DV_D, :] = dvt[hd * DV_D:(hd + 1) * DV_D]
                dvt_ref[0, s, hd, DV_D:, :] = jnp.ones((VT_ROWS - DV_D, t), BF16)
        else:
            dq_ref[rows, :] = (dq * (DH_D ** -0.5)).astype(BF16)
        dk = _dot(h, wdk_ref[...])
        _store_heads(dkf_ref, dk, s * t)
        dkb_ref[rows, :] = dk.astype(BF16)
        gr = _dot(h, wgr_ref[...]).astype(BF16)
        gq_ref[rows, :] = (_dot(h, wgla_ref[:, 0:GQ]) * (DK_G ** -0.5)).astype(BF16)
        a = _dot(gr, wa2_ref[...]) + ba_ref[...]
        gk_ref[rows, :] = _dot(h, wgla_ref[:, GQ:2 * GQ]).astype(BF16)
        la_ref[rows, :] = (jnp.minimum(a, 0.0) - jnp.log1p(jnp.exp(-jnp.abs(a)))) * (math.log2(math.e) / GATE_TAU)
        gv_ref[rows, :] = _dot(h, wgla_ref[:, 2 * GQ:2 * GQ + GV]).astype(BF16)
        gg_ref[rows, :] = _dot(h, wgla_ref[:, 2 * GQ + GV:2 * GQ + 2 * GV]).astype(BF16)


def _inproj_call(x, mixpre_g, w, seq_len, transposed):
    m = x.shape[0]
    tm = min(TOKEN_TILE, m)
    nb = m // seq_len
    row = lambda wd: pl.BlockSpec((tm, wd), lambda i: (i, 0))
    heads = pl.BlockSpec((tm * H_D, LANES), lambda i: (i, 0))
    out_shape = [jax.ShapeDtypeStruct((m, GQ), BF16), jax.ShapeDtypeStruct((m, GQ), BF16),
                 jax.ShapeDtypeStruct((m, GV), BF16), jax.ShapeDtypeStruct((m, GV), BF16),
                 jax.ShapeDtypeStruct((m, GQ), F32)]
    out_specs = [row(GQ), row(GQ), row(GV), row(GV), row(GQ)]
    if transposed:
        t = ATT_TILE
        nq, per_step = seq_len // t, tm // t
        spb = nq // per_step
        out_shape.append(jax.ShapeDtypeStruct((nb, nq, H_D, 2 * DH_D, t), BF16))
        out_specs.append(pl.BlockSpec((1, per_step, H_D, 2 * DH_D, t), lambda i: (i // spb, i % spb, 0, 0, 0)))
    else:
        out_shape.append(jax.ShapeDtypeStruct((m, DQ), BF16))
        out_specs.append(row(DQ))
    out_shape += [jax.ShapeDtypeStruct((m * H_D, LANES), F32), jax.ShapeDtypeStruct((m, DQ), BF16),
                  jax.ShapeDtypeStruct((m * H_D, LANES), F32)]
    out_specs += [heads, row(DQ), heads]
    if transposed:
        out_shape.append(jax.ShapeDtypeStruct((nb, nq, H_D, VT_ROWS, t), BF16))
        out_specs.append(pl.BlockSpec((1, per_step, H_D, VT_ROWS, t), lambda i: (i // spb, i % spb, 0, 0, 0)))
    in_specs = [row(D_MODEL), _const_spec((1, D_MODEL)), _const_spec((D_MODEL, 2 * GQ + 2 * GV)),
                _const_spec((D_MODEL, LANES)), _const_spec((LANES, GQ)), _const_spec((1, GQ)),
                _const_spec((D_MODEL, DQ)), _const_spec((D_MODEL, DQ)), _const_spec((D_MODEL, DV))]
    return pl.pallas_call(
        functools.partial(_inproj_kernel, transposed=transposed),
        grid=(m // tm,),
        in_specs=in_specs,
        out_specs=out_specs,
        out_shape=out_shape,
        compiler_params=pltpu.CompilerParams(dimension_semantics=("arbitrary",),
                                             vmem_limit_bytes=VMEM_LIMIT),
        name="inproj_t" if transposed else "inproj",
    )(x, mixpre_g, w["gla"], w["gr"], w["a2"], w["ba"], w["dq"], w["dk"], w["dv"])


def _cumsum_rows(tril, x):
    hi = x.astype(BF16)
    r = x - hi.astype(F32)
    mid = r.astype(BF16)
    lo = (r - mid.astype(F32)).astype(BF16)
    return _dot(tril, hi) + _dot(tril, mid) + _dot(tril, lo)


def _gla_exact_scores(q_ref, k_ref, b_ref, a_ref, bi, c, slot, L):
    rows = slice(c * L, (c + 1) * L)
    q = q_ref[bi, rows, :].astype(F32)
    k = k_ref[bi, rows, :].astype(F32)
    b = b_ref[slot]
    row_id = lax.broadcasted_iota(jnp.int32, (L, GQ), 0)
    col_id = lax.broadcasted_iota(jnp.int32, (16, L), 1)
    head_of_lane = lax.broadcasted_iota(jnp.int32, (16, GQ), 1) >> int(math.log2(DK_G))
    head_ind = (head_of_lane == lax.broadcasted_iota(jnp.int32, (16, GQ), 0)).astype(BF16)

    def one_row(t, carry):
        sel = row_id == t
        qt = jnp.sum(jnp.where(sel, q, 0.0), axis=0, keepdims=True)
        bt = jnp.sum(jnp.where(sel, b, 0.0), axis=0, keepdims=True)
        prod = qt * k * jnp.exp2(jnp.minimum(bt - b, 0.0))
        hi = prod.astype(BF16)
        lo = (prod - hi.astype(F32)).astype(BF16)
        per_head = _dot_nt(head_ind, hi) + _dot_nt(head_ind, lo)
        per_head = jnp.where(col_id <= t, per_head, 0.0)
        for h in range(H_G):
            a_ref[slot, pl.ds(h * L + t, 1), :] = per_head[h:h + 1, :]
        return carry

    lax.fori_loop(0, L, one_row, 0)


def _gla_kernel(q_ref, k_ref, v_ref, la_ref, gg_ref, g_ref, s0_ref, out_ref, sfin_ref,
                s_ref, a_ref, b_ref, *, L, nchunks, nbatch):
    i = pl.program_id(1)

    @pl.when(i == 0)
    def _():
        s_ref[...] = s0_ref[...]

    causal = (lax.broadcasted_iota(jnp.int32, (H_G * L, L), 1)
              <= (lax.broadcasted_iota(jnp.int32, (H_G * L, L), 0) & (L - 1)))
    lane_head = lax.broadcasted_iota(jnp.int32, (L, GQ), 1) >> int(math.log2(DK_G))
    tr = lax.broadcasted_iota(jnp.int32, (L, L), 0)
    tc = lax.broadcasted_iota(jnp.int32, (L, L), 1)
    tril = (tc <= tr).astype(BF16)

    def stack_heads(x):
        return jnp.concatenate([jnp.where(lane_head == h, x, 0.0) for h in range(H_G)], axis=0).astype(BF16)

    work = [(c, bi, bi * nchunks + c) for c in range(nchunks) for bi in range(nbatch)]
    for c, bi, slot in work:
        rows = slice(c * L, (c + 1) * L)
        b = _cumsum_rows(tril, la_ref[bi, rows, :])
        b_ref[slot] = b
        b_mid = b[L // 2 - 1:L // 2, :]
        q_mid = stack_heads(q_ref[bi, rows, :].astype(F32) * jnp.exp2(b - b_mid))
        k_mid = (k_ref[bi, rows, :].astype(F32) * jnp.exp2(b_mid - b)).astype(BF16)
        a_ref[slot] = jnp.where(causal, _dot_nt(q_mid, k_mid), 0.0)

    @pl.when(jnp.min(la_ref[...]) < -(2.0 * GLA_EXP_RANGE / L))
    def _():
        for c, bi, slot in work:
            _gla_exact_scores(q_ref, k_ref, b_ref, a_ref, bi, c, slot, L)

    states = [s_ref[bi] for bi in range(nbatch)]
    for c, bi, slot in work:
        rows = slice(c * L, (c + 1) * L)
        s = states[bi]
        b = b_ref[slot]
        b_end = b[L - 1:L, :]
        q = q_ref[bi, rows, :].astype(F32)
        k = k_ref[bi, rows, :].astype(F32)
        v = v_ref[bi, rows, :]
        q_abs = stack_heads(q * jnp.exp2(b))
        k_end = k * jnp.exp2(b_end - b)
        decay = jnp.exp2(b_end)
        a = a_ref[slot].astype(BF16)
        o_state = _dot(q_abs, s.astype(BF16))

        kt = jnp.concatenate([k_end, jnp.broadcast_to(decay, (8, GQ)),
                              jnp.zeros((LANES - L - 8, GQ), F32)], axis=0).T
        kt_b = kt.astype(BF16)
        v_pad = jnp.concatenate([v, jnp.zeros((LANES - L, GV), BF16)], axis=0)
        decay_col = kt[:, L:L + 1]
        states[bi] = jnp.concatenate(
            [decay_col[h * DK_G:(h + 1) * DK_G] * s[h * DK_G:(h + 1) * DK_G]
             + _dot(kt_b[h * DK_G:(h + 1) * DK_G], v_pad[:, h * DV_G:(h + 1) * DV_G])
             for h in range(H_G)], axis=0)

        for h in range(H_G):
            cols = slice(h * DV_G, (h + 1) * DV_G)
            o = _dot(a[h * L:(h + 1) * L], v[:, cols]) + o_state[h * L:(h + 1) * L]
            gate = gg_ref[bi, rows, cols].astype(F32)
            out_ref[bi, rows, cols] = (_rms(o, g_ref[...]) * _silu(gate)).astype(BF16)

    for bi in range(nbatch):
        s_ref[bi] = states[bi]

    @pl.when(i == pl.num_programs(1) - 1)
    def _():
        for bi in range(nbatch):
            sfin_ref[bi] = states[bi]


def _gla_call(gq, gk, gv, la, gg, gla_g, s0, L, rows_per_step, nbatch):
    nb, seq_len = gq.shape[0], gq.shape[1]
    nchunks = rows_per_step // L
    blk = lambda w: pl.BlockSpec((nbatch, rows_per_step, w), lambda b, i: (b, i, 0))
    state = pl.BlockSpec((nbatch, GQ, DV_G), lambda b, i: (b, 0, 0))
    return pl.pallas_call(
        functools.partial(_gla_kernel, L=L, nchunks=nchunks, nbatch=nbatch),
        grid=(nb // nbatch, seq_len // rows_per_step),
        in_specs=[blk(GQ), blk(GQ), blk(GV), blk(GQ), blk(GV),
                  pl.BlockSpec((1, DV_G), lambda b, i: (0, 0)), state],
        out_specs=[blk(GV), state],
        out_shape=[jax.ShapeDtypeStruct((nb, seq_len, GV), BF16),
                   jax.ShapeDtypeStruct((nb, GQ, DV_G), F32)],
        scratch_shapes=[pltpu.VMEM((nbatch, GQ, DV_G), F32), pltpu.VMEM((nbatch * nchunks, H_G * L, L), F32),
                        pltpu.VMEM((nbatch * nchunks, L, GQ), F32)],
        compiler_params=pltpu.CompilerParams(dimension_semantics=("arbitrary", "arbitrary"),
                                             vmem_limit_bytes=VMEM_LIMIT),
        name="gla",
    )(gq, gk, gv, la, gg, gla_g, s0)


def _lambda(lq1_ref, lk1_ref, lq2_ref, lk2_ref, lam_init):
    s1 = jnp.sum(lq1_ref[...] * lk1_ref[...], axis=-1, keepdims=True)
    s2 = jnp.sum(lq2_ref[...] * lk2_ref[...], axis=-1, keepdims=True)
    return jnp.exp(s1) - jnp.exp(s2) + lam_init


def _diff_prompt_kernel(qt_ref, k_ref, vt_ref, bias_ref, lq1_ref, lk1_ref, lq2_ref, lk2_ref, g_ref, o_ref,
                        rhs_ref, m_ref, acc_ref, s_ref, *, lam_init):
    t = ATT_TILE
    i = pl.program_id(2)
    nq = pl.num_programs(2)

    chains = range(2 * ATT_HEADS)

    def half_queries(iq):
        qrow = lax.broadcasted_iota(jnp.int32, (2 * DH_D, t), 0)
        out = []
        for hh in range(ATT_HEADS):
            qt = qt_ref[0, iq, hh].astype(F32)
            out += [jnp.where((qrow >= c * DH_D) & (qrow < (c + 1) * DH_D), qt, 0.0).astype(BF16) for c in range(2)]
        return out

    def keys(j, hh):
        return k_ref[0, pl.ds(pl.multiple_of(j * t, t), t), hh * 2 * DH_D:(hh + 1) * 2 * DH_D]

    def scores(j, ch):
        return _dot(keys(j, ch // 2), rhs_ref[ch])

    def step(j, parity, diagonal):
        for ch in chains:
            if diagonal:
                diagonal_chain(j, parity, ch)
                continue
            s_ref[1 - parity, ch] = scores(j + 1, ch)
            st = s_ref[parity, ch]
            m_old = m_ref[ch]
            m_new = jnp.maximum(m_old, jnp.max(st, axis=0, keepdims=True))
            alpha = jnp.exp2(m_old - m_new)
            p = jnp.exp2(st - m_new)
            acc_ref[ch] = alpha * acc_ref[ch] + _dot(vt_ref[0, j, ch // 2], p.astype(BF16))
            m_ref[ch] = m_new

    def diagonal_chain(j, parity, ch):
        u = t // 2
        top = s_ref[parity, ch, :u, :] + bias_ref[:u, :]
        bot = s_ref[parity, ch, u:, u:] + bias_ref[u:, u:]
        blk_max = jnp.max(top, axis=0, keepdims=True)
        blk_max = jnp.concatenate([blk_max[:, :u],
                                   jnp.maximum(blk_max[:, u:], jnp.max(bot, axis=0, keepdims=True))], axis=1)
        m_old = m_ref[ch]
        m_new = jnp.maximum(m_old, blk_max)
        alpha = jnp.exp2(m_old - m_new)
        p_top = jnp.exp2(top - m_new).astype(BF16)
        p_bot = jnp.exp2(bot - m_new[:, u:]).astype(BF16)
        vt = vt_ref[0, j, ch // 2]
        pv_bot = _dot(vt[:, u:], p_bot)
        pv = _dot(vt[:, :u], p_top)
        acc_ref[ch] = alpha * acc_ref[ch] + jnp.concatenate([pv[:, :u], pv[:, u:] + pv_bot], axis=1)
        m_ref[ch] = m_new

    def finish():
        q_next = half_queries(jnp.minimum(i + 1, nq - 1))
        for ch in chains:
            s_ref[0, ch] = _dot(keys(0, ch // 2), q_next[ch])
        lam = _lambda(lq1_ref, lk1_ref, lq2_ref, lk2_ref, lam_init)
        for hh in range(ATT_HEADS):
            a1, a2 = acc_ref[2 * hh], acc_ref[2 * hh + 1]
            ot = a1[:DV_D] * (1.0 / a1[DV_D:DV_D + 1]) - lam * (a2[:DV_D] * (1.0 / a2[DV_D:DV_D + 1]))
            r = lax.rsqrt(jnp.mean(ot * ot, axis=0, keepdims=True) + EPS)
            ot = ot * r * g_ref[...] * (1.0 - lam_init)
            o_ref[0, :, hh * DV_D:(hh + 1) * DV_D] = ot.T.astype(BF16)

    def pair(n, carry):
        step(2 * n, 0, False)
        step(2 * n + 1, 1, False)
        return carry

    q_now = half_queries(i)
    for ch in chains:
        rhs_ref[ch] = q_now[ch]

    @pl.when(i == 0)
    def _():
        for ch in chains:
            s_ref[0, ch] = scores(0, ch)

    m_ref[...] = jnp.full(m_ref.shape, -jnp.inf, F32)
    acc_ref[...] = jnp.zeros(acc_ref.shape, F32)
    lax.fori_loop(0, i // 2, pair, 0)

    @pl.when(i % 2 == 0)
    def _():
        step(i, 0, True)
        finish()

    @pl.when(i % 2 == 1)
    def _():
        step(i - 1, 0, False)
        step(i, 1, True)
        finish()


def _diff_prompt_call(dqt, dkb, dvt, lams, g_col, lam_init):
    nb, seq_len = dkb.shape[0], dkb.shape[1]
    t, nh = ATT_TILE, ATT_HEADS
    chunk_of = jnp.arange(t, dtype=jnp.int32) // CHUNK
    bias = jnp.where(chunk_of[:, None] <= chunk_of[None, :], 0.0, -jnp.inf).astype(F32)
    vec = pl.BlockSpec((1, DH_D), lambda b, h, i: (0, 0))
    return pl.pallas_call(
        functools.partial(_diff_prompt_kernel, lam_init=lam_init),
        grid=(nb, H_D // nh, seq_len // t),
        in_specs=[pl.BlockSpec((1, seq_len // t, nh, 2 * DH_D, t), lambda b, h, i: (b, 0, h, 0, 0)),
                  pl.BlockSpec((1, seq_len, nh * 2 * DH_D), lambda b, h, i: (b, 0, h)),
                  pl.BlockSpec((1, seq_len // t, nh, VT_ROWS, t), lambda b, h, i: (b, 0, h, 0, 0)),
                  pl.BlockSpec((t, t), lambda b, h, i: (0, 0), pipeline_mode=pl.Buffered(1)),
                  vec, vec, vec, vec,
                  pl.BlockSpec((DV_D, 1), lambda b, h, i: (0, 0))],
        out_specs=pl.BlockSpec((1, t, nh * DV_D), lambda b, h, i: (b, i, h)),
        out_shape=jax.ShapeDtypeStruct((nb, seq_len, DV), BF16),
        scratch_shapes=[pltpu.VMEM((2 * nh, 2 * DH_D, t), BF16), pltpu.VMEM((2 * nh, 1, t), F32),
                        pltpu.VMEM((2 * nh, VT_ROWS, t), F32), pltpu.VMEM((2, 2 * nh, t, t), F32)],
        compiler_params=pltpu.CompilerParams(dimension_semantics=("arbitrary", "arbitrary", "arbitrary"),
                                             vmem_limit_bytes=VMEM_LIMIT),
        name="diff_prompt",
    )(dqt, dkb, dvt, bias, *lams, g_col)


def _decode_attention(q_ref, kc_ref, vc_ref, kn_ref, vn_ref, lq1_ref, lk1_ref, lq2_ref, lk2_ref, g_ref,
                      o_ref, *, lam_init):
    T = q_ref.shape[1]
    P = kc_ref.shape[1] // H_D
    lam = _lambda(lq1_ref, lk1_ref, lq2_ref, lk2_ref, lam_init)
    lane = lax.broadcasted_iota(jnp.int32, (T, 2 * DH_D), 1)
    new_col = lax.broadcasted_iota(jnp.int32, (2 * T, LANES), 1)
    pad = jnp.zeros((LANES - T, 2 * DH_D), BF16)
    for h in range(H_D):
        cols = slice(h * 2 * DH_D, (h + 1) * 2 * DH_D)
        q = q_ref[0, :, cols].astype(F32)
        qs = jnp.concatenate([jnp.where(lane < DH_D, q, 0.0), jnp.where(lane >= DH_D, q, 0.0)],
                             axis=0).astype(BF16)
        kc = kc_ref[0, pl.ds(h, P, stride=H_D), :].astype(BF16)
        vc = vc_ref[0, pl.ds(h, P, stride=H_D), :].astype(BF16)
        kn = jnp.concatenate([kn_ref[0, pl.ds(h, T, stride=H_D), :].astype(BF16), pad], axis=0)
        vn = jnp.concatenate([vn_ref[0, pl.ds(h, T, stride=H_D), :].astype(BF16), pad], axis=0)
        sc = _dot_nt(qs, kc)
        sn = jnp.where(new_col < T, _dot_nt(qs, kn), -jnp.inf)
        m = jnp.maximum(jnp.max(sc, axis=-1, keepdims=True), jnp.max(sn, axis=-1, keepdims=True))
        pc = jnp.exp(sc - m)
        pn = jnp.exp(sn - m)
        l = jnp.sum(pc, axis=-1, keepdims=True) + jnp.sum(pn, axis=-1, keepdims=True)
        o2 = (_dot(pc.astype(BF16), vc) + _dot(pn.astype(BF16), vn)) / l
        o = o2[:T] - lam * o2[T:]
        o_ref[0, :, cols] = (_rms(o, g_ref[...]) * (1.0 - lam_init)).astype(BF16)


def kernel(x_prompt, x_sample, state_gla, cache_diff_k, cache_diff_v, w_in, w_gate_a2, b_gate_a, gla_norm_g, lambda_q1, lambda_k1, lambda_q2, lambda_k2, diff_norm_g, w_out, mix_pre_g, mix_post_g, ffn1_pre_g, ffn1_post_g, ffn1_w_gate, ffn1_w_up, ffn1_w_down, ffn2_pre_g, ffn2_post_g, ffn2_w_gate, ffn2_w_up, ffn2_w_down):
    B, S, _ = x_prompt.shape
    Bs, T, _ = x_sample.shape
    depth = w_in.shape[0]
    P = cache_diff_k.shape[2]
    xp = x_prompt.reshape(B * S, D_MODEL)
    xs = x_sample.reshape(Bs * T, D_MODEL)
    cache_k = cache_diff_k.reshape(depth * Bs, P * H_D, LANES)
    cache_v = cache_diff_v.reshape(depth * Bs, P * H_D, LANES)
    outs = [[] for _ in range(6)]
    for l in range(depth):
        lam_init = 0.8 - 0.6 * math.exp(-0.3 * l)
        row = lambda v: v[l].reshape(1, -1)
        o_gr = 2 * GQ + GV
        o_gg = o_gr + GATE_RANK
        o_dq = o_gg + GV
        wi = w_in[l]
        w = {
            "gla": jnp.concatenate([wi[:, :o_gr], wi[:, o_gg:o_dq]], axis=1).astype(BF16),
            "gr": jnp.pad(wi[:, o_gr:o_gg], ((0, 0), (0, LANES - GATE_RANK))).astype(BF16),
            "a2": jnp.pad(w_gate_a2[l], ((0, LANES - GATE_RANK), (0, 0))).astype(BF16),
            "ba": row(b_gate_a),
            "dq": wi[:, o_dq:o_dq + DQ].astype(BF16),
            "dk": wi[:, o_dq + DQ:o_dq + 2 * DQ].astype(BF16),
            "dv": wi[:, o_dq + 2 * DQ:].astype(BF16),
        }
        ffn1 = (row(ffn1_pre_g), row(ffn1_post_g), ffn1_w_gate[l].astype(BF16),
                ffn1_w_up[l].astype(BF16), ffn1_w_down[l].astype(BF16))
        ffn2 = (row(ffn2_pre_g), row(ffn2_post_g), ffn2_w_gate[l].astype(BF16),
                ffn2_w_up[l].astype(BF16), ffn2_w_down[l].astype(BF16))
        wo = w_out[l].astype(BF16)
        wog, wod = wo[:GV], wo[GV:]
        lams = (row(lambda_q1), row(lambda_k1), row(lambda_q2), row(lambda_k2))
        gla_g = row(gla_norm_g)

        x1s = _ffn_call(xs, *ffn1)
        gq_s, gk_s, gv_s, gg_s, la_s, dq_s, dkf_s, _, dvf_s = _inproj_call(x1s, row(mix_pre_g), w, T, False)
        s3 = lambda a: a.reshape(Bs, T, a.shape[-1])
        decode_attn = (s3(dq_s), cache_k, cache_v, dkf_s.reshape(Bs, T * H_D, LANES),
                       dvf_s.reshape(Bs, T * H_D, LANES), l * Bs, lams, row(diff_norm_g), lam_init)

        x1, d_out_s = _ffn_call(xp, *ffn1, decode_attn=decode_attn)
        gq, gk, gv, gg, la, dqt, dkf, dkb, dvf, dvt = _inproj_call(x1, row(mix_pre_g), w, S, True)
        b3 = lambda a: a.reshape(B, S, a.shape[-1])
        g_out, s_p = _gla_call(b3(gq), b3(gk), b3(gv), b3(la), b3(gg), gla_g,
                               jnp.zeros((B, GQ, DV_G), F32), CHUNK, GLA_ROWS, GLA_PROMPT_BATCH)
        d_out = _diff_prompt_call(dqt, b3(dkb), dvt, lams, diff_norm_g[l].reshape(DV_D, 1), lam_init)
        xp = _ffn_call(x1, *ffn2, merge=(g_out.reshape(B * S, GV), d_out.reshape(B * S, DV),
                                         wog, wod, row(mix_post_g)))
        outs[0].append(s_p.reshape(B, H_G, DK_G, DV_G))
        outs[1].append(dkf.reshape(B, S, H_D, 2 * DH_D))
        outs[2].append(dvf.reshape(B, S, H_D, DV_D))

        g_out_s, s_s = _gla_call(s3(gq_s), s3(gk_s), s3(gv_s), s3(la_s), s3(gg_s), gla_g,
                                 state_gla[l].reshape(Bs, GQ, DV_G), T, T, GLA_DECODE_BATCH)
        xs = _ffn_call(x1s, *ffn2, merge=(g_out_s.reshape(Bs * T, GV), d_out_s.reshape(Bs * T, DV),
                                          wog, wod, row(mix_post_g)))
        outs[3].append(s_s.reshape(Bs, H_G, DK_G, DV_G))
        outs[4].append(dkf_s.reshape(Bs, T, H_D, 2 * DH_D))
        outs[5].append(dvf_s.reshape(Bs, T, H_D, DV_D))

    sg_p, k_p, v_p, sg_s, k_s, v_s = (jnp.stack(o) for o in outs)
    return (xp.reshape(B, S, D_MODEL), xs.reshape(Bs, T, D_MODEL), sg_p, k_p, v_p, sg_s, k_s, v_s)
```

```python
import functools
import math

import jax
import jax.numpy as jnp
from jax import lax
from jax.experimental import pallas as pl
from jax.experimental.pallas import tpu as pltpu

F32 = jnp.float32
BF16 = jnp.bfloat16

D_MODEL = 1024
D_FF = 2816
CHUNK = 64
H_G, DK_G, DV_G = 4, 64, 128
GATE_RANK = 16
GATE_TAU = 16.0
H_D, DH_D, DV_D = 4, 64, 128
EPS = 1e-6

GQ = H_G * DK_G
GV = H_G * DV_G
DQ = H_D * 2 * DH_D
DV = H_D * DV_D

LANES = 128
MXU_N = 256
TOKEN_TILE = 1024
FFN_TILE = 1024
FFN_SUB = 512
FF_TILE = MXU_N
ATT_TILE = 512
ATT_HEADS = 4
VT_ROWS = DV_D + 16
GLA_ROWS = 256
GLA_PROMPT_BATCH = 8
GLA_DECODE_BATCH = 8
GLA_EXP_RANGE = 115.0
VMEM_LIMIT = 56 * 1024 * 1024


def _dot(a, b):
    return jnp.dot(a, b, preferred_element_type=F32)


def _dot_nt(a, b):
    return lax.dot_general(a, b, (((1,), (1,)), ((), ())), preferred_element_type=F32)


def _rms(x, g):
    r = lax.rsqrt(jnp.mean(x * x, axis=-1, keepdims=True) + EPS)
    return x * r * g


def _silu(x):
    return x * jax.nn.sigmoid(x)


def _const_spec(shape):
    nd = len(shape)
    return pl.BlockSpec(shape, lambda *_: (0,) * nd, pipeline_mode=pl.Buffered(1))


def _swiglu_half(x, pre_ref, post_ref, wg_ref, wu_ref, wd_ref, h_ref, act_ref):
    h_ref[...] = _rms(x, pre_ref[...]).astype(BF16)
    for j in range(D_FF // FF_TILE):
        cols = slice(j * FF_TILE, (j + 1) * FF_TILE)
        h = h_ref[...]
        gate = _dot(h, wg_ref[:, cols])
        up = _dot(h, wu_ref[:, cols])
        act_ref[:, cols] = (_silu(gate) * up).astype(BF16)
    f = _dot(act_ref[...], wd_ref[...])
    return x + 0.5 * _rms(f, post_ref[...])


def _sub_tiles(ref):
    return [slice(s * FFN_SUB, (s + 1) * FFN_SUB) for s in range(ref.shape[0] // FFN_SUB)]


def _ffn_kernel(x_ref, pre_ref, post_ref, wg_ref, wu_ref, wd_ref, o_ref, h_ref, act_ref):
    for s, rows in enumerate(_sub_tiles(x_ref)):
        o_ref[rows, :] = _swiglu_half(x_ref[rows, :], pre_ref, post_ref, wg_ref, wu_ref, wd_ref,
                                      h_ref.at[s], act_ref.at[s])


def _ffn_decode_attn_kernel(x_ref, pre_ref, post_ref, wg_ref, wu_ref, wd_ref, *rest, lam_init):
    att_refs, (o_ref, att_o_ref, h_ref, act_ref) = rest[:-4], rest[-4:]
    _ffn_kernel(x_ref, pre_ref, post_ref, wg_ref, wu_ref, wd_ref, o_ref, h_ref, act_ref)
    _decode_attention(*att_refs, att_o_ref, lam_init=lam_init)


def _merge_ffn_kernel(x_ref, g_ref, d_ref, wog_ref, wod_ref, mixg_ref,
                      pre_ref, post_ref, wg_ref, wu_ref, wd_ref, o_ref, h_ref, act_ref, x2_ref):
    for s, rows in enumerate(_sub_tiles(x_ref)):
        y = _dot(g_ref[rows, :], wog_ref[...]) + _dot(d_ref[rows, :], wod_ref[...])
        x2_ref[s] = x_ref[rows, :] + _rms(y, mixg_ref[...])
    for s, rows in enumerate(_sub_tiles(x_ref)):
        o_ref[rows, :] = _swiglu_half(x2_ref[s], pre_ref, post_ref, wg_ref, wu_ref, wd_ref,
                                      h_ref.at[s], act_ref.at[s])


def _ffn_call(x, pre_g, post_g, wg, wu, wd, merge=None, decode_attn=None):
    m = x.shape[0]
    tm = min(FFN_TILE, m)
    nsub = tm // FFN_SUB
    row = lambda w: pl.BlockSpec((tm, w), lambda i: (i, 0))
    ffn_specs = [_const_spec((1, D_MODEL)), _const_spec((1, D_MODEL)),
                 _const_spec((D_MODEL, D_FF)), _const_spec((D_MODEL, D_FF)), _const_spec((D_FF, D_MODEL))]
    scratch = [pltpu.VMEM((nsub, FFN_SUB, D_MODEL), BF16), pltpu.VMEM((nsub, FFN_SUB, D_FF), BF16)]
    out_specs, out_shape = row(D_MODEL), jax.ShapeDtypeStruct((m, D_MODEL), F32)
    if decode_attn is not None:
        dq, kc, vc, kn, vn, cache_base, lams, g_row, lam_init = decode_attn
        nb, T, _ = dq.shape
        assert merge is None and nb == m // tm, "one decode batch row per FFN grid step"
        new_q = pl.BlockSpec((1, T, DQ), lambda b: (b, 0, 0))
        new_kv = pl.BlockSpec((1, T * H_D, LANES), lambda b: (b, 0, 0))
        cache = pl.BlockSpec((1, kc.shape[1], LANES), lambda b: (cache_base + b, 0, 0))
        vec = pl.BlockSpec((1, DH_D), lambda b: (0, 0))
        kern = functools.partial(_ffn_decode_attn_kernel, lam_init=lam_init)
        ins = (x, pre_g, post_g, wg, wu, wd, dq, kc, vc, kn, vn, *lams, g_row)
        in_specs = ([row(D_MODEL)] + ffn_specs + [new_q, cache, cache, new_kv, new_kv, vec, vec, vec, vec,
                                                 pl.BlockSpec((1, DV_D), lambda b: (0, 0))])
        out_specs = [out_specs, new_q]
        out_shape = [out_shape, jax.ShapeDtypeStruct((nb, T, DV), BF16)]
    elif merge is None:
        kern, ins = _ffn_kernel, (x, pre_g, post_g, wg, wu, wd)
        in_specs = [row(D_MODEL)] + ffn_specs
    else:
        g_out, d_out, wog, wod, mix_g = merge
        kern, ins = _merge_ffn_kernel, (x, g_out, d_out, wog, wod, mix_g, pre_g, post_g, wg, wu, wd)
        in_specs = ([row(D_MODEL), row(GV), row(DV), _const_spec((GV, D_MODEL)),
                     _const_spec((DV, D_MODEL)), _const_spec((1, D_MODEL))] + ffn_specs)
        scratch = scratch + [pltpu.VMEM((nsub, FFN_SUB, D_MODEL), F32)]
    return pl.pallas_call(
        kern,
        grid=(m // tm,),
        in_specs=in_specs,
        out_specs=out_specs,
        out_shape=out_shape,
        scratch_shapes=scratch,
        compiler_params=pltpu.CompilerParams(dimension_semantics=("arbitrary",),
                                             vmem_limit_bytes=VMEM_LIMIT),
        name="merge_ffn" if merge is not None else ("ffn_decode_attn" if decode_attn is not None else "ffn"),
    )(*ins)


def _store_heads(ref, x, row0):
    rows = x.shape[0]
    for h in range(H_D):
        ref[pl.ds(row0 * H_D + h, rows, stride=H_D), :] = x[:, h * LANES:(h + 1) * LANES]


def _inproj_kernel(x_ref, mixpre_ref, wgla_ref, wgr_ref, wa2_ref, ba_ref, wdq_ref, wdk_ref, wdv_ref,
                   gq_ref, gk_ref, gv_ref, gg_ref, la_ref, dq_ref, dkf_ref, dkb_ref, dvf_ref,
                   *maybe_dvt_ref, transposed):
    t = ATT_TILE
    for s in range(x_ref.shape[0] // t):
        rows = slice(s * t, (s + 1) * t)
        h = _rms(x_ref[rows, :], mixpre_ref[...]).astype(BF16)
        dv = _dot(h, wdv_ref[...])
        _store_heads(dvf_ref, dv, s * t)
        dq = _dot(h, wdq_ref[...])
        if transposed:
            (dvt_ref,) = maybe_dvt_ref
            dvt = dv.T.astype(BF16)
            dqt = (dq * (DH_D ** -0.5 * math.log2(math.e))).T.astype(BF16)
            for hd in range(H_D):
                dq_ref[0, s, hd] = dqt[hd * 2 * DH_D:(hd + 1) * 2 * DH_D]
                dvt_ref[0, s, hd, :DV_D, :] = dvt[hd * DV_D:(hd + 1) * DV_D]
                dvt_ref[0, s, hd, DV_D:, :] = jnp.ones((VT_ROWS - DV_D, t), BF16)
        else:
            dq_ref[rows, :] = (dq * (DH_D ** -0.5)).astype(BF16)
        dk = _dot(h, wdk_ref[...])
        _store_heads(dkf_ref, dk, s * t)
        dkb_ref[rows, :] = dk.astype(BF16)
        gr = _dot(h, wgr_ref[...]).astype(BF16)
        gq_ref[rows, :] = (_dot(h, wgla_ref[:, 0:GQ]) * (DK_G ** -0.5)).astype(BF16)
        a = _dot(gr, wa2_ref[...]) + ba_ref[...]
        gk_ref[rows, :] = _dot(h, wgla_ref[:, GQ:2 * GQ]).astype(BF16)
        la_ref[rows, :] = (jnp.minimum(a, 0.0) - jnp.log1p(jnp.exp(-jnp.abs(a)))) * (math.log2(math.e) / GATE_TAU)
        gv_ref[rows, :] = _dot(h, wgla_ref[:, 2 * GQ:2 * GQ + GV]).astype(BF16)
        gg_ref[rows, :] = _dot(h, wgla_ref[:, 2 * GQ + GV:2 * GQ + 2 * GV]).astype(BF16)


def _inproj_call(x, mixpre_g, w, seq_len, transposed):
    m = x.shape[0]
    tm = min(TOKEN_TILE, m)
    nb = m // seq_len
    row = lambda wd: pl.BlockSpec((tm, wd), lambda i: (i, 0))
    heads = pl.BlockSpec((tm * H_D, LANES), lambda i: (i, 0))
    out_shape = [jax.ShapeDtypeStruct((m, GQ), BF16), jax.ShapeDtypeStruct((m, GQ), BF16),
                 jax.ShapeDtypeStruct((m, GV), BF16), jax.ShapeDtypeStruct((m, GV), BF16),
                 jax.ShapeDtypeStruct((m, GQ), F32)]
    out_specs = [row(GQ), row(GQ), row(GV), row(GV), row(GQ)]
    if transposed:
        t = ATT_TILE
        nq, per_step = seq_len // t, tm // t
        spb = nq // per_step
        out_shape.append(jax.ShapeDtypeStruct((nb, nq, H_D, 2 * DH_D, t), BF16))
        out_specs.append(pl.BlockSpec((1, per_step, H_D, 2 * DH_D, t), lambda i: (i // spb, i % spb, 0, 0, 0)))
    else:
        out_shape.append(jax.ShapeDtypeStruct((m, DQ), BF16))
        out_specs.append(row(DQ))
    out_shape += [jax.ShapeDtypeStruct((m * H_D, LANES), F32), jax.ShapeDtypeStruct((m, DQ), BF16),
                  jax.ShapeDtypeStruct((m * H_D, LANES), F32)]
    out_specs += [heads, row(DQ), heads]
    if transposed:
        out_shape.append(jax.ShapeDtypeStruct((nb, nq, H_D, VT_ROWS, t), BF16))
        out_specs.append(pl.BlockSpec((1, per_step, H_D, VT_ROWS, t), lambda i: (i // spb, i % spb, 0, 0, 0)))
    in_specs = [row(D_MODEL), _const_spec((1, D_MODEL)), _const_spec((D_MODEL, 2 * GQ + 2 * GV)),
                _const_spec((D_MODEL, LANES)), _const_spec((LANES, GQ)), _const_spec((1, GQ)),
                _const_spec((D_MODEL, DQ)), _const_spec((D_MODEL, DQ)), _const_spec((D_MODEL, DV))]
    return pl.pallas_call(
        functools.partial(_inproj_kernel, transposed=transposed),
        grid=(m // tm,),
        in_specs=in_specs,
        out_specs=out_specs,
        out_shape=out_shape,
        compiler_params=pltpu.CompilerParams(dimension_semantics=("arbitrary",),
                                             vmem_limit_bytes=VMEM_LIMIT),
        name="inproj_t" if transposed else "inproj",
    )(x, mixpre_g, w["gla"], w["gr"], w["a2"], w["ba"], w["dq"], w["dk"], w["dv"])


def _cumsum_rows(tril, x):
    hi = x.astype(BF16)
    r = x - hi.astype(F32)
    mid = r.astype(BF16)
    lo = (r - mid.astype(F32)).astype(BF16)
    return _dot(tril, hi) + _dot(tril, mid) + _dot(tril, lo)


def _gla_exact_scores(q_ref, k_ref, b_ref, a_ref, bi, c, slot, L):
    rows = slice(c * L, (c + 1) * L)
    q = q_ref[bi, rows, :].astype(F32)
    k = k_ref[bi, rows, :].astype(F32)
    b = b_ref[slot]
    row_id = lax.broadcasted_iota(jnp.int32, (L, GQ), 0)
    col_id = lax.broadcasted_iota(jnp.int32, (16, L), 1)
    head_of_lane = lax.broadcasted_iota(jnp.int32, (16, GQ), 1) >> int(math.log2(DK_G))
    head_ind = (head_of_lane == lax.broadcasted_iota(jnp.int32, (16, GQ), 0)).astype(BF16)

    def one_row(t, carry):
        sel = row_id == t
        qt = jnp.sum(jnp.where(sel, q, 0.0), axis=0, keepdims=True)
        bt = jnp.sum(jnp.where(sel, b, 0.0), axis=0, keepdims=True)
        prod = qt * k * jnp.exp2(jnp.minimum(bt - b, 0.0))
        hi = prod.astype(BF16)
        lo = (prod - hi.astype(F32)).astype(BF16)
        per_head = _dot_nt(head_ind, hi) + _dot_nt(head_ind, lo)
        per_head = jnp.where(col_id <= t, per_head, 0.0)
        for h in range(H_G):
            a_ref[slot, pl.ds(h * L + t, 1), :] = per_head[h:h + 1, :]
        return carry

    lax.fori_loop(0, L, one_row, 0)


def _gla_kernel(q_ref, k_ref, v_ref, la_ref, gg_ref, g_ref, s0_ref, out_ref, sfin_ref,
                s_ref, a_ref, b_ref, *, L, nchunks, nbatch):
    i = pl.program_id(1)

    @pl.when(i == 0)
    def _():
        s_ref[...] = s0_ref[...]

    causal = (lax.broadcasted_iota(jnp.int32, (H_G * L, L), 1)
              <= (lax.broadcasted_iota(jnp.int32, (H_G * L, L), 0) & (L - 1)))
    lane_head = lax.broadcasted_iota(jnp.int32, (L, GQ), 1) >> int(math.log2(DK_G))
    tr = lax.broadcasted_iota(jnp.int32, (L, L), 0)
    tc = lax.broadcasted_iota(jnp.int32, (L, L), 1)
    tril = (tc <= tr).astype(BF16)

    def stack_heads(x):
        return jnp.concatenate([jnp.where(lane_head == h, x, 0.0) for h in range(H_G)], axis=0).astype(BF16)

    work = [(c, bi, bi * nchunks + c) for c in range(nchunks) for bi in range(nbatch)]
    for c, bi, slot in work:
        rows = slice(c * L, (c + 1) * L)
        b = _cumsum_rows(tril, la_ref[bi, rows, :])
        b_ref[slot] = b
        b_mid = b[L // 2 - 1:L // 2, :]
        q_mid = stack_heads(q_ref[bi, rows, :].astype(F32) * jnp.exp2(b - b_mid))
        k_mid = (k_ref[bi, rows, :].astype(F32) * jnp.exp2(b_mid - b)).astype(BF16)
        a_ref[slot] = jnp.where(causal, _dot_nt(q_mid, k_mid), 0.0)

    @pl.when(jnp.min(la_ref[...]) < -(2.0 * GLA_EXP_RANGE / L))
    def _():
        for c, bi, slot in work:
            _gla_exact_scores(q_ref, k_ref, b_ref, a_ref, bi, c, slot, L)

    states = [s_ref[bi] for bi in range(nbatch)]
    for c, bi, slot in work:
        rows = slice(c * L, (c + 1) * L)
        s = states[bi]
        b = b_ref[slot]
        b_end = b[L - 1:L, :]
        q = q_ref[bi, rows, :].astype(F32)
        k = k_ref[bi, rows, :].astype(F32)
        v = v_ref[bi, rows, :]
        q_abs = stack_heads(q * jnp.exp2(b))
        k_end = k * jnp.exp2(b_end - b)
        decay = jnp.exp2(b_end)
        a = a_ref[slot].astype(BF16)
        o_state = _dot(q_abs, s.astype(BF16))

        kt = jnp.concatenate([k_end, jnp.broadcast_to(decay, (8, GQ)),
                              jnp.zeros((LANES - L - 8, GQ), F32)], axis=0).T
        kt_b = kt.astype(BF16)
        v_pad = jnp.concatenate([v, jnp.zeros((LANES - L, GV), BF16)], axis=0)
        decay_col = kt[:, L:L + 1]
        states[bi] = jnp.concatenate(
            [decay_col[h * DK_G:(h + 1) * DK_G] * s[h * DK_G:(h + 1) * DK_G]
             + _dot(kt_b[h * DK_G:(h + 1) * DK_G], v_pad[:, h * DV_G:(h + 1) * DV_G])
             for h in range(H_G)], axis=0)

        for h in range(H_G):
            cols = slice(h * DV_G, (h + 1) * DV_G)
            o = _dot(a[h * L:(h + 1) * L], v[:, cols]) + o_state[h * L:(h + 1) * L]
            gate = gg_ref[bi, rows, cols].astype(F32)
            out_ref[bi, rows, cols] = (_rms(o, g_ref[...]) * _silu(gate)).astype(BF16)

    for bi in range(nbatch):
        s_ref[bi] = states[bi]

    @pl.when(i == pl.num_programs(1) - 1)
    def _():
        for bi in range(nbatch):
            sfin_ref[bi] = states[bi]


def _gla_call(gq, gk, gv, la, gg, gla_g, s0, L, rows_per_step, nbatch):
    nb, seq_len = gq.shape[0], gq.shape[1]
    nchunks = rows_per_step // L
    blk = lambda w: pl.BlockSpec((nbatch, rows_per_step, w), lambda b, i: (b, i, 0))
    state = pl.BlockSpec((nbatch, GQ, DV_G), lambda b, i: (b, 0, 0))
    return pl.pallas_call(
        functools.partial(_gla_kernel, L=L, nchunks=nchunks, nbatch=nbatch),
        grid=(nb // nbatch, seq_len // rows_per_step),
        in_specs=[blk(GQ), blk(GQ), blk(GV), blk(GQ), blk(GV),
                  pl.BlockSpec((1, DV_G), lambda b, i: (0, 0)), state],
        out_specs=[blk(GV), state],
        out_shape=[jax.ShapeDtypeStruct((nb, seq_len, GV), BF16),
                   jax.ShapeDtypeStruct((nb, GQ, DV_G), F32)],
        scratch_shapes=[pltpu.VMEM((nbatch, GQ, DV_G), F32), pltpu.VMEM((nbatch * nchunks, H_G * L, L), F32),
                        pltpu.VMEM((nbatch * nchunks, L, GQ), F32)],
        compiler_params=pltpu.CompilerParams(dimension_semantics=("arbitrary", "arbitrary"),
                                             vmem_limit_bytes=VMEM_LIMIT),
        name="gla",
    )(gq, gk, gv, la, gg, gla_g, s0)


def _lambda(lq1_ref, lk1_ref, lq2_ref, lk2_ref, lam_init):
    s1 = jnp.sum(lq1_ref[...] * lk1_ref[...], axis=-1, keepdims=True)
    s2 = jnp.sum(lq2_ref[...] * lk2_ref[...], axis=-1, keepdims=True)
    return jnp.exp(s1) - jnp.exp(s2) + lam_init


def _diff_prompt_kernel(qt_ref, k_ref, vt_ref, bias_ref, lq1_ref, lk1_ref, lq2_ref, lk2_ref, g_ref, o_ref,
                        rhs_ref, m_ref, acc_ref, s_ref, *, lam_init):
    t = ATT_TILE
    i = pl.program_id(2)
    nq = pl.num_programs(2)

    chains = range(2 * ATT_HEADS)

    def half_queries(iq):
        qrow = lax.broadcasted_iota(jnp.int32, (2 * DH_D, t), 0)
        out = []
        for hh in range(ATT_HEADS):
            qt = qt_ref[0, iq, hh].astype(F32)
            out += [jnp.where((qrow >= c * DH_D) & (qrow < (c + 1) * DH_D), qt, 0.0).astype(BF16) for c in range(2)]
        return out

    def keys(j, hh):
        return k_ref[0, pl.ds(pl.multiple_of(j * t, t), t), hh * 2 * DH_D:(hh + 1) * 2 * DH_D]

    def scores(j, ch):
        return _dot(keys(j, ch // 2), rhs_ref[ch])

    def step(j, parity, diagonal):
        if diagonal:
            q_next = half_queries(jnp.minimum(i + 1, nq - 1))
        for ch in chains:
            if diagonal:
                diagonal_chain(j, parity, ch, q_next[ch])
                continue
            s_ref[1 - parity, ch] = scores(j + 1, ch)
            st = s_ref[parity, ch]
            m_old = m_ref[ch]
            m_new = jnp.maximum(m_old, jnp.max(st, axis=0, keepdims=True))
            alpha = jnp.exp2(m_old - m_new)
            p = jnp.exp2(st - m_new)
            acc_ref[ch] = alpha * acc_ref[ch] + _dot(vt_ref[0, j, ch // 2], p.astype(BF16))
            m_ref[ch] = m_new

    def diagonal_chain(j, parity, ch, q_next):
        u = t // 2
        top = s_ref[parity, ch, :u, :] + bias_ref[:u, :]
        bot = s_ref[parity, ch, u:, u:] + bias_ref[u:, u:]
        blk_max = jnp.max(top, axis=0, keepdims=True)
        blk_max = jnp.concatenate([blk_max[:, :u],
                                   jnp.maximum(blk_max[:, u:], jnp.max(bot, axis=0, keepdims=True))], axis=1)
        m_old = m_ref[ch]
        m_new = jnp.maximum(m_old, blk_max)
        alpha = jnp.exp2(m_old - m_new)
        p_top = jnp.exp2(top - m_new).astype(BF16)
        p_bot = jnp.exp2(bot - m_new[:, u:]).astype(BF16)
        s_ref[0, ch] = _dot(keys(0, ch // 2), q_next)
        vt = vt_ref[0, j, ch // 2]
        pv_bot = _dot(vt[:, u:], p_bot)
        pv = _dot(vt[:, :u], p_top)
        acc_ref[ch] = alpha * acc_ref[ch] + jnp.concatenate([pv[:, :u], pv[:, u:] + pv_bot], axis=1)
        m_ref[ch] = m_new

    def finish():
        lam = _lambda(lq1_ref, lk1_ref, lq2_ref, lk2_ref, lam_init)
        for hh in range(ATT_HEADS):
            a1, a2 = acc_ref[2 * hh], acc_ref[2 * hh + 1]
            ot = a1[:DV_D] * (1.0 / a1[DV_D:DV_D + 1]) - lam * (a2[:DV_D] * (1.0 / a2[DV_D:DV_D + 1]))
            r = lax.rsqrt(jnp.mean(ot * ot, axis=0, keepdims=True) + EPS)
            ot = ot * r * g_ref[...] * (1.0 - lam_init)
            o_ref[0, :, hh * DV_D:(hh + 1) * DV_D] = ot.T.astype(BF16)

    def pair(n, carry):
        step(2 * n, 0, False)
        step(2 * n + 1, 1, False)
        return carry

    q_now = half_queries(i)
    for ch in chains:
        rhs_ref[ch] = q_now[ch]

    @pl.when(i == 0)
    def _():
        for ch in chains:
            s_ref[0, ch] = scores(0, ch)

    m_ref[...] = jnp.full(m_ref.shape, -jnp.inf, F32)
    acc_ref[...] = jnp.zeros(acc_ref.shape, F32)
    lax.fori_loop(0, i // 2, pair, 0)

    @pl.when(i % 2 == 0)
    def _():
        step(i, 0, True)
        finish()

    @pl.when(i % 2 == 1)
    def _():
        step(i - 1, 0, False)
        step(i, 1, True)
        finish()


def _diff_prompt_call(dqt, dkb, dvt, lams, g_col, lam_init):
    nb, seq_len = dkb.shape[0], dkb.shape[1]
    t, nh = ATT_TILE, ATT_HEADS
    chunk_of = jnp.arange(t, dtype=jnp.int32) // CHUNK
    bias = jnp.where(chunk_of[:, None] <= chunk_of[None, :], 0.0, -jnp.inf).astype(F32)
    vec = pl.BlockSpec((1, DH_D), lambda b, h, i: (0, 0))
    return pl.pallas_call(
        functools.partial(_diff_prompt_kernel, lam_init=lam_init),
        grid=(nb, H_D // nh, seq_len // t),
        in_specs=[pl.BlockSpec((1, seq_len // t, nh, 2 * DH_D, t), lambda b, h, i: (b, 0, h, 0, 0)),
                  pl.BlockSpec((1, seq_len, nh * 2 * DH_D), lambda b, h, i: (b, 0, h)),
                  pl.BlockSpec((1, seq_len // t, nh, VT_ROWS, t), lambda b, h, i: (b, 0, h, 0, 0)),
                  pl.BlockSpec((t, t), lambda b, h, i: (0, 0), pipeline_mode=pl.Buffered(1)),
                  vec, vec, vec, vec,
                  pl.BlockSpec((DV_D, 1), lambda b, h, i: (0, 0))],
        out_specs=pl.BlockSpec((1, t, nh * DV_D), lambda b, h, i: (b, i, h)),
        out_shape=jax.ShapeDtypeStruct((nb, seq_len, DV), BF16),
        scratch_shapes=[pltpu.VMEM((2 * nh, 2 * DH_D, t), BF16), pltpu.VMEM((2 * nh, 1, t), F32),
                        pltpu.VMEM((2 * nh, VT_ROWS, t), F32), pltpu.VMEM((2, 2 * nh, t, t), F32)],
        compiler_params=pltpu.CompilerParams(dimension_semantics=("arbitrary", "arbitrary", "arbitrary"),
                                             vmem_limit_bytes=VMEM_LIMIT),
        name="diff_prompt",
    )(dqt, dkb, dvt, bias, *lams, g_col)


def _decode_attention(q_ref, kc_ref, vc_ref, kn_ref, vn_ref, lq1_ref, lk1_ref, lq2_ref, lk2_ref, g_ref,
                      o_ref, *, lam_init):
    T = q_ref.shape[1]
    P = kc_ref.shape[1] // H_D
    lam = _lambda(lq1_ref, lk1_ref, lq2_ref, lk2_ref, lam_init)
    lane = lax.broadcasted_iota(jnp.int32, (T, 2 * DH_D), 1)
    new_col = lax.broadcasted_iota(jnp.int32, (2 * T, LANES), 1)
    pad = jnp.zeros((LANES - T, 2 * DH_D), BF16)
    for h in range(H_D):
        cols = slice(h * 2 * DH_D, (h + 1) * 2 * DH_D)
        q = q_ref[0, :, cols].astype(F32)
        qs = jnp.concatenate([jnp.where(lane < DH_D, q, 0.0), jnp.where(lane >= DH_D, q, 0.0)],
                             axis=0).astype(BF16)
        kc = kc_ref[0, pl.ds(h, P, stride=H_D), :].astype(BF16)
        vc = vc_ref[0, pl.ds(h, P, stride=H_D), :].astype(BF16)
        kn = jnp.concatenate([kn_ref[0, pl.ds(h, T, stride=H_D), :].astype(BF16), pad], axis=0)
        vn = jnp.concatenate([vn_ref[0, pl.ds(h, T, stride=H_D), :].astype(BF16), pad], axis=0)
        sc = _dot_nt(qs, kc)
        sn = jnp.where(new_col < T, _dot_nt(qs, kn), -jnp.inf)
        m = jnp.maximum(jnp.max(sc, axis=-1, keepdims=True), jnp.max(sn, axis=-1, keepdims=True))
        pc = jnp.exp(sc - m)
        pn = jnp.exp(sn - m)
        l = jnp.sum(pc, axis=-1, keepdims=True) + jnp.sum(pn, axis=-1, keepdims=True)
        o2 = (_dot(pc.astype(BF16), vc) + _dot(pn.astype(BF16), vn)) / l
        o = o2[:T] - lam * o2[T:]
        o_ref[0, :, cols] = (_rms(o, g_ref[...]) * (1.0 - lam_init)).astype(BF16)


def kernel(x_prompt, x_sample, state_gla, cache_diff_k, cache_diff_v, w_in, w_gate_a2, b_gate_a, gla_norm_g, lambda_q1, lambda_k1, lambda_q2, lambda_k2, diff_norm_g, w_out, mix_pre_g, mix_post_g, ffn1_pre_g, ffn1_post_g, ffn1_w_gate, ffn1_w_up, ffn1_w_down, ffn2_pre_g, ffn2_post_g, ffn2_w_gate, ffn2_w_up, ffn2_w_down):
    B, S, _ = x_prompt.shape
    Bs, T, _ = x_sample.shape
    depth = w_in.shape[0]
    P = cache_diff_k.shape[2]
    xp = x_prompt.reshape(B * S, D_MODEL)
    xs = x_sample.reshape(Bs * T, D_MODEL)
    cache_k = cache_diff_k.reshape(depth * Bs, P * H_D, LANES)
    cache_v = cache_diff_v.reshape(depth * Bs, P * H_D, LANES)
    outs = [[] for _ in range(6)]
    for l in range(depth):
        lam_init = 0.8 - 0.6 * math.exp(-0.3 * l)
        row = lambda v: v[l].reshape(1, -1)
        o_gr = 2 * GQ + GV
        o_gg = o_gr + GATE_RANK
        o_dq = o_gg + GV
        wi = w_in[l]
        w = {
            "gla": jnp.concatenate([wi[:, :o_gr], wi[:, o_gg:o_dq]], axis=1).astype(BF16),
            "gr": jnp.pad(wi[:, o_gr:o_gg], ((0, 0), (0, LANES - GATE_RANK))).astype(BF16),
            "a2": jnp.pad(w_gate_a2[l], ((0, LANES - GATE_RANK), (0, 0))).astype(BF16),
            "ba": row(b_gate_a),
            "dq": wi[:, o_dq:o_dq + DQ].astype(BF16),
            "dk": wi[:, o_dq + DQ:o_dq + 2 * DQ].astype(BF16),
            "dv": wi[:, o_dq + 2 * DQ:].astype(BF16),
        }
        ffn1 = (row(ffn1_pre_g), row(ffn1_post_g), ffn1_w_gate[l].astype(BF16),
                ffn1_w_up[l].astype(BF16), ffn1_w_down[l].astype(BF16))
        ffn2 = (row(ffn2_pre_g), row(ffn2_post_g), ffn2_w_gate[l].astype(BF16),
                ffn2_w_up[l].astype(BF16), ffn2_w_down[l].astype(BF16))
        wo = w_out[l].astype(BF16)
        wog, wod = wo[:GV], wo[GV:]
        lams = (row(lambda_q1), row(lambda_k1), row(lambda_q2), row(lambda_k2))
        gla_g = row(gla_norm_g)

        x1s = _ffn_call(xs, *ffn1)
        gq_s, gk_s, gv_s, gg_s, la_s, dq_s, dkf_s, _, dvf_s = _inproj_call(x1s, row(mix_pre_g), w, T, False)
        s3 = lambda a: a.reshape(Bs, T, a.shape[-1])
        decode_attn = (s3(dq_s), cache_k, cache_v, dkf_s.reshape(Bs, T * H_D, LANES),
                       dvf_s.reshape(Bs, T * H_D, LANES), l * Bs, lams, row(diff_norm_g), lam_init)

        x1, d_out_s = _ffn_call(xp, *ffn1, decode_attn=decode_attn)
        gq, gk, gv, gg, la, dqt, dkf, dkb, dvf, dvt = _inproj_call(x1, row(mix_pre_g), w, S, True)
        b3 = lambda a: a.reshape(B, S, a.shape[-1])
        g_out, s_p = _gla_call(b3(gq), b3(gk), b3(gv), b3(la), b3(gg), gla_g,
                               jnp.zeros((B, GQ, DV_G), F32), CHUNK, GLA_ROWS, GLA_PROMPT_BATCH)
        d_out = _diff_prompt_call(dqt, b3(dkb), dvt, lams, diff_norm_g[l].reshape(DV_D, 1), lam_init)
        xp = _ffn_call(x1, *ffn2, merge=(g_out.reshape(B * S, GV), d_out.reshape(B * S, DV),
                                         wog, wod, row(mix_post_g)))
        outs[0].append(s_p.reshape(B, H_G, DK_G, DV_G))
        outs[1].append(dkf.reshape(B, S, H_D, 2 * DH_D))
        outs[2].append(dvf.reshape(B, S, H_D, DV_D))

        g_out_s, s_s = _gla_call(s3(gq_s), s3(gk_s), s3(gv_s), s3(la_s), s3(gg_s), gla_g,
                                 state_gla[l].reshape(Bs, GQ, DV_G), T, T, GLA_DECODE_BATCH)
        xs = _ffn_call(x1s, *ffn2, merge=(g_out_s.reshape(Bs * T, GV), d_out_s.reshape(Bs * T, DV),
                                          wog, wod, row(mix_post_g)))
        outs[3].append(s_s.reshape(Bs, H_G, DK_G, DV_G))
        outs[4].append(dkf_s.reshape(Bs, T, H_D, 2 * DH_D))
        outs[5].append(dvf_s.reshape(Bs, T, H_D, DV_D))

    sg_p, k_p, v_p, sg_s, k_s, v_s = (jnp.stack(o) for o in outs)
    return (xp.reshape(B, S, D_MODEL), xs.reshape(Bs, T, D_MODEL), sg_p, k_p, v_p, sg_s, k_s, v_s)
```

```python
import functools
import math

import jax
import jax.numpy as jnp
from jax import lax
from jax.experimental import pallas as pl
from jax.experimental.pallas import tpu as pltpu

F32 = jnp.float32
BF16 = jnp.bfloat16

D_MODEL = 1024
D_FF = 2816
CHUNK = 64
H_G, DK_G, DV_G = 4, 64, 128
GATE_RANK = 16
GATE_TAU = 16.0
H_D, DH_D, DV_D = 4, 64, 128
EPS = 1e-6

GQ = H_G * DK_G
GV = H_G * DV_G
DQ = H_D * 2 * DH_D
DV = H_D * DV_D

LANES = 128
MXU_N = 256
TOKEN_TILE = 1024
FFN_TILE = 1024
FFN_SUB = 512
FF_TILE = MXU_N
ATT_TILE = 512
ATT_HEADS = 4
VT_ROWS = DV_D + 16
GLA_ROWS = 256
GLA_PROMPT_BATCH = 8
GLA_DECODE_BATCH = 8
GLA_EXP_RANGE = 115.0
VMEM_LIMIT = 56 * 1024 * 1024


def _dot(a, b):
    return jnp.dot(a, b, preferred_element_type=F32)


def _dot_nt(a, b):
    return lax.dot_general(a, b, (((1,), (1,)), ((), ())), preferred_element_type=F32)


def _rms(x, g):
    r = lax.rsqrt(jnp.mean(x * x, axis=-1, keepdims=True) + EPS)
    return x * r * g


def _silu(x):
    return x * jax.nn.sigmoid(x)


def _const_spec(shape):
    nd = len(shape)
    return pl.BlockSpec(shape, lambda *_: (0,) * nd, pipeline_mode=pl.Buffered(1))


def _swiglu_half(x, pre_ref, post_ref, wg_ref, wu_ref, wd_ref, h_ref, act_ref):
    h_ref[...] = _rms(x, pre_ref[...]).astype(BF16)
    for j in range(D_FF // FF_TILE):
        cols = slice(j * FF_TILE, (j + 1) * FF_TILE)
        h = h_ref[...]
        gate = _dot(h, wg_ref[:, cols])
        up = _dot(h, wu_ref[:, cols])
        act_ref[:, cols] = (_silu(gate) * up).astype(BF16)
    f = _dot(act_ref[...], wd_ref[...])
    return x + 0.5 * _rms(f, post_ref[...])


def _sub_tiles(ref):
    return [slice(s * FFN_SUB, (s + 1) * FFN_SUB) for s in range(ref.shape[0] // FFN_SUB)]


def _ffn_kernel(x_ref, pre_ref, post_ref, wg_ref, wu_ref, wd_ref, o_ref, h_ref, act_ref):
    for s, rows in enumerate(_sub_tiles(x_ref)):
        o_ref[rows, :] = _swiglu_half(x_ref[rows, :], pre_ref, post_ref, wg_ref, wu_ref, wd_ref,
                                      h_ref.at[s], act_ref.at[s])


def _ffn_decode_attn_kernel(x_ref, pre_ref, post_ref, wg_ref, wu_ref, wd_ref, *rest, lam_init):
    att_refs, (o_ref, att_o_ref, h_ref, act_ref) = rest[:-4], rest[-4:]
    _ffn_kernel(x_ref, pre_ref, post_ref, wg_ref, wu_ref, wd_ref, o_ref, h_ref, act_ref)
    _decode_attention(*att_refs, att_o_ref, lam_init=lam_init)


def _merge_ffn_kernel(x_ref, g_ref, d_ref, wog_ref, wod_ref, mixg_ref,
                      pre_ref, post_ref, wg_ref, wu_ref, wd_ref, o_ref, h_ref, act_ref, x2_ref):
    for s, rows in enumerate(_sub_tiles(x_ref)):
        y = _dot(g_ref[rows, :], wog_ref[...]) + _dot(d_ref[rows, :], wod_ref[...])
        x2_ref[s] = x_ref[rows, :] + _rms(y, mixg_ref[...])
    for s, rows in enumerate(_sub_tiles(x_ref)):
        o_ref[rows, :] = _swiglu_half(x2_ref[s], pre_ref, post_ref, wg_ref, wu_ref, wd_ref,
                                      h_ref.at[s], act_ref.at[s])


def _ffn_call(x, pre_g, post_g, wg, wu, wd, merge=None, decode_attn=None):
    assert (merge is None) != (decode_attn is None)
    m = x.shape[0]
    tm = min(FFN_TILE, m)
    nsub = tm // FFN_SUB
    row = lambda w: pl.BlockSpec((tm, w), lambda i: (i, 0))
    ffn_specs = [_const_spec((1, D_MODEL)), _const_spec((1, D_MODEL)),
                 _const_spec((D_MODEL, D_FF)), _const_spec((D_MODEL, D_FF)), _const_spec((D_FF, D_MODEL))]
    scratch = [pltpu.VMEM((nsub, FFN_SUB, D_MODEL), BF16), pltpu.VMEM((nsub, FFN_SUB, D_FF), BF16)]
    out_specs, out_shape = row(D_MODEL), jax.ShapeDtypeStruct((m, D_MODEL), F32)
    if decode_attn is not None:
        dq, kc, vc, kn, vn, cache_base, lams, g_row, lam_init = decode_attn
        nb, T, _ = dq.shape
        assert merge is None and nb == m // tm, "one decode batch row per FFN grid step"
        new_q = pl.BlockSpec((1, T, DQ), lambda b: (b, 0, 0))
        new_kv = pl.BlockSpec((1, T * H_D, LANES), lambda b: (b, 0, 0))
        cache = pl.BlockSpec((1, kc.shape[1], LANES), lambda b: (cache_base + b, 0, 0))
        vec = pl.BlockSpec((1, DH_D), lambda b: (0, 0))
        kern = functools.partial(_ffn_decode_attn_kernel, lam_init=lam_init)
        ins = (x, pre_g, post_g, wg, wu, wd, dq, kc, vc, kn, vn, *lams, g_row)
        in_specs = ([row(D_MODEL)] + ffn_specs + [new_q, cache, cache, new_kv, new_kv, vec, vec, vec, vec,
                                                 pl.BlockSpec((1, DV_D), lambda b: (0, 0))])
        out_specs = [out_specs, new_q]
        out_shape = [out_shape, jax.ShapeDtypeStruct((nb, T, DV), BF16)]
    else:
        g_out, d_out, wog, wod, mix_g = merge
        kern, ins = _merge_ffn_kernel, (x, g_out, d_out, wog, wod, mix_g, pre_g, post_g, wg, wu, wd)
        in_specs = ([row(D_MODEL), row(GV), row(DV), _const_spec((GV, D_MODEL)),
                     _const_spec((DV, D_MODEL)), _const_spec((1, D_MODEL))] + ffn_specs)
        scratch = scratch + [pltpu.VMEM((nsub, FFN_SUB, D_MODEL), F32)]
    return pl.pallas_call(
        kern,
        grid=(m // tm,),
        in_specs=in_specs,
        out_specs=out_specs,
        out_shape=out_shape,
        scratch_shapes=scratch,
        compiler_params=pltpu.CompilerParams(dimension_semantics=("arbitrary",),
                                             vmem_limit_bytes=VMEM_LIMIT),
        name="merge_ffn" if merge is not None else "ffn_decode_attn",
    )(*ins)


def _stream_prologue(x, pre_ref, x_scr, h_ref, acc_ref):
    x_scr[...] = x
    h_ref[...] = _rms(x, pre_ref[...]).astype(BF16)
    acc_ref[...] = jnp.zeros(acc_ref.shape, F32)


def _stream_chunk(j, post_ref, wg_ref, wu_ref, wd_ref, o_ref, wgb_ref, wub_ref, wdb_ref, x_scr, h_ref, acc_ref):
    wg, wu, wd = (r[...].astype(BF16) for r in (wg_ref, wu_ref, wd_ref))
    wgb_ref[...], wub_ref[...], wdb_ref[...] = wg, wu, wd
    h = h_ref[...]
    act = (_silu(_dot(h, wg)) * _dot(h, wu)).astype(BF16)
    acc_ref[...] += _dot(act, wd)

    @pl.when(j == pl.num_programs(0) - 1)
    def _():
        o_ref[...] = x_scr[...] + 0.5 * _rms(acc_ref[...], post_ref[...])


def _ffn_stream_kernel(x_ref, pre_ref, post_ref, *rest):
    j = pl.program_id(0)

    @pl.when(j == 0)
    def _():
        _stream_prologue(x_ref[...], pre_ref, *rest[-3:])

    _stream_chunk(j, post_ref, *rest)


def _merge_ffn_stream_kernel(x_ref, g_ref, d_ref, wog_ref, wod_ref, mixg_ref, pre_ref, post_ref, *rest):
    j = pl.program_id(0)

    @pl.when(j == 0)
    def _():
        y = _dot(g_ref[...], wog_ref[...]) + _dot(d_ref[...], wod_ref[...])
        _stream_prologue(x_ref[...] + _rms(y, mixg_ref[...]), pre_ref, *rest[-3:])

    _stream_chunk(j, post_ref, *rest)


def _ffn_stream_call(x, pre_g, post_g, wg, wu, wd, merge=None):
    m = x.shape[0]
    whole = lambda w: pl.BlockSpec((m, w), lambda j: (0, 0))
    vec = pl.BlockSpec((1, D_MODEL), lambda j: (0, 0))
    cols = pl.BlockSpec((D_MODEL, FF_TILE), lambda j: (0, j))
    rows = pl.BlockSpec((FF_TILE, D_MODEL), lambda j: (j, 0))
    if merge is None:
        kern, ins, in_specs = _ffn_stream_kernel, (x, pre_g, post_g, wg, wu, wd), [whole(D_MODEL)]
    else:
        g_out, d_out, wog, wod, mix_g = merge
        kern, ins = _merge_ffn_stream_kernel, (x, g_out, d_out, wog, wod, mix_g, pre_g, post_g, wg, wu, wd)
        in_specs = [whole(D_MODEL), whole(GV), whole(DV), pl.BlockSpec((GV, D_MODEL), lambda j: (0, 0)),
                    pl.BlockSpec((DV, D_MODEL), lambda j: (0, 0)), vec]
    out, wgb, wub, wdb = pl.pallas_call(
        kern,
        grid=(D_FF // FF_TILE,),
        in_specs=in_specs + [vec, vec, cols, cols, rows],
        out_specs=[whole(D_MODEL), cols, cols, rows],
        out_shape=[jax.ShapeDtypeStruct((m, D_MODEL), F32), jax.ShapeDtypeStruct((D_MODEL, D_FF), BF16),
                   jax.ShapeDtypeStruct((D_MODEL, D_FF), BF16), jax.ShapeDtypeStruct((D_FF, D_MODEL), BF16)],
        scratch_shapes=[pltpu.VMEM((m, D_MODEL), F32), pltpu.VMEM((m, D_MODEL), BF16),
                        pltpu.VMEM((m, D_MODEL), F32)],
        compiler_params=pltpu.CompilerParams(dimension_semantics=("arbitrary",),
                                             vmem_limit_bytes=VMEM_LIMIT),
        name="merge_ffn_stream" if merge is not None else "ffn_stream",
    )(*ins)
    return out, (wgb, wub, wdb)


def _store_heads(ref, x, row0):
    rows = x.shape[0]
    for h in range(H_D):
        ref[pl.ds(row0 * H_D + h, rows, stride=H_D), :] = x[:, h * LANES:(h + 1) * LANES]


def _regroup_w_in_kernel(w_ref, gla_ref, gr_ref, dq_ref, dk_ref, dv_ref):
    o_gr = 2 * GQ + GV
    o_gg = o_gr + GATE_RANK
    o_dq = o_gg + GV
    w = w_ref[...]
    gla_ref[:, :o_gr] = w[:, :o_gr].astype(BF16)
    gla_ref[:, o_gr:] = w[:, o_gg:o_dq].astype(BF16)
    gr_ref[...] = jnp.concatenate([w[:, o_gr:o_gg], jnp.zeros((w.shape[0], LANES - GATE_RANK), F32)],
                                  axis=1).astype(BF16)
    dq_ref[...] = w[:, o_dq:o_dq + DQ].astype(BF16)
    dk_ref[...] = w[:, o_dq + DQ:o_dq + 2 * DQ].astype(BF16)
    dv_ref[...] = w[:, o_dq + 2 * DQ:].astype(BF16)


def _regroup_w_in_call(w_in):
    rb = 256
    d_in = w_in.shape[1]
    widths = (2 * GQ + 2 * GV, LANES, DQ, DQ, DV)
    return pl.pallas_call(
        _regroup_w_in_kernel,
        grid=(D_MODEL // rb,),
        in_specs=[pl.BlockSpec((rb, d_in), lambda i: (i, 0))],
        out_specs=[pl.BlockSpec((rb, n), lambda i: (i, 0)) for n in widths],
        out_shape=[jax.ShapeDtypeStruct((D_MODEL, n), BF16) for n in widths],
        compiler_params=pltpu.CompilerParams(dimension_semantics=("arbitrary",),
                                             vmem_limit_bytes=VMEM_LIMIT),
        name="regroup_w_in",
    )(w_in)


def _inproj_kernel(x_ref, mixpre_ref, wgla_ref, wgr_ref, wa2_ref, ba_ref, wdq_ref, wdk_ref, wdv_ref,
                   gq_ref, gk_ref, gv_ref, gg_ref, la_ref, dq_ref, dkf_ref, dkb_ref, dvf_ref,
                   *maybe_dvt_ref, transposed):
    t = ATT_TILE
    for s in range(x_ref.shape[0] // t):
        rows = slice(s * t, (s + 1) * t)
        h = _rms(x_ref[rows, :], mixpre_ref[...]).astype(BF16)
        dv = _dot(h, wdv_ref[...])
        _store_heads(dvf_ref, dv, s * t)
        dq = _dot(h, wdq_ref[...])
        if transposed:
            (dvt_ref,) = maybe_dvt_ref
            dvt = dv.T.astype(BF16)
            dqt = (dq * (DH_D ** -0.5 * math.log2(math.e))).T.astype(BF16)
            for hd in range(H_D):
                dq_ref[0, s, hd] = dqt[hd * 2 * DH_D:(hd + 1) * 2 * DH_D]
                dvt_ref[0, s, hd, :DV_D, :] = dvt[hd * DV_D:(hd + 1) * DV_D]
                dvt_ref[0, s, hd, DV_D:, :] = jnp.ones((VT_ROWS - DV_D, t), BF16)
        else:
            dq_ref[rows, :] = (dq * (DH_D ** -0.5)).astype(BF16)
        dk = _dot(h, wdk_ref[...])
        _store_heads(dkf_ref, dk, s * t)
        dkb_ref[rows, :] = dk.astype(BF16)
        gr = _dot(h, wgr_ref[...]).astype(BF16)
        gq_ref[rows, :] = (_dot(h, wgla_ref[:, 0:GQ]) * (DK_G ** -0.5)).astype(BF16)
        a = _dot(gr, wa2_ref[...]) + ba_ref[...]
        gk_ref[rows, :] = _dot(h, wgla_ref[:, GQ:2 * GQ]).astype(BF16)
        la_ref[rows, :] = (jnp.minimum(a, 0.0) - jnp.log1p(jnp.exp(-jnp.abs(a)))) * (math.log2(math.e) / GATE_TAU)
        gv_ref[rows, :] = _dot(h, wgla_ref[:, 2 * GQ:2 * GQ + GV]).astype(BF16)
        gg_ref[rows, :] = _dot(h, wgla_ref[:, 2 * GQ + GV:2 * GQ + 2 * GV]).astype(BF16)


def _inproj_call(x, mixpre_g, w, seq_len, transposed):
    m = x.shape[0]
    tm = min(TOKEN_TILE, m)
    nb = m // seq_len
    row = lambda wd: pl.BlockSpec((tm, wd), lambda i: (i, 0))
    heads = pl.BlockSpec((tm * H_D, LANES), lambda i: (i, 0))
    out_shape = [jax.ShapeDtypeStruct((m, GQ), BF16), jax.ShapeDtypeStruct((m, GQ), BF16),
                 jax.ShapeDtypeStruct((m, GV), BF16), jax.ShapeDtypeStruct((m, GV), BF16),
                 jax.ShapeDtypeStruct((m, GQ), F32)]
    out_specs = [row(GQ), row(GQ), row(GV), row(GV), row(GQ)]
    if transposed:
        t = ATT_TILE
        nq, per_step = seq_len // t, tm // t
        spb = nq // per_step
        out_shape.append(jax.ShapeDtypeStruct((nb, nq, H_D, 2 * DH_D, t), BF16))
        out_specs.append(pl.BlockSpec((1, per_step, H_D, 2 * DH_D, t), lambda i: (i // spb, i % spb, 0, 0, 0)))
    else:
        out_shape.append(jax.ShapeDtypeStruct((m, DQ), BF16))
        out_specs.append(row(DQ))
    out_shape += [jax.ShapeDtypeStruct((m * H_D, LANES), F32), jax.ShapeDtypeStruct((m, DQ), BF16),
                  jax.ShapeDtypeStruct((m * H_D, LANES), F32)]
    out_specs += [heads, row(DQ), heads]
    if transposed:
        out_shape.append(jax.ShapeDtypeStruct((nb, nq, H_D, VT_ROWS, t), BF16))
        out_specs.append(pl.BlockSpec((1, per_step, H_D, VT_ROWS, t), lambda i: (i // spb, i % spb, 0, 0, 0)))
    in_specs = [row(D_MODEL), _const_spec((1, D_MODEL)), _const_spec((D_MODEL, 2 * GQ + 2 * GV)),
                _const_spec((D_MODEL, LANES)), _const_spec((LANES, GQ)), _const_spec((1, GQ)),
                _const_spec((D_MODEL, DQ)), _const_spec((D_MODEL, DQ)), _const_spec((D_MODEL, DV))]
    return pl.pallas_call(
        functools.partial(_inproj_kernel, transposed=transposed),
        grid=(m // tm,),
        in_specs=in_specs,
        out_specs=out_specs,
        out_shape=out_shape,
        compiler_params=pltpu.CompilerParams(dimension_semantics=("arbitrary",),
                                             vmem_limit_bytes=VMEM_LIMIT),
        name="inproj_t" if transposed else "inproj",
    )(x, mixpre_g, w["gla"], w["gr"], w["a2"], w["ba"], w["dq"], w["dk"], w["dv"])


def _cumsum_rows(tril, x):
    hi = x.astype(BF16)
    r = x - hi.astype(F32)
    mid = r.astype(BF16)
    lo = (r - mid.astype(F32)).astype(BF16)
    return _dot(tril, hi) + _dot(tril, mid) + _dot(tril, lo)


def _gla_exact_scores(q_ref, k_ref, b_ref, a_ref, bi, c, slot, L):
    rows = slice(c * L, (c + 1) * L)
    q = q_ref[bi, rows, :].astype(F32)
    k = k_ref[bi, rows, :].astype(F32)
    b = b_ref[slot]
    row_id = lax.broadcasted_iota(jnp.int32, (L, GQ), 0)
    col_id = lax.broadcasted_iota(jnp.int32, (16, L), 1)
    head_of_lane = lax.broadcasted_iota(jnp.int32, (16, GQ), 1) >> int(math.log2(DK_G))
    head_ind = (head_of_lane == lax.broadcasted_iota(jnp.int32, (16, GQ), 0)).astype(BF16)

    def one_row(t, carry):
        sel = row_id == t
        qt = jnp.sum(jnp.where(sel, q, 0.0), axis=0, keepdims=True)
        bt = jnp.sum(jnp.where(sel, b, 0.0), axis=0, keepdims=True)
        prod = qt * k * jnp.exp2(jnp.minimum(bt - b, 0.0))
        hi = prod.astype(BF16)
        lo = (prod - hi.astype(F32)).astype(BF16)
        per_head = _dot_nt(head_ind, hi) + _dot_nt(head_ind, lo)
        per_head = jnp.where(col_id <= t, per_head, 0.0)
        for h in range(H_G):
            a_ref[slot, pl.ds(h * L + t, 1), :] = per_head[h:h + 1, :]
        return carry

    lax.fori_loop(0, L, one_row, 0)


def _gla_kernel(q_ref, k_ref, v_ref, la_ref, gg_ref, g_ref, s0_ref, out_ref, sfin_ref,
                s_ref, a_ref, b_ref, *, L, nchunks, nbatch):
    i = pl.program_id(1)

    @pl.when(i == 0)
    def _():
        s_ref[...] = s0_ref[...]

    causal = (lax.broadcasted_iota(jnp.int32, (H_G * L, L), 1)
              <= (lax.broadcasted_iota(jnp.int32, (H_G * L, L), 0) & (L - 1)))
    lane_head = lax.broadcasted_iota(jnp.int32, (L, GQ), 1) >> int(math.log2(DK_G))
    tr = lax.broadcasted_iota(jnp.int32, (L, L), 0)
    tc = lax.broadcasted_iota(jnp.int32, (L, L), 1)
    tril = (tc <= tr).astype(BF16)

    def stack_heads(x):
        return jnp.concatenate([jnp.where(lane_head == h, x, 0.0) for h in range(H_G)], axis=0).astype(BF16)

    work = [(c, bi, bi * nchunks + c) for c in range(nchunks) for bi in range(nbatch)]
    for c, bi, slot in work:
        rows = slice(c * L, (c + 1) * L)
        b = _cumsum_rows(tril, la_ref[bi, rows, :])
        b_ref[slot] = b
        b_mid = b[L // 2 - 1:L // 2, :]
        q_mid = stack_heads(q_ref[bi, rows, :].astype(F32) * jnp.exp2(b - b_mid))
        k_mid = (k_ref[bi, rows, :].astype(F32) * jnp.exp2(b_mid - b)).astype(BF16)
        a_ref[slot] = jnp.where(causal, _dot_nt(q_mid, k_mid), 0.0)

    @pl.when(jnp.min(la_ref[...]) < -(2.0 * GLA_EXP_RANGE / L))
    def _():
        for c, bi, slot in work:
            _gla_exact_scores(q_ref, k_ref, b_ref, a_ref, bi, c, slot, L)

    states = [s_ref[bi] for bi in range(nbatch)]
    for c, bi, slot in work:
        rows = slice(c * L, (c + 1) * L)
        s = states[bi]
        b = b_ref[slot]
        b_end = b[L - 1:L, :]
        q = q_ref[bi, rows, :].astype(F32)
        k = k_ref[bi, rows, :].astype(F32)
        v = v_ref[bi, rows, :]
        q_abs = stack_heads(q * jnp.exp2(b))
        k_end = k * jnp.exp2(b_end - b)
        decay = jnp.exp2(b_end)
        a = a_ref[slot].astype(BF16)
        o_state = _dot(q_abs, s.astype(BF16))

        kt = jnp.concatenate([k_end, jnp.broadcast_to(decay, (8, GQ)),
                              jnp.zeros((LANES - L - 8, GQ), F32)], axis=0).T
        kt_b = kt.astype(BF16)
        v_pad = jnp.concatenate([v, jnp.zeros((LANES - L, GV), BF16)], axis=0)
        decay_col = kt[:, L:L + 1]
        states[bi] = jnp.concatenate(
            [decay_col[h * DK_G:(h + 1) * DK_G] * s[h * DK_G:(h + 1) * DK_G]
             + _dot(kt_b[h * DK_G:(h + 1) * DK_G], v_pad[:, h * DV_G:(h + 1) * DV_G])
             for h in range(H_G)], axis=0)

        for h in range(H_G):
            cols = slice(h * DV_G, (h + 1) * DV_G)
            o = _dot(a[h * L:(h + 1) * L], v[:, cols]) + o_state[h * L:(h + 1) * L]
            gate = gg_ref[bi, rows, cols].astype(F32)
            out_ref[bi, rows, cols] = (_rms(o, g_ref[...]) * _silu(gate)).astype(BF16)

    for bi in range(nbatch):
        s_ref[bi] = states[bi]

    @pl.when(i == pl.num_programs(1) - 1)
    def _():
        for bi in range(nbatch):
            sfin_ref[bi] = states[bi]


def _gla_call(gq, gk, gv, la, gg, gla_g, s0, L, rows_per_step, nbatch):
    nb, seq_len = gq.shape[0], gq.shape[1]
    nchunks = rows_per_step // L
    blk = lambda w: pl.BlockSpec((nbatch, rows_per_step, w), lambda b, i: (b, i, 0))
    state = pl.BlockSpec((nbatch, GQ, DV_G), lambda b, i: (b, 0, 0))
    return pl.pallas_call(
        functools.partial(_gla_kernel, L=L, nchunks=nchunks, nbatch=nbatch),
        grid=(nb // nbatch, seq_len // rows_per_step),
        in_specs=[blk(GQ), blk(GQ), blk(GV), blk(GQ), blk(GV),
                  pl.BlockSpec((1, DV_G), lambda b, i: (0, 0)), state],
        out_specs=[blk(GV), state],
        out_shape=[jax.ShapeDtypeStruct((nb, seq_len, GV), BF16),
                   jax.ShapeDtypeStruct((nb, GQ, DV_G), F32)],
        scratch_shapes=[pltpu.VMEM((nbatch, GQ, DV_G), F32), pltpu.VMEM((nbatch * nchunks, H_G * L, L), F32),
                        pltpu.VMEM((nbatch * nchunks, L, GQ), F32)],
        compiler_params=pltpu.CompilerParams(dimension_semantics=("arbitrary", "arbitrary"),
                                             vmem_limit_bytes=VMEM_LIMIT),
        name="gla",
    )(gq, gk, gv, la, gg, gla_g, s0)


def _lambda(lq1_ref, lk1_ref, lq2_ref, lk2_ref, lam_init):
    s1 = jnp.sum(lq1_ref[...] * lk1_ref[...], axis=-1, keepdims=True)
    s2 = jnp.sum(lq2_ref[...] * lk2_ref[...], axis=-1, keepdims=True)
    return jnp.exp(s1) - jnp.exp(s2) + lam_init


def _diff_prompt_kernel(qt_ref, k_ref, vt_ref, bias_ref, lq1_ref, lk1_ref, lq2_ref, lk2_ref, g_ref, o_ref,
                        rhs_ref, m_ref, acc_ref, s_ref, *, lam_init):
    t = ATT_TILE
    i = pl.program_id(2)
    nq = pl.num_programs(2)

    chains = range(2 * ATT_HEADS)

    def half_queries(iq):
        qrow = lax.broadcasted_iota(jnp.int32, (2 * DH_D, t), 0)
        out = []
        for hh in range(ATT_HEADS):
            qt = qt_ref[0, iq, hh].astype(F32)
            out += [jnp.where((qrow >= c * DH_D) & (qrow < (c + 1) * DH_D), qt, 0.0).astype(BF16) for c in range(2)]
        return out

    def keys(j, hh):
        return k_ref[0, pl.ds(pl.multiple_of(j * t, t), t), hh * 2 * DH_D:(hh + 1) * 2 * DH_D]

    def scores(j, ch):
        return _dot(keys(j, ch // 2), rhs_ref[ch])

    def step(j, parity, diagonal):
        if diagonal:
            q_next = half_queries(jnp.minimum(i + 1, nq - 1))
        for ch in chains:
            if diagonal:
                diagonal_chain(j, parity, ch, q_next[ch])
                continue
            s_ref[1 - parity, ch] = scores(j + 1, ch)
            st = s_ref[parity, ch]
            m_old = m_ref[ch]
            m_new = jnp.maximum(m_old, jnp.max(st, axis=0, keepdims=True))
            alpha = jnp.exp2(m_old - m_new)
            p = jnp.exp2(st - m_new)
            acc_ref[ch] = alpha * acc_ref[ch] + _dot(vt_ref[0, j, ch // 2], p.astype(BF16))
            m_ref[ch] = m_new

    def diagonal_chain(j, parity, ch, q_next):
        u = t // 2
        top = s_ref[parity, ch, :u, :] + bias_ref[:u, :]
        bot = s_ref[parity, ch, u:, u:] + bias_ref[u:, u:]
        blk_max = jnp.max(top, axis=0, keepdims=True)
        blk_max = jnp.concatenate([blk_max[:, :u],
                                   jnp.maximum(blk_max[:, u:], jnp.max(bot, axis=0, keepdims=True))], axis=1)
        m_old = m_ref[ch]
        m_new = jnp.maximum(m_old, blk_max)
        alpha = jnp.exp2(m_old - m_new)
        p_top = jnp.exp2(top - m_new).astype(BF16)
        p_bot = jnp.exp2(bot - m_new[:, u:]).astype(BF16)
        s_ref[0, ch] = _dot(keys(0, ch // 2), q_next)
        vt = vt_ref[0, j, ch // 2]
        pv_bot = _dot(vt[:, u:], p_bot)
        pv = _dot(vt[:, :u], p_top)
        acc_ref[ch] = alpha * acc_ref[ch] + jnp.concatenate([pv[:, :u], pv[:, u:] + pv_bot], axis=1)
        m_ref[ch] = m_new

    def finish():
        lam = _lambda(lq1_ref, lk1_ref, lq2_ref, lk2_ref, lam_init)
        for hh in range(ATT_HEADS):
            a1, a2 = acc_ref[2 * hh], acc_ref[2 * hh + 1]
            ot = a1[:DV_D] * (1.0 / a1[DV_D:DV_D + 1]) - lam * (a2[:DV_D] * (1.0 / a2[DV_D:DV_D + 1]))
            r = lax.rsqrt(jnp.mean(ot * ot, axis=0, keepdims=True) + EPS)
            ot = ot * r * g_ref[...] * (1.0 - lam_init)
            o_ref[0, :, hh * DV_D:(hh + 1) * DV_D] = ot.T.astype(BF16)

    def pair(n, carry):
        step(2 * n, 0, False)
        step(2 * n + 1, 1, False)
        return carry

    q_now = half_queries(i)
    for ch in chains:
        rhs_ref[ch] = q_now[ch]

    @pl.when(i == 0)
    def _():
        for ch in chains:
            s_ref[0, ch] = scores(0, ch)

    m_ref[...] = jnp.full(m_ref.shape, -jnp.inf, F32)
    acc_ref[...] = jnp.zeros(acc_ref.shape, F32)
    lax.fori_loop(0, i // 2, pair, 0)

    @pl.when(i % 2 == 0)
    def _():
        step(i, 0, True)
        finish()

    @pl.when(i % 2 == 1)
    def _():
        step(i - 1, 0, False)
        step(i, 1, True)
        finish()


def _diff_prompt_call(dqt, dkb, dvt, lams, g_col, lam_init):
    nb, seq_len = dkb.shape[0], dkb.shape[1]
    t, nh = ATT_TILE, ATT_HEADS
    chunk_of = jnp.arange(t, dtype=jnp.int32) // CHUNK
    bias = jnp.where(chunk_of[:, None] <= chunk_of[None, :], 0.0, -jnp.inf).astype(F32)
    vec = pl.BlockSpec((1, DH_D), lambda b, h, i: (0, 0))
    return pl.pallas_call(
        functools.partial(_diff_prompt_kernel, lam_init=lam_init),
        grid=(nb, H_D // nh, seq_len // t),
        in_specs=[pl.BlockSpec((1, seq_len // t, nh, 2 * DH_D, t), lambda b, h, i: (b, 0, h, 0, 0)),
                  pl.BlockSpec((1, seq_len, nh * 2 * DH_D), lambda b, h, i: (b, 0, h)),
                  pl.BlockSpec((1, seq_len // t, nh, VT_ROWS, t), lambda b, h, i: (b, 0, h, 0, 0)),
                  pl.BlockSpec((t, t), lambda b, h, i: (0, 0), pipeline_mode=pl.Buffered(1)),
                  vec, vec, vec, vec,
                  pl.BlockSpec((DV_D, 1), lambda b, h, i: (0, 0))],
        out_specs=pl.BlockSpec((1, t, nh * DV_D), lambda b, h, i: (b, i, h)),
        out_shape=jax.ShapeDtypeStruct((nb, seq_len, DV), BF16),
        scratch_shapes=[pltpu.VMEM((2 * nh, 2 * DH_D, t), BF16), pltpu.VMEM((2 * nh, 1, t), F32),
                        pltpu.VMEM((2 * nh, VT_ROWS, t), F32), pltpu.VMEM((2, 2 * nh, t, t), F32)],
        compiler_params=pltpu.CompilerParams(dimension_semantics=("arbitrary", "arbitrary", "arbitrary"),
                                             vmem_limit_bytes=VMEM_LIMIT),
        name="diff_prompt",
    )(dqt, dkb, dvt, bias, *lams, g_col)


def _decode_attention(q_ref, kc_ref, vc_ref, kn_ref, vn_ref, lq1_ref, lk1_ref, lq2_ref, lk2_ref, g_ref,
                      o_ref, *, lam_init):
    T = q_ref.shape[1]
    P = kc_ref.shape[1] // H_D
    lam = _lambda(lq1_ref, lk1_ref, lq2_ref, lk2_ref, lam_init)
    lane = lax.broadcasted_iota(jnp.int32, (T, 2 * DH_D), 1)
    new_col = lax.broadcasted_iota(jnp.int32, (2 * T, LANES), 1)
    pad = jnp.zeros((LANES - T, 2 * DH_D), BF16)
    for h in range(H_D):
        cols = slice(h * 2 * DH_D, (h + 1) * 2 * DH_D)
        q = q_ref[0, :, cols].astype(F32)
        qs = jnp.concatenate([jnp.where(lane < DH_D, q, 0.0), jnp.where(lane >= DH_D, q, 0.0)],
                             axis=0).astype(BF16)
        kc = kc_ref[0, pl.ds(h, P, stride=H_D), :].astype(BF16)
        vc = vc_ref[0, pl.ds(h, P, stride=H_D), :].astype(BF16)
        kn = jnp.concatenate([kn_ref[0, pl.ds(h, T, stride=H_D), :].astype(BF16), pad], axis=0)
        vn = jnp.concatenate([vn_ref[0, pl.ds(h, T, stride=H_D), :].astype(BF16), pad], axis=0)
        sc = _dot_nt(qs, kc)
        sn = jnp.where(new_col < T, _dot_nt(qs, kn), -jnp.inf)
        m = jnp.maximum(jnp.max(sc, axis=-1, keepdims=True), jnp.max(sn, axis=-1, keepdims=True))
        pc = jnp.exp(sc - m)
        pn = jnp.exp(sn - m)
        l = jnp.sum(pc, axis=-1, keepdims=True) + jnp.sum(pn, axis=-1, keepdims=True)
        o2 = (_dot(pc.astype(BF16), vc) + _dot(pn.astype(BF16), vn)) / l
        o = o2[:T] - lam * o2[T:]
        o_ref[0, :, cols] = (_rms(o, g_ref[...]) * (1.0 - lam_init)).astype(BF16)


def kernel(x_prompt, x_sample, state_gla, cache_diff_k, cache_diff_v, w_in, w_gate_a2, b_gate_a, gla_norm_g, lambda_q1, lambda_k1, lambda_q2, lambda_k2, diff_norm_g, w_out, mix_pre_g, mix_post_g, ffn1_pre_g, ffn1_post_g, ffn1_w_gate, ffn1_w_up, ffn1_w_down, ffn2_pre_g, ffn2_post_g, ffn2_w_gate, ffn2_w_up, ffn2_w_down):
    B, S, _ = x_prompt.shape
    Bs, T, _ = x_sample.shape
    depth = w_in.shape[0]
    P = cache_diff_k.shape[2]
    xp = x_prompt.reshape(B * S, D_MODEL)
    xs = x_sample.reshape(Bs * T, D_MODEL)
    cache_k = cache_diff_k.reshape(depth * Bs, P * H_D, LANES)
    cache_v = cache_diff_v.reshape(depth * Bs, P * H_D, LANES)
    outs = [[] for _ in range(6)]
    for l in range(depth):
        lam_init = 0.8 - 0.6 * math.exp(-0.3 * l)
        row = lambda v: v[l].reshape(1, -1)
        w = dict(zip(("gla", "gr", "dq", "dk", "dv"), _regroup_w_in_call(w_in[l])))
        w["a2"] = jnp.pad(w_gate_a2[l], ((0, LANES - GATE_RANK), (0, 0))).astype(BF16)
        w["ba"] = row(b_gate_a)
        ffn1_g = (row(ffn1_pre_g), row(ffn1_post_g))
        ffn2_g = (row(ffn2_pre_g), row(ffn2_post_g))
        wo = w_out[l].astype(BF16)
        wog, wod = wo[:GV], wo[GV:]
        lams = (row(lambda_q1), row(lambda_k1), row(lambda_q2), row(lambda_k2))
        gla_g = row(gla_norm_g)

        x1s, ffn1_w = _ffn_stream_call(xs, *ffn1_g, ffn1_w_gate[l], ffn1_w_up[l], ffn1_w_down[l])
        gq_s, gk_s, gv_s, gg_s, la_s, dq_s, dkf_s, _, dvf_s = _inproj_call(x1s, row(mix_pre_g), w, T, False)
        s3 = lambda a: a.reshape(Bs, T, a.shape[-1])
        decode_attn = (s3(dq_s), cache_k, cache_v, dkf_s.reshape(Bs, T * H_D, LANES),
                       dvf_s.reshape(Bs, T * H_D, LANES), l * Bs, lams, row(diff_norm_g), lam_init)

        x1, d_out_s = _ffn_call(xp, *ffn1_g, *ffn1_w, decode_attn=decode_attn)
        gq, gk, gv, gg, la, dqt, dkf, dkb, dvf, dvt = _inproj_call(x1, row(mix_pre_g), w, S, True)
        b3 = lambda a: a.reshape(B, S, a.shape[-1])
        g_out, s_p = _gla_call(b3(gq), b3(gk), b3(gv), b3(la), b3(gg), gla_g,
                               jnp.zeros((B, GQ, DV_G), F32), CHUNK, GLA_ROWS, GLA_PROMPT_BATCH)
        d_out = _diff_prompt_call(dqt, b3(dkb), dvt, lams, diff_norm_g[l].reshape(DV_D, 1), lam_init)

        g_out_s, s_s = _gla_call(s3(gq_s), s3(gk_s), s3(gv_s), s3(la_s), s3(gg_s), gla_g,
                                 state_gla[l].reshape(Bs, GQ, DV_G), T, T, GLA_DECODE_BATCH)
        xs, ffn2_w = _ffn_stream_call(x1s, *ffn2_g, ffn2_w_gate[l], ffn2_w_up[l], ffn2_w_down[l],
                                      merge=(g_out_s.reshape(Bs * T, GV), d_out_s.reshape(Bs * T, DV),
                                             wog, wod, row(mix_post_g)))

        xp = _ffn_call(x1, *ffn2_g, *ffn2_w, merge=(g_out.reshape(B * S, GV), d_out.reshape(B * S, DV),
                                                    wog, wod, row(mix_post_g)))
        outs[0].append(s_p.reshape(B, H_G, DK_G, DV_G))
        outs[1].append(dkf.reshape(B, S, H_D, 2 * DH_D))
        outs[2].append(dvf.reshape(B, S, H_D, DV_D))
        outs[3].append(s_s.reshape(Bs, H_G, DK_G, DV_G))
        outs[4].append(dkf_s.reshape(Bs, T, H_D, 2 * DH_D))
        outs[5].append(dvf_s.reshape(Bs, T, H_D, DV_D))

    sg_p, k_p, v_p, sg_s, k_s, v_s = (jnp.stack(o) for o in outs)
    return (xp.reshape(B, S, D_MODEL), xs.reshape(Bs, T, D_MODEL), sg_p, k_p, v_p, sg_s, k_s, v_s)
```

```python
import functools
import math

import jax
import jax.numpy as jnp
from jax import lax
from jax.experimental import pallas as pl
from jax.experimental.pallas import tpu as pltpu

F32 = jnp.float32
BF16 = jnp.bfloat16

D_MODEL = 1024
D_FF = 2816
CHUNK = 64
H_G, DK_G, DV_G = 4, 64, 128
GATE_RANK = 16
GATE_TAU = 16.0
H_D, DH_D, DV_D = 4, 64, 128
EPS = 1e-6

GQ = H_G * DK_G
GV = H_G * DV_G
DQ = H_D * 2 * DH_D
DV = H_D * DV_D

LANES = 128
MXU_N = 256
TOKEN_TILE = 1024
FFN_TILE = 1024
FFN_SUB = 512
FF_TILE = MXU_N
ATT_TILE = 512
ATT_HEADS = 4
VT_ROWS = DV_D + 16
GLA_ROWS = 256
GLA_PROMPT_BATCH = 8
GLA_DECODE_BATCH = 8
GLA_EXP_RANGE = 115.0
VMEM_LIMIT = 56 * 1024 * 1024


def _dot(a, b):
    return jnp.dot(a, b, preferred_element_type=F32)


def _dot_nt(a, b):
    return lax.dot_general(a, b, (((1,), (1,)), ((), ())), preferred_element_type=F32)


def _rms(x, g):
    r = lax.rsqrt(jnp.mean(x * x, axis=-1, keepdims=True) + EPS)
    return x * r * g


def _silu(x):
    return x * jax.nn.sigmoid(x)


def _const_spec(shape):
    nd = len(shape)
    return pl.BlockSpec(shape, lambda *_: (0,) * nd, pipeline_mode=pl.Buffered(1))


def _swiglu_half(x, pre_ref, post_ref, wg_ref, wu_ref, wd_ref, h_ref, act_ref, between=()):
    assert len(between) <= D_FF // FF_TILE
    h_ref[...] = _rms(x, pre_ref[...]).astype(BF16)
    for j in range(D_FF // FF_TILE):
        cols = slice(j * FF_TILE, (j + 1) * FF_TILE)
        h = h_ref[...]
        gate = _dot(h, wg_ref[:, cols])
        up = _dot(h, wu_ref[:, cols])
        act_ref[:, cols] = (_silu(gate) * up).astype(BF16)
        if j < len(between):
            between[j]()
    f = _dot(act_ref[...], wd_ref[...])
    return x + 0.5 * _rms(f, post_ref[...])


def _sub_tiles(ref):
    return [slice(s * FFN_SUB, (s + 1) * FFN_SUB) for s in range(ref.shape[0] // FFN_SUB)]


def _ffn_kernel(x_ref, pre_ref, post_ref, wg_ref, wu_ref, wd_ref, o_ref, h_ref, act_ref):
    for s, rows in enumerate(_sub_tiles(x_ref)):
        o_ref[rows, :] = _swiglu_half(x_ref[rows, :], pre_ref, post_ref, wg_ref, wu_ref, wd_ref,
                                      h_ref.at[s], act_ref.at[s])


def _ffn_decode_attn_kernel(x_ref, pre_ref, post_ref, wg_ref, wu_ref, wd_ref, *rest, lam_init):
    att_refs, (o_ref, att_o_ref, h_ref, act_ref) = rest[:-4], rest[-4:]
    stages = _decode_attention(*att_refs, att_o_ref, lam_init=lam_init)
    for s, rows in enumerate(_sub_tiles(x_ref)):
        o_ref[rows, :] = _swiglu_half(x_ref[rows, :], pre_ref, post_ref, wg_ref, wu_ref, wd_ref,
                                      h_ref.at[s], act_ref.at[s], between=stages if s == 0 else ())


def _merge_ffn_kernel(x_ref, g_ref, d_ref, wog_ref, wod_ref, mixg_ref,
                      pre_ref, post_ref, wg_ref, wu_ref, wd_ref, o_ref, h_ref, act_ref, x2_ref):
    for s, rows in enumerate(_sub_tiles(x_ref)):
        y = _dot(g_ref[rows, :], wog_ref[...]) + _dot(d_ref[rows, :], wod_ref[...])
        x2_ref[s] = x_ref[rows, :] + _rms(y, mixg_ref[...])
    for s, rows in enumerate(_sub_tiles(x_ref)):
        o_ref[rows, :] = _swiglu_half(x2_ref[s], pre_ref, post_ref, wg_ref, wu_ref, wd_ref,
                                      h_ref.at[s], act_ref.at[s])


def _ffn_call(x, pre_g, post_g, wg, wu, wd, merge=None, decode_attn=None):
    assert (merge is None) != (decode_attn is None)
    m = x.shape[0]
    tm = min(FFN_TILE, m)
    nsub = tm // FFN_SUB
    row = lambda w: pl.BlockSpec((tm, w), lambda i: (i, 0))
    ffn_specs = [_const_spec((1, D_MODEL)), _const_spec((1, D_MODEL)),
                 _const_spec((D_MODEL, D_FF)), _const_spec((D_MODEL, D_FF)), _const_spec((D_FF, D_MODEL))]
    scratch = [pltpu.VMEM((nsub, FFN_SUB, D_MODEL), BF16), pltpu.VMEM((nsub, FFN_SUB, D_FF), BF16)]
    out_specs, out_shape = row(D_MODEL), jax.ShapeDtypeStruct((m, D_MODEL), F32)
    if decode_attn is not None:
        dq, kc, vc, kn, vn, cache_base, lams, g_row, lam_init = decode_attn
        nb, T, _ = dq.shape
        assert merge is None and nb == m // tm, "one decode batch row per FFN grid step"
        new_q = pl.BlockSpec((1, T, DQ), lambda b: (b, 0, 0))
        new_kv = pl.BlockSpec((1, T * H_D, LANES), lambda b: (b, 0, 0))
        cache = pl.BlockSpec((1, kc.shape[1], LANES), lambda b: (cache_base + b, 0, 0))
        vec = pl.BlockSpec((1, DH_D), lambda b: (0, 0))
        kern = functools.partial(_ffn_decode_attn_kernel, lam_init=lam_init)
        ins = (x, pre_g, post_g, wg, wu, wd, dq, kc, vc, kn, vn, *lams, g_row)
        in_specs = ([row(D_MODEL)] + ffn_specs + [new_q, cache, cache, new_kv, new_kv, vec, vec, vec, vec,
                                                 pl.BlockSpec((1, DV_D), lambda b: (0, 0))])
        out_specs = [out_specs, new_q]
        out_shape = [out_shape, jax.ShapeDtypeStruct((nb, T, DV), BF16)]
    else:
        g_out, d_out, wog, wod, mix_g = merge
        kern, ins = _merge_ffn_kernel, (x, g_out, d_out, wog, wod, mix_g, pre_g, post_g, wg, wu, wd)
        in_specs = ([row(D_MODEL), row(GV), row(DV), _const_spec((GV, D_MODEL)),
                     _const_spec((DV, D_MODEL)), _const_spec((1, D_MODEL))] + ffn_specs)
        scratch = scratch + [pltpu.VMEM((nsub, FFN_SUB, D_MODEL), F32)]
    return pl.pallas_call(
        kern,
        grid=(m // tm,),
        in_specs=in_specs,
        out_specs=out_specs,
        out_shape=out_shape,
        scratch_shapes=scratch,
        compiler_params=pltpu.CompilerParams(dimension_semantics=("arbitrary",),
                                             vmem_limit_bytes=VMEM_LIMIT),
        name="merge_ffn" if merge is not None else "ffn_decode_attn",
    )(*ins)


def _stream_prologue(x, pre_ref, x_scr, h_ref, acc_ref):
    x_scr[...] = x
    h_ref[...] = _rms(x, pre_ref[...]).astype(BF16)
    acc_ref[...] = jnp.zeros(acc_ref.shape, F32)


def _stream_chunk(j, post_ref, wg_ref, wu_ref, wd_ref, o_ref, wgb_ref, wub_ref, wdb_ref, x_scr, h_ref, acc_ref):
    wg, wu, wd = (r[...].astype(BF16) for r in (wg_ref, wu_ref, wd_ref))
    wgb_ref[...], wub_ref[...], wdb_ref[...] = wg, wu, wd
    h = h_ref[...]
    act = (_silu(_dot(h, wg)) * _dot(h, wu)).astype(BF16)
    acc_ref[...] += _dot(act, wd)

    @pl.when(j == pl.num_programs(0) - 1)
    def _():
        o_ref[...] = x_scr[...] + 0.5 * _rms(acc_ref[...], post_ref[...])


def _ffn_stream_kernel(x_ref, pre_ref, post_ref, *rest):
    j = pl.program_id(0)

    @pl.when(j == 0)
    def _():
        _stream_prologue(x_ref[...], pre_ref, *rest[-3:])

    _stream_chunk(j, post_ref, *rest)


def _merge_ffn_stream_kernel(x_ref, g_ref, d_ref, wog_ref, wod_ref, mixg_ref, pre_ref, post_ref, *rest):
    j = pl.program_id(0)

    @pl.when(j == 0)
    def _():
        y = _dot(g_ref[...], wog_ref[...]) + _dot(d_ref[...], wod_ref[...])
        _stream_prologue(x_ref[...] + _rms(y, mixg_ref[...]), pre_ref, *rest[-3:])

    _stream_chunk(j, post_ref, *rest)


def _ffn_stream_call(x, pre_g, post_g, wg, wu, wd, merge=None):
    m = x.shape[0]
    whole = lambda w: pl.BlockSpec((m, w), lambda j: (0, 0))
    vec = pl.BlockSpec((1, D_MODEL), lambda j: (0, 0))
    cols = pl.BlockSpec((D_MODEL, FF_TILE), lambda j: (0, j))
    rows = pl.BlockSpec((FF_TILE, D_MODEL), lambda j: (j, 0))
    if merge is None:
        kern, ins, in_specs = _ffn_stream_kernel, (x, pre_g, post_g, wg, wu, wd), [whole(D_MODEL)]
    else:
        g_out, d_out, wog, wod, mix_g = merge
        kern, ins = _merge_ffn_stream_kernel, (x, g_out, d_out, wog, wod, mix_g, pre_g, post_g, wg, wu, wd)
        in_specs = [whole(D_MODEL), whole(GV), whole(DV), pl.BlockSpec((GV, D_MODEL), lambda j: (0, 0)),
                    pl.BlockSpec((DV, D_MODEL), lambda j: (0, 0)), vec]
    out, wgb, wub, wdb = pl.pallas_call(
        kern,
        grid=(D_FF // FF_TILE,),
        in_specs=in_specs + [vec, vec, cols, cols, rows],
        out_specs=[whole(D_MODEL), cols, cols, rows],
        out_shape=[jax.ShapeDtypeStruct((m, D_MODEL), F32), jax.ShapeDtypeStruct((D_MODEL, D_FF), BF16),
                   jax.ShapeDtypeStruct((D_MODEL, D_FF), BF16), jax.ShapeDtypeStruct((D_FF, D_MODEL), BF16)],
        scratch_shapes=[pltpu.VMEM((m, D_MODEL), F32), pltpu.VMEM((m, D_MODEL), BF16),
                        pltpu.VMEM((m, D_MODEL), F32)],
        compiler_params=pltpu.CompilerParams(dimension_semantics=("arbitrary",),
                                             vmem_limit_bytes=VMEM_LIMIT),
        name="merge_ffn_stream" if merge is not None else "ffn_stream",
    )(*ins)
    return out, (wgb, wub, wdb)


def _store_heads(ref, x, row0):
    rows = x.shape[0]
    for h in range(H_D):
        ref[pl.ds(row0 * H_D + h, rows, stride=H_D), :] = x[:, h * LANES:(h + 1) * LANES]


def _regroup_w_in_kernel(w_ref, gla_ref, gr_ref, dq_ref, dk_ref, dv_ref):
    o_gr = 2 * GQ + GV
    o_gg = o_gr + GATE_RANK
    o_dq = o_gg + GV
    w = w_ref[...]
    gla_ref[:, :o_gr] = w[:, :o_gr].astype(BF16)
    gla_ref[:, o_gr:] = w[:, o_gg:o_dq].astype(BF16)
    gr_ref[...] = jnp.concatenate([w[:, o_gr:o_gg], jnp.zeros((w.shape[0], LANES - GATE_RANK), F32)],
                                  axis=1).astype(BF16)
    dq_ref[...] = w[:, o_dq:o_dq + DQ].astype(BF16)
    dk_ref[...] = w[:, o_dq + DQ:o_dq + 2 * DQ].astype(BF16)
    dv_ref[...] = w[:, o_dq + 2 * DQ:].astype(BF16)


def _regroup_w_in_call(w_in):
    rb = 256
    d_in = w_in.shape[1]
    widths = (2 * GQ + 2 * GV, LANES, DQ, DQ, DV)
    return pl.pallas_call(
        _regroup_w_in_kernel,
        grid=(D_MODEL // rb,),
        in_specs=[pl.BlockSpec((rb, d_in), lambda i: (i, 0))],
        out_specs=[pl.BlockSpec((rb, n), lambda i: (i, 0)) for n in widths],
        out_shape=[jax.ShapeDtypeStruct((D_MODEL, n), BF16) for n in widths],
        compiler_params=pltpu.CompilerParams(dimension_semantics=("arbitrary",),
                                             vmem_limit_bytes=VMEM_LIMIT),
        name="regroup_w_in",
    )(w_in)


def _inproj_kernel(x_ref, mixpre_ref, wgla_ref, wgr_ref, wa2_ref, ba_ref, wdq_ref, wdk_ref, wdv_ref,
                   gq_ref, gk_ref, gv_ref, gg_ref, la_ref, dq_ref, dkf_ref, dkb_ref, dvf_ref,
                   *maybe_dvt_ref, transposed):
    t = ATT_TILE
    for s in range(x_ref.shape[0] // t):
        rows = slice(s * t, (s + 1) * t)
        h = _rms(x_ref[rows, :], mixpre_ref[...]).astype(BF16)
        dv = _dot(h, wdv_ref[...])
        _store_heads(dvf_ref, dv, s * t)
        dq = _dot(h, wdq_ref[...])
        if transposed:
            (dvt_ref,) = maybe_dvt_ref
            dvt = dv.T.astype(BF16)
            dqt = (dq * (DH_D ** -0.5 * math.log2(math.e))).T.astype(BF16)
            for hd in range(H_D):
                dq_ref[0, s, hd] = dqt[hd * 2 * DH_D:(hd + 1) * 2 * DH_D]
                dvt_ref[0, s, hd, :DV_D, :] = dvt[hd * DV_D:(hd + 1) * DV_D]
                dvt_ref[0, s, hd, DV_D:, :] = jnp.ones((VT_ROWS - DV_D, t), BF16)
        else:
            dq_ref[rows, :] = (dq * (DH_D ** -0.5)).astype(BF16)
        dk = _dot(h, wdk_ref[...])
        _store_heads(dkf_ref, dk, s * t)
        dkb_ref[rows, :] = dk.astype(BF16)
        gr = _dot(h, wgr_ref[...]).astype(BF16)
        gq_ref[rows, :] = (_dot(h, wgla_ref[:, 0:GQ]) * (DK_G ** -0.5)).astype(BF16)
        a = _dot(gr, wa2_ref[...]) + ba_ref[...]
        gk_ref[rows, :] = _dot(h, wgla_ref[:, GQ:2 * GQ]).astype(BF16)
        la_ref[rows, :] = (jnp.minimum(a, 0.0) - jnp.log1p(jnp.exp(-jnp.abs(a)))) * (math.log2(math.e) / GATE_TAU)
        gv_ref[rows, :] = _dot(h, wgla_ref[:, 2 * GQ:2 * GQ + GV]).astype(BF16)
        gg_ref[rows, :] = _dot(h, wgla_ref[:, 2 * GQ + GV:2 * GQ + 2 * GV]).astype(BF16)


def _inproj_call(x, mixpre_g, w, seq_len, transposed):
    m = x.shape[0]
    tm = min(TOKEN_TILE, m)
    nb = m // seq_len
    row = lambda wd: pl.BlockSpec((tm, wd), lambda i: (i, 0))
    heads = pl.BlockSpec((tm * H_D, LANES), lambda i: (i, 0))
    out_shape = [jax.ShapeDtypeStruct((m, GQ), BF16), jax.ShapeDtypeStruct((m, GQ), BF16),
                 jax.ShapeDtypeStruct((m, GV), BF16), jax.ShapeDtypeStruct((m, GV), BF16),
                 jax.ShapeDtypeStruct((m, GQ), F32)]
    out_specs = [row(GQ), row(GQ), row(GV), row(GV), row(GQ)]
    if transposed:
        t = ATT_TILE
        nq, per_step = seq_len // t, tm // t
        spb = nq // per_step
        out_shape.append(jax.ShapeDtypeStruct((nb, nq, H_D, 2 * DH_D, t), BF16))
        out_specs.append(pl.BlockSpec((1, per_step, H_D, 2 * DH_D, t), lambda i: (i // spb, i % spb, 0, 0, 0)))
    else:
        out_shape.append(jax.ShapeDtypeStruct((m, DQ), BF16))
        out_specs.append(row(DQ))
    out_shape += [jax.ShapeDtypeStruct((m * H_D, LANES), F32), jax.ShapeDtypeStruct((m, DQ), BF16),
                  jax.ShapeDtypeStruct((m * H_D, LANES), F32)]
    out_specs += [heads, row(DQ), heads]
    if transposed:
        out_shape.append(jax.ShapeDtypeStruct((nb, nq, H_D, VT_ROWS, t), BF16))
        out_specs.append(pl.BlockSpec((1, per_step, H_D, VT_ROWS, t), lambda i: (i // spb, i % spb, 0, 0, 0)))
    in_specs = [row(D_MODEL), _const_spec((1, D_MODEL)), _const_spec((D_MODEL, 2 * GQ + 2 * GV)),
                _const_spec((D_MODEL, LANES)), _const_spec((LANES, GQ)), _const_spec((1, GQ)),
                _const_spec((D_MODEL, DQ)), _const_spec((D_MODEL, DQ)), _const_spec((D_MODEL, DV))]
    return pl.pallas_call(
        functools.partial(_inproj_kernel, transposed=transposed),
        grid=(m // tm,),
        in_specs=in_specs,
        out_specs=out_specs,
        out_shape=out_shape,
        compiler_params=pltpu.CompilerParams(dimension_semantics=("arbitrary",),
                                             vmem_limit_bytes=VMEM_LIMIT),
        name="inproj_t" if transposed else "inproj",
    )(x, mixpre_g, w["gla"], w["gr"], w["a2"], w["ba"], w["dq"], w["dk"], w["dv"])


def _cumsum_rows(tril, x):
    hi = x.astype(BF16)
    r = x - hi.astype(F32)
    mid = r.astype(BF16)
    lo = (r - mid.astype(F32)).astype(BF16)
    return _dot(tril, hi) + _dot(tril, mid) + _dot(tril, lo)


def _gla_exact_scores(q_ref, k_ref, b_ref, a_ref, bi, c, slot, L):
    rows = slice(c * L, (c + 1) * L)
    q = q_ref[bi, rows, :].astype(F32)
    k = k_ref[bi, rows, :].astype(F32)
    b = b_ref[slot]
    row_id = lax.broadcasted_iota(jnp.int32, (L, GQ), 0)
    col_id = lax.broadcasted_iota(jnp.int32, (16, L), 1)
    head_of_lane = lax.broadcasted_iota(jnp.int32, (16, GQ), 1) >> int(math.log2(DK_G))
    head_ind = (head_of_lane == lax.broadcasted_iota(jnp.int32, (16, GQ), 0)).astype(BF16)

    def one_row(t, carry):
        sel = row_id == t
        qt = jnp.sum(jnp.where(sel, q, 0.0), axis=0, keepdims=True)
        bt = jnp.sum(jnp.where(sel, b, 0.0), axis=0, keepdims=True)
        prod = qt * k * jnp.exp2(jnp.minimum(bt - b, 0.0))
        hi = prod.astype(BF16)
        lo = (prod - hi.astype(F32)).astype(BF16)
        per_head = _dot_nt(head_ind, hi) + _dot_nt(head_ind, lo)
        per_head = jnp.where(col_id <= t, per_head, 0.0)
        for h in range(H_G):
            a_ref[slot, pl.ds(h * L + t, 1), :] = per_head[h:h + 1, :]
        return carry

    lax.fori_loop(0, L, one_row, 0)


def _gla_kernel(q_ref, k_ref, v_ref, la_ref, gg_ref, g_ref, s0_ref, out_ref, sfin_ref,
                s_ref, a_ref, b_ref, *, L, nchunks, nbatch):
    i = pl.program_id(1)

    @pl.when(i == 0)
    def _():
        s_ref[...] = s0_ref[...]

    causal = (lax.broadcasted_iota(jnp.int32, (H_G * L, L), 1)
              <= (lax.broadcasted_iota(jnp.int32, (H_G * L, L), 0) & (L - 1)))
    lane_head = lax.broadcasted_iota(jnp.int32, (L, GQ), 1) >> int(math.log2(DK_G))
    tr = lax.broadcasted_iota(jnp.int32, (L, L), 0)
    tc = lax.broadcasted_iota(jnp.int32, (L, L), 1)
    tril = (tc <= tr).astype(BF16)

    def stack_heads(x):
        return jnp.concatenate([jnp.where(lane_head == h, x, 0.0) for h in range(H_G)], axis=0).astype(BF16)

    work = [(c, bi, bi * nchunks + c) for c in range(nchunks) for bi in range(nbatch)]
    for c, bi, slot in work:
        rows = slice(c * L, (c + 1) * L)
        b = _cumsum_rows(tril, la_ref[bi, rows, :])
        b_ref[slot] = b
        b_mid = b[L // 2 - 1:L // 2, :]
        q_mid = stack_heads(q_ref[bi, rows, :].astype(F32) * jnp.exp2(b - b_mid))
        k_mid = (k_ref[bi, rows, :].astype(F32) * jnp.exp2(b_mid - b)).astype(BF16)
        a_ref[slot] = jnp.where(causal, _dot_nt(q_mid, k_mid), 0.0)

    @pl.when(jnp.min(la_ref[...]) < -(2.0 * GLA_EXP_RANGE / L))
    def _():
        for c, bi, slot in work:
            _gla_exact_scores(q_ref, k_ref, b_ref, a_ref, bi, c, slot, L)

    states = [s_ref[bi] for bi in range(nbatch)]
    for c, bi, slot in work:
        rows = slice(c * L, (c + 1) * L)
        s = states[bi]
        b = b_ref[slot]
        b_end = b[L - 1:L, :]
        q = q_ref[bi, rows, :].astype(F32)
        k = k_ref[bi, rows, :].astype(F32)
        v = v_ref[bi, rows, :]
        q_abs = stack_heads(q * jnp.exp2(b))
        k_end = k * jnp.exp2(b_end - b)
        decay = jnp.exp2(b_end)
        a = a_ref[slot].astype(BF16)
        o_state = _dot(q_abs, s.astype(BF16))

        kt = jnp.concatenate([k_end, jnp.broadcast_to(decay, (8, GQ)),
                              jnp.zeros((LANES - L - 8, GQ), F32)], axis=0).T
        kt_b = kt.astype(BF16)
        v_pad = jnp.concatenate([v, jnp.zeros((LANES - L, GV), BF16)], axis=0)
        decay_col = kt[:, L:L + 1]
        states[bi] = jnp.concatenate(
            [decay_col[h * DK_G:(h + 1) * DK_G] * s[h * DK_G:(h + 1) * DK_G]
             + _dot(kt_b[h * DK_G:(h + 1) * DK_G], v_pad[:, h * DV_G:(h + 1) * DV_G])
             for h in range(H_G)], axis=0)

        for h in range(H_G):
            cols = slice(h * DV_G, (h + 1) * DV_G)
            o = _dot(a[h * L:(h + 1) * L], v[:, cols]) + o_state[h * L:(h + 1) * L]
            gate = gg_ref[bi, rows, cols].astype(F32)
            out_ref[bi, rows, cols] = (_rms(o, g_ref[...]) * _silu(gate)).astype(BF16)

    for bi in range(nbatch):
        s_ref[bi] = states[bi]

    @pl.when(i == pl.num_programs(1) - 1)
    def _():
        for bi in range(nbatch):
            sfin_ref[bi] = states[bi]


def _gla_call(gq, gk, gv, la, gg, gla_g, s0, L, rows_per_step, nbatch):
    nb, seq_len = gq.shape[0], gq.shape[1]
    nchunks = rows_per_step // L
    blk = lambda w: pl.BlockSpec((nbatch, rows_per_step, w), lambda b, i: (b, i, 0))
    state = pl.BlockSpec((nbatch, GQ, DV_G), lambda b, i: (b, 0, 0))
    return pl.pallas_call(
        functools.partial(_gla_kernel, L=L, nchunks=nchunks, nbatch=nbatch),
        grid=(nb // nbatch, seq_len // rows_per_step),
        in_specs=[blk(GQ), blk(GQ), blk(GV), blk(GQ), blk(GV),
                  pl.BlockSpec((1, DV_G), lambda b, i: (0, 0)), state],
        out_specs=[blk(GV), state],
        out_shape=[jax.ShapeDtypeStruct((nb, seq_len, GV), BF16),
                   jax.ShapeDtypeStruct((nb, GQ, DV_G), F32)],
        scratch_shapes=[pltpu.VMEM((nbatch, GQ, DV_G), F32), pltpu.VMEM((nbatch * nchunks, H_G * L, L), F32),
                        pltpu.VMEM((nbatch * nchunks, L, GQ), F32)],
        compiler_params=pltpu.CompilerParams(dimension_semantics=("arbitrary", "arbitrary"),
                                             vmem_limit_bytes=VMEM_LIMIT),
        name="gla",
    )(gq, gk, gv, la, gg, gla_g, s0)


def _lambda(lq1_ref, lk1_ref, lq2_ref, lk2_ref, lam_init):
    s1 = jnp.sum(lq1_ref[...] * lk1_ref[...], axis=-1, keepdims=True)
    s2 = jnp.sum(lq2_ref[...] * lk2_ref[...], axis=-1, keepdims=True)
    return jnp.exp(s1) - jnp.exp(s2) + lam_init


def _diff_prompt_kernel(qt_ref, k_ref, vt_ref, bias_ref, lq1_ref, lk1_ref, lq2_ref, lk2_ref, g_ref, o_ref,
                        rhs_ref, m_ref, acc_ref, s_ref, *, lam_init):
    t = ATT_TILE
    i = pl.program_id(2)
    nq = pl.num_programs(2)

    chains = range(2 * ATT_HEADS)

    def half_queries(iq):
        qrow = lax.broadcasted_iota(jnp.int32, (2 * DH_D, t), 0)
        out = []
        for hh in range(ATT_HEADS):
            qt = qt_ref[0, iq, hh].astype(F32)
            out += [jnp.where((qrow >= c * DH_D) & (qrow < (c + 1) * DH_D), qt, 0.0).astype(BF16) for c in range(2)]
        return out

    def keys(j, hh):
        return k_ref[0, pl.ds(pl.multiple_of(j * t, t), t), hh * 2 * DH_D:(hh + 1) * 2 * DH_D]

    def scores(j, ch):
        return _dot(keys(j, ch // 2), rhs_ref[ch])

    def step(j, parity, diagonal):
        if diagonal:
            q_next = half_queries(jnp.minimum(i + 1, nq - 1))
        for ch in chains:
            if diagonal:
                diagonal_chain(j, parity, ch, q_next[ch])
                continue
            s_ref[1 - parity, ch] = scores(j + 1, ch)
            st = s_ref[parity, ch]
            m_old = m_ref[ch]
            m_new = jnp.maximum(m_old, jnp.max(st, axis=0, keepdims=True))
            alpha = jnp.exp2(m_old - m_new)
            p = jnp.exp2(st - m_new)
            acc_ref[ch] = alpha * acc_ref[ch] + _dot(vt_ref[0, j, ch // 2], p.astype(BF16))
            m_ref[ch] = m_new

    def diagonal_chain(j, parity, ch, q_next):
        u = t // 2
        top = s_ref[parity, ch, :u, :] + bias_ref[:u, :]
        bot = s_ref[parity, ch, u:, u:] + bias_ref[u:, u:]
        blk_max = jnp.max(top, axis=0, keepdims=True)
        blk_max = jnp.concatenate([blk_max[:, :u],
                                   jnp.maximum(blk_max[:, u:], jnp.max(bot, axis=0, keepdims=True))], axis=1)
        m_old = m_ref[ch]
        m_new = jnp.maximum(m_old, blk_max)
        alpha = jnp.exp2(m_old - m_new)
        p_top = jnp.exp2(top - m_new).astype(BF16)
        p_bot = jnp.exp2(bot - m_new[:, u:]).astype(BF16)
        s_ref[0, ch] = _dot(keys(0, ch // 2), q_next)
        vt = vt_ref[0, j, ch // 2]
        pv_bot = _dot(vt[:, u:], p_bot)
        pv = _dot(vt[:, :u], p_top)
        acc_ref[ch] = alpha * acc_ref[ch] + jnp.concatenate([pv[:, :u], pv[:, u:] + pv_bot], axis=1)
        m_ref[ch] = m_new

    def finish():
        lam = _lambda(lq1_ref, lk1_ref, lq2_ref, lk2_ref, lam_init)
        for hh in range(ATT_HEADS):
            a1, a2 = acc_ref[2 * hh], acc_ref[2 * hh + 1]
            ot = a1[:DV_D] * (1.0 / a1[DV_D:DV_D + 1]) - lam * (a2[:DV_D] * (1.0 / a2[DV_D:DV_D + 1]))
            r = lax.rsqrt(jnp.mean(ot * ot, axis=0, keepdims=True) + EPS)
            ot = ot * r * g_ref[...] * (1.0 - lam_init)
            o_ref[0, :, hh * DV_D:(hh + 1) * DV_D] = ot.T.astype(BF16)

    def pair(n, carry):
        step(2 * n, 0, False)
        step(2 * n + 1, 1, False)
        return carry

    q_now = half_queries(i)
    for ch in chains:
        rhs_ref[ch] = q_now[ch]

    @pl.when(i == 0)
    def _():
        for ch in chains:
            s_ref[0, ch] = scores(0, ch)

    m_ref[...] = jnp.full(m_ref.shape, -jnp.inf, F32)
    acc_ref[...] = jnp.zeros(acc_ref.shape, F32)
    lax.fori_loop(0, i // 2, pair, 0)

    @pl.when(i % 2 == 0)
    def _():
        step(i, 0, True)
        finish()

    @pl.when(i % 2 == 1)
    def _():
        step(i - 1, 0, False)
        step(i, 1, True)
        finish()


def _diff_prompt_call(dqt, dkb, dvt, lams, g_col, lam_init):
    nb, seq_len = dkb.shape[0], dkb.shape[1]
    t, nh = ATT_TILE, ATT_HEADS
    chunk_of = jnp.arange(t, dtype=jnp.int32) // CHUNK
    bias = jnp.where(chunk_of[:, None] <= chunk_of[None, :], 0.0, -jnp.inf).astype(F32)
    vec = pl.BlockSpec((1, DH_D), lambda b, h, i: (0, 0))
    return pl.pallas_call(
        functools.partial(_diff_prompt_kernel, lam_init=lam_init),
        grid=(nb, H_D // nh, seq_len // t),
        in_specs=[pl.BlockSpec((1, seq_len // t, nh, 2 * DH_D, t), lambda b, h, i: (b, 0, h, 0, 0)),
                  pl.BlockSpec((1, seq_len, nh * 2 * DH_D), lambda b, h, i: (b, 0, h)),
                  pl.BlockSpec((1, seq_len // t, nh, VT_ROWS, t), lambda b, h, i: (b, 0, h, 0, 0)),
                  pl.BlockSpec((t, t), lambda b, h, i: (0, 0), pipeline_mode=pl.Buffered(1)),
                  vec, vec, vec, vec,
                  pl.BlockSpec((DV_D, 1), lambda b, h, i: (0, 0))],
        out_specs=pl.BlockSpec((1, t, nh * DV_D), lambda b, h, i: (b, i, h)),
        out_shape=jax.ShapeDtypeStruct((nb, seq_len, DV), BF16),
        scratch_shapes=[pltpu.VMEM((2 * nh, 2 * DH_D, t), BF16), pltpu.VMEM((2 * nh, 1, t), F32),
                        pltpu.VMEM((2 * nh, VT_ROWS, t), F32), pltpu.VMEM((2, 2 * nh, t, t), F32)],
        compiler_params=pltpu.CompilerParams(dimension_semantics=("arbitrary", "arbitrary", "arbitrary"),
                                             vmem_limit_bytes=VMEM_LIMIT),
        name="diff_prompt",
    )(dqt, dkb, dvt, bias, *lams, g_col)


def _decode_attention(q_ref, kc_ref, vc_ref, kn_ref, vn_ref, lq1_ref, lk1_ref, lq2_ref, lk2_ref, g_ref,
                      o_ref, *, lam_init):
    T = q_ref.shape[1]
    P = kc_ref.shape[1] // H_D
    lane = lax.broadcasted_iota(jnp.int32, (T, 2 * DH_D), 1)
    new_col = lax.broadcasted_iota(jnp.int32, (2 * T, LANES), 1)
    pad = jnp.zeros((LANES - T, 2 * DH_D), BF16)
    pending = {}

    def scores(h):
        cols = slice(h * 2 * DH_D, (h + 1) * 2 * DH_D)
        q = q_ref[0, :, cols].astype(F32)
        qs = jnp.concatenate([jnp.where(lane < DH_D, q, 0.0), jnp.where(lane >= DH_D, q, 0.0)],
                             axis=0).astype(BF16)
        kc = kc_ref[0, pl.ds(h, P, stride=H_D), :].astype(BF16)
        kn = jnp.concatenate([kn_ref[0, pl.ds(h, T, stride=H_D), :].astype(BF16), pad], axis=0)
        sc = _dot_nt(qs, kc)
        sn = jnp.where(new_col < T, _dot_nt(qs, kn), -jnp.inf)
        pending[h] = (sc, sn)

    def finish(h):
        cols = slice(h * 2 * DH_D, (h + 1) * 2 * DH_D)
        sc, sn = pending.pop(h)
        vc = vc_ref[0, pl.ds(h, P, stride=H_D), :].astype(BF16)
        vn = jnp.concatenate([vn_ref[0, pl.ds(h, T, stride=H_D), :].astype(BF16), pad], axis=0)
        m = jnp.maximum(jnp.max(sc, axis=-1, keepdims=True), jnp.max(sn, axis=-1, keepdims=True))
        pc = jnp.exp(sc - m)
        pn = jnp.exp(sn - m)
        l = jnp.sum(pc, axis=-1, keepdims=True) + jnp.sum(pn, axis=-1, keepdims=True)
        o2 = (_dot(pc.astype(BF16), vc) + _dot(pn.astype(BF16), vn)) / l
        lam = _lambda(lq1_ref, lk1_ref, lq2_ref, lk2_ref, lam_init)
        o = o2[:T] - lam * o2[T:]
        o_ref[0, :, cols] = (_rms(o, g_ref[...]) * (1.0 - lam_init)).astype(BF16)

    order = [(scores, 0)] + [job for h in range(H_D - 1) for job in ((scores, h + 1), (finish, h))]
    order.append((finish, H_D - 1))
    return [functools.partial(fn, h) for fn, h in order]


def kernel(x_prompt, x_sample, state_gla, cache_diff_k, cache_diff_v, w_in, w_gate_a2, b_gate_a, gla_norm_g, lambda_q1, lambda_k1, lambda_q2, lambda_k2, diff_norm_g, w_out, mix_pre_g, mix_post_g, ffn1_pre_g, ffn1_post_g, ffn1_w_gate, ffn1_w_up, ffn1_w_down, ffn2_pre_g, ffn2_post_g, ffn2_w_gate, ffn2_w_up, ffn2_w_down):
    B, S, _ = x_prompt.shape
    Bs, T, _ = x_sample.shape
    depth = w_in.shape[0]
    P = cache_diff_k.shape[2]
    xp = x_prompt.reshape(B * S, D_MODEL)
    xs = x_sample.reshape(Bs * T, D_MODEL)
    cache_k = cache_diff_k.reshape(depth * Bs, P * H_D, LANES)
    cache_v = cache_diff_v.reshape(depth * Bs, P * H_D, LANES)
    outs = [[] for _ in range(6)]
    for l in range(depth):
        lam_init = 0.8 - 0.6 * math.exp(-0.3 * l)
        row = lambda v: v[l].reshape(1, -1)
        w = dict(zip(("gla", "gr", "dq", "dk", "dv"), _regroup_w_in_call(w_in[l])))
        w["a2"] = jnp.pad(w_gate_a2[l], ((0, LANES - GATE_RANK), (0, 0))).astype(BF16)
        w["ba"] = row(b_gate_a)
        ffn1_g = (row(ffn1_pre_g), row(ffn1_post_g))
        ffn2_g = (row(ffn2_pre_g), row(ffn2_post_g))
        wo = w_out[l].astype(BF16)
        wog, wod = wo[:GV], wo[GV:]
        lams = (row(lambda_q1), row(lambda_k1), row(lambda_q2), row(lambda_k2))
        gla_g = row(gla_norm_g)

        x1s, ffn1_w = _ffn_stream_call(xs, *ffn1_g, ffn1_w_gate[l], ffn1_w_up[l], ffn1_w_down[l])
        gq_s, gk_s, gv_s, gg_s, la_s, dq_s, dkf_s, _, dvf_s = _inproj_call(x1s, row(mix_pre_g), w, T, False)
        s3 = lambda a: a.reshape(Bs, T, a.shape[-1])
        decode_attn = (s3(dq_s), cache_k, cache_v, dkf_s.reshape(Bs, T * H_D, LANES),
                       dvf_s.reshape(Bs, T * H_D, LANES), l * Bs, lams, row(diff_norm_g), lam_init)

        x1, d_out_s = _ffn_call(xp, *ffn1_g, *ffn1_w, decode_attn=decode_attn)
        gq, gk, gv, gg, la, dqt, dkf, dkb, dvf, dvt = _inproj_call(x1, row(mix_pre_g), w, S, True)
        b3 = lambda a: a.reshape(B, S, a.shape[-1])
        g_out, s_p = _gla_call(b3(gq), b3(gk), b3(gv), b3(la), b3(gg), gla_g,
                               jnp.zeros((B, GQ, DV_G), F32), CHUNK, GLA_ROWS, GLA_PROMPT_BATCH)
        d_out = _diff_prompt_call(dqt, b3(dkb), dvt, lams, diff_norm_g[l].reshape(DV_D, 1), lam_init)

        g_out_s, s_s = _gla_call(s3(gq_s), s3(gk_s), s3(gv_s), s3(la_s), s3(gg_s), gla_g,
                                 state_gla[l].reshape(Bs, GQ, DV_G), T, T, GLA_DECODE_BATCH)
        xs, ffn2_w = _ffn_stream_call(x1s, *ffn2_g, ffn2_w_gate[l], ffn2_w_up[l], ffn2_w_down[l],
                                      merge=(g_out_s.reshape(Bs * T, GV), d_out_s.reshape(Bs * T, DV),
                                             wog, wod, row(mix_post_g)))

        xp = _ffn_call(x1, *ffn2_g, *ffn2_w, merge=(g_out.reshape(B * S, GV), d_out.reshape(B * S, DV),
                                                    wog, wod, row(mix_post_g)))
        outs[0].append(s_p.reshape(B, H_G, DK_G, DV_G))
        outs[1].append(dkf.reshape(B, S, H_D, 2 * DH_D))
        outs[2].append(dvf.reshape(B, S, H_D, DV_D))
        outs[3].append(s_s.reshape(Bs, H_G, DK_G, DV_G))
        outs[4].append(dkf_s.reshape(Bs, T, H_D, 2 * DH_D))
        outs[5].append(dvf_s.reshape(Bs, T, H_D, DV_D))

    sg_p, k_p, v_p, sg_s, k_s, v_s = (jnp.stack(o) for o in outs)
    return (xp.reshape(B, S, D_MODEL), xs.reshape(Bs, T, D_MODEL), sg_p, k_p, v_p, sg_s, k_s, v_s)
```

```python
import functools
import math

import jax
import jax.numpy as jnp
from jax import lax
from jax.experimental import pallas as pl
from jax.experimental.pallas import tpu as pltpu

F32 = jnp.float32
BF16 = jnp.bfloat16

D_MODEL = 1024
D_FF = 2816
CHUNK = 64
H_G, DK_G, DV_G = 4, 64, 128
GATE_RANK = 16
GATE_TAU = 16.0
H_D, DH_D, DV_D = 4, 64, 128
EPS = 1e-6

GQ = H_G * DK_G
GV = H_G * DV_G
DQ = H_D * 2 * DH_D
DV = H_D * DV_D

LANES = 128
BF16_SUBLANES = 16
MXU_N = 256
TOKEN_TILE = 1024
FFN_TILE = 1024
FFN_SUB = 512
FF_TILE = MXU_N
REGROUP_ROWS = 256
ATT_TILE = 512
ATT_HEADS = 4
VT_ROWS = DV_D + BF16_SUBLANES
GLA_ROWS = 256
GLA_PROMPT_BATCH = 8
GLA_DECODE_BATCH = 8
GLA_EXP_RANGE = 115.0
VMEM_LIMIT = 56 * 1024 * 1024


def _dot(a, b):
    return jnp.dot(a, b, preferred_element_type=F32)


def _dot_nt(a, b):
    return lax.dot_general(a, b, (((1,), (1,)), ((), ())), preferred_element_type=F32)


def _rms(x, g):
    r = lax.rsqrt(jnp.mean(x * x, axis=-1, keepdims=True) + EPS)
    return x * r * g


def _silu(x):
    return x * jax.nn.sigmoid(x)


def _const_spec(shape):
    nd = len(shape)
    return pl.BlockSpec(shape, lambda *_: (0,) * nd, pipeline_mode=pl.Buffered(1))


def _swiglu_half(x, pre_ref, post_ref, wg_ref, wu_ref, wd_ref, h_ref, act_ref):
    h_ref[...] = _rms(x, pre_ref[...]).astype(BF16)
    for j in range(D_FF // FF_TILE):
        cols = slice(j * FF_TILE, (j + 1) * FF_TILE)
        h = h_ref[...]
        gate = _dot(h, wg_ref[:, cols])
        up = _dot(h, wu_ref[:, cols])
        act_ref[:, cols] = (_silu(gate) * up).astype(BF16)
    f = _dot(act_ref[...], wd_ref[...])
    return x + 0.5 * _rms(f, post_ref[...])


def _sub_tiles(ref):
    return [slice(s * FFN_SUB, (s + 1) * FFN_SUB) for s in range(ref.shape[0] // FFN_SUB)]


def _ffn_kernel(x_ref, pre_ref, post_ref, wg_ref, wu_ref, wd_ref, o_ref, h_ref, act_ref):
    for s, rows in enumerate(_sub_tiles(x_ref)):
        o_ref[rows, :] = _swiglu_half(x_ref[rows, :], pre_ref, post_ref, wg_ref, wu_ref, wd_ref,
                                      h_ref.at[s], act_ref.at[s])


def _ffn_decode_attn_kernel(x_ref, pre_ref, post_ref, wg_ref, wu_ref, wd_ref, *rest, lam_init):
    att_refs, (o_ref, att_o_ref, h_ref, act_ref) = rest[:-4], rest[-4:]
    _ffn_kernel(x_ref, pre_ref, post_ref, wg_ref, wu_ref, wd_ref, o_ref, h_ref, act_ref)
    _decode_attention(*att_refs, att_o_ref, lam_init=lam_init)


def _merge_ffn_kernel(x_ref, g_ref, d_ref, wog_ref, wod_ref, mixg_ref,
                      pre_ref, post_ref, wg_ref, wu_ref, wd_ref, o_ref, h_ref, act_ref, x2_ref):
    for s, rows in enumerate(_sub_tiles(x_ref)):
        y = _dot(g_ref[rows, :], wog_ref[...]) + _dot(d_ref[rows, :], wod_ref[...])
        x2_ref[s] = x_ref[rows, :] + _rms(y, mixg_ref[...])
    for s, rows in enumerate(_sub_tiles(x_ref)):
        o_ref[rows, :] = _swiglu_half(x2_ref[s], pre_ref, post_ref, wg_ref, wu_ref, wd_ref,
                                      h_ref.at[s], act_ref.at[s])


def _ffn_call(x, pre_g, post_g, wg, wu, wd, merge=None, decode_attn=None):
    assert (merge is None) != (decode_attn is None)
    m = x.shape[0]
    tm = min(FFN_TILE, m)
    nsub = tm // FFN_SUB
    row = lambda w: pl.BlockSpec((tm, w), lambda i: (i, 0))
    ffn_specs = [_const_spec((1, D_MODEL)), _const_spec((1, D_MODEL)),
                 _const_spec((D_MODEL, D_FF)), _const_spec((D_MODEL, D_FF)), _const_spec((D_FF, D_MODEL))]
    scratch = [pltpu.VMEM((nsub, FFN_SUB, D_MODEL), BF16), pltpu.VMEM((nsub, FFN_SUB, D_FF), BF16)]
    out_specs, out_shape = row(D_MODEL), jax.ShapeDtypeStruct((m, D_MODEL), F32)
    if decode_attn is not None:
        dq, kc, vc, kn, vn, cache_base, lams, g_row, lam_init = decode_attn
        nb, T, _ = dq.shape
        assert merge is None and nb == m // tm, "one decode batch row per FFN grid step"
        new_q = pl.BlockSpec((1, T, DQ), lambda b: (b, 0, 0))
        new_kv = pl.BlockSpec((1, T * H_D, LANES), lambda b: (b, 0, 0))
        cache = pl.BlockSpec((1, kc.shape[1], LANES), lambda b: (cache_base + b, 0, 0))
        vec = pl.BlockSpec((1, DH_D), lambda b: (0, 0))
        kern = functools.partial(_ffn_decode_attn_kernel, lam_init=lam_init)
        ins = (x, pre_g, post_g, wg, wu, wd, dq, kc, vc, kn, vn, *lams, g_row)
        in_specs = ([row(D_MODEL)] + ffn_specs + [new_q, cache, cache, new_kv, new_kv, vec, vec, vec, vec,
                                                 pl.BlockSpec((1, DV_D), lambda b: (0, 0))])
        out_specs = [out_specs, new_q]
        out_shape = [out_shape, jax.ShapeDtypeStruct((nb, T, DV), BF16)]
    else:
        g_out, d_out, wog, wod, mix_g = merge
        kern, ins = _merge_ffn_kernel, (x, g_out, d_out, wog, wod, mix_g, pre_g, post_g, wg, wu, wd)
        in_specs = ([row(D_MODEL), row(GV), row(DV), _const_spec((GV, D_MODEL)),
                     _const_spec((DV, D_MODEL)), _const_spec((1, D_MODEL))] + ffn_specs)
        scratch = scratch + [pltpu.VMEM((nsub, FFN_SUB, D_MODEL), F32)]
    return pl.pallas_call(
        kern,
        grid=(m // tm,),
        in_specs=in_specs,
        out_specs=out_specs,
        out_shape=out_shape,
        scratch_shapes=scratch,
        compiler_params=pltpu.CompilerParams(dimension_semantics=("arbitrary",),
                                             vmem_limit_bytes=VMEM_LIMIT),
        name="merge_ffn" if merge is not None else "ffn_decode_attn",
    )(*ins)


def _stream_prologue(x, pre_ref, x_scr, h_ref, acc_ref):
    x_scr[...] = x
    h_ref[...] = _rms(x, pre_ref[...]).astype(BF16)
    acc_ref[...] = jnp.zeros(acc_ref.shape, F32)


def _stream_chunk(j, post_ref, wg_ref, wu_ref, wd_ref, o_ref, wgb_ref, wub_ref, wdb_ref, x_scr, h_ref, acc_ref):
    wg, wu, wd = (r[...].astype(BF16) for r in (wg_ref, wu_ref, wd_ref))
    wgb_ref[...], wub_ref[...], wdb_ref[...] = wg, wu, wd
    h = h_ref[...]
    act = (_silu(_dot(h, wg)) * _dot(h, wu)).astype(BF16)
    acc_ref[...] += _dot(act, wd)

    @pl.when(j == pl.num_programs(0) - 1)
    def _():
        o_ref[...] = x_scr[...] + 0.5 * _rms(acc_ref[...], post_ref[...])


def _ffn_stream_kernel(x_ref, pre_ref, post_ref, *rest):
    j = pl.program_id(0)

    @pl.when(j == 0)
    def _():
        _stream_prologue(x_ref[...], pre_ref, *rest[-3:])

    _stream_chunk(j, post_ref, *rest)


def _merge_ffn_stream_kernel(x_ref, g_ref, d_ref, wog_ref, wod_ref, mixg_ref, pre_ref, post_ref, *rest):
    j = pl.program_id(0)

    @pl.when(j == 0)
    def _():
        y = _dot(g_ref[...], wog_ref[...]) + _dot(d_ref[...], wod_ref[...])
        _stream_prologue(x_ref[...] + _rms(y, mixg_ref[...]), pre_ref, *rest[-3:])

    _stream_chunk(j, post_ref, *rest)


def _ffn_stream_call(x, pre_g, post_g, wg, wu, wd, merge=None):
    m = x.shape[0]
    whole = lambda w: pl.BlockSpec((m, w), lambda j: (0, 0))
    vec = pl.BlockSpec((1, D_MODEL), lambda j: (0, 0))
    cols = pl.BlockSpec((D_MODEL, FF_TILE), lambda j: (0, j))
    rows = pl.BlockSpec((FF_TILE, D_MODEL), lambda j: (j, 0))
    if merge is None:
        kern, ins, in_specs = _ffn_stream_kernel, (x, pre_g, post_g, wg, wu, wd), [whole(D_MODEL)]
    else:
        g_out, d_out, wog, wod, mix_g = merge
        kern, ins = _merge_ffn_stream_kernel, (x, g_out, d_out, wog, wod, mix_g, pre_g, post_g, wg, wu, wd)
        in_specs = [whole(D_MODEL), whole(GV), whole(DV), pl.BlockSpec((GV, D_MODEL), lambda j: (0, 0)),
                    pl.BlockSpec((DV, D_MODEL), lambda j: (0, 0)), vec]
    out, wgb, wub, wdb = pl.pallas_call(
        kern,
        grid=(D_FF // FF_TILE,),
        in_specs=in_specs + [vec, vec, cols, cols, rows],
        out_specs=[whole(D_MODEL), cols, cols, rows],
        out_shape=[jax.ShapeDtypeStruct((m, D_MODEL), F32), jax.ShapeDtypeStruct((D_MODEL, D_FF), BF16),
                   jax.ShapeDtypeStruct((D_MODEL, D_FF), BF16), jax.ShapeDtypeStruct((D_FF, D_MODEL), BF16)],
        scratch_shapes=[pltpu.VMEM((m, D_MODEL), F32), pltpu.VMEM((m, D_MODEL), BF16),
                        pltpu.VMEM((m, D_MODEL), F32)],
        compiler_params=pltpu.CompilerParams(dimension_semantics=("arbitrary",),
                                             vmem_limit_bytes=VMEM_LIMIT),
        name="merge_ffn_stream" if merge is not None else "ffn_stream",
    )(*ins)
    return out, (wgb, wub, wdb)


def _store_heads(ref, x, row0):
    rows = x.shape[0]
    for h in range(H_D):
        ref[pl.ds(row0 * H_D + h, rows, stride=H_D), :] = x[:, h * LANES:(h + 1) * LANES]


def _regroup_w_in_kernel(w_ref, gla_ref, gr_ref, dq_ref, dk_ref, dv_ref):
    o_gr = 2 * GQ + GV
    o_gg = o_gr + GATE_RANK
    o_dq = o_gg + GV
    w = w_ref[...]
    gla_ref[:, :o_gr] = w[:, :o_gr].astype(BF16)
    gla_ref[:, o_gr:] = w[:, o_gg:o_dq].astype(BF16)
    gr_ref[...] = jnp.concatenate([w[:, o_gr:o_gg], jnp.zeros((w.shape[0], LANES - GATE_RANK), F32)],
                                  axis=1).astype(BF16)
    dq_ref[...] = w[:, o_dq:o_dq + DQ].astype(BF16)
    dk_ref[...] = w[:, o_dq + DQ:o_dq + 2 * DQ].astype(BF16)
    dv_ref[...] = w[:, o_dq + 2 * DQ:].astype(BF16)


def _regroup_w_in_call(w_in):
    rb = REGROUP_ROWS
    d_in = w_in.shape[1]
    widths = (2 * GQ + 2 * GV, LANES, DQ, DQ, DV)
    return pl.pallas_call(
        _regroup_w_in_kernel,
        grid=(D_MODEL // rb,),
        in_specs=[pl.BlockSpec((rb, d_in), lambda i: (i, 0))],
        out_specs=[pl.BlockSpec((rb, n), lambda i: (i, 0)) for n in widths],
        out_shape=[jax.ShapeDtypeStruct((D_MODEL, n), BF16) for n in widths],
        compiler_params=pltpu.CompilerParams(dimension_semantics=("arbitrary",),
                                             vmem_limit_bytes=VMEM_LIMIT),
        name="regroup_w_in",
    )(w_in)


def _inproj_kernel(x_ref, mixpre_ref, wgla_ref, wgr_ref, wa2_ref, ba_ref, wdq_ref, wdk_ref, wdv_ref,
                   gq_ref, gk_ref, gv_ref, gg_ref, la_ref, dq_ref, dkf_ref, dkb_ref, dvf_ref,
                   *maybe_dvt_ref, transposed):
    t = ATT_TILE
    for s in range(x_ref.shape[0] // t):
        rows = slice(s * t, (s + 1) * t)
        h = _rms(x_ref[rows, :], mixpre_ref[...]).astype(BF16)
        dv = _dot(h, wdv_ref[...])
        _store_heads(dvf_ref, dv, s * t)
        dq = _dot(h, wdq_ref[...])
        if transposed:
            (dvt_ref,) = maybe_dvt_ref
            dvt = dv.T.astype(BF16)
            dqt = (dq * (DH_D ** -0.5 * math.log2(math.e))).T.astype(BF16)
            for hd in range(H_D):
                dq_ref[0, s, hd] = dqt[hd * 2 * DH_D:(hd + 1) * 2 * DH_D]
                dvt_ref[0, s, hd, :DV_D, :] = dvt[hd * DV_D:(hd + 1) * DV_D]
                dvt_ref[0, s, hd, DV_D:, :] = jnp.ones((VT_ROWS - DV_D, t), BF16)
        else:
            dq_ref[rows, :] = (dq * (DH_D ** -0.5)).astype(BF16)
        dk = _dot(h, wdk_ref[...])
        _store_heads(dkf_ref, dk, s * t)
        dkb_ref[rows, :] = dk.astype(BF16)
        gr = _dot(h, wgr_ref[...]).astype(BF16)
        gq_ref[rows, :] = (_dot(h, wgla_ref[:, 0:GQ]) * (DK_G ** -0.5)).astype(BF16)
        a = _dot(gr, wa2_ref[...]) + ba_ref[...]
        gk_ref[rows, :] = _dot(h, wgla_ref[:, GQ:2 * GQ]).astype(BF16)
        la_ref[rows, :] = (jnp.minimum(a, 0.0) - jnp.log1p(jnp.exp(-jnp.abs(a)))) * (math.log2(math.e) / GATE_TAU)
        gv_ref[rows, :] = _dot(h, wgla_ref[:, 2 * GQ:2 * GQ + GV]).astype(BF16)
        gg_ref[rows, :] = _dot(h, wgla_ref[:, 2 * GQ + GV:2 * GQ + 2 * GV]).astype(BF16)


def _inproj_call(x, mixpre_g, w, seq_len, transposed):
    m = x.shape[0]
    tm = min(TOKEN_TILE, m)
    nb = m // seq_len
    row = lambda wd: pl.BlockSpec((tm, wd), lambda i: (i, 0))
    heads = pl.BlockSpec((tm * H_D, LANES), lambda i: (i, 0))
    out_shape = [jax.ShapeDtypeStruct((m, GQ), BF16), jax.ShapeDtypeStruct((m, GQ), BF16),
                 jax.ShapeDtypeStruct((m, GV), BF16), jax.ShapeDtypeStruct((m, GV), BF16),
                 jax.ShapeDtypeStruct((m, GQ), F32)]
    out_specs = [row(GQ), row(GQ), row(GV), row(GV), row(GQ)]
    if transposed:
        t = ATT_TILE
        nq, per_step = seq_len // t, tm // t
        spb = nq // per_step
        out_shape.append(jax.ShapeDtypeStruct((nb, nq, H_D, 2 * DH_D, t), BF16))
        out_specs.append(pl.BlockSpec((1, per_step, H_D, 2 * DH_D, t), lambda i: (i // spb, i % spb, 0, 0, 0)))
    else:
        out_shape.append(jax.ShapeDtypeStruct((m, DQ), BF16))
        out_specs.append(row(DQ))
    out_shape += [jax.ShapeDtypeStruct((m * H_D, LANES), F32), jax.ShapeDtypeStruct((m, DQ), BF16),
                  jax.ShapeDtypeStruct((m * H_D, LANES), F32)]
    out_specs += [heads, row(DQ), heads]
    if transposed:
        out_shape.append(jax.ShapeDtypeStruct((nb, nq, H_D, VT_ROWS, t), BF16))
        out_specs.append(pl.BlockSpec((1, per_step, H_D, VT_ROWS, t), lambda i: (i // spb, i % spb, 0, 0, 0)))
    in_specs = [row(D_MODEL), _const_spec((1, D_MODEL)), _const_spec((D_MODEL, 2 * GQ + 2 * GV)),
                _const_spec((D_MODEL, LANES)), _const_spec((LANES, GQ)), _const_spec((1, GQ)),
                _const_spec((D_MODEL, DQ)), _const_spec((D_MODEL, DQ)), _const_spec((D_MODEL, DV))]
    return pl.pallas_call(
        functools.partial(_inproj_kernel, transposed=transposed),
        grid=(m // tm,),
        in_specs=in_specs,
        out_specs=out_specs,
        out_shape=out_shape,
        compiler_params=pltpu.CompilerParams(dimension_semantics=("arbitrary",),
                                             vmem_limit_bytes=VMEM_LIMIT),
        name="inproj_t" if transposed else "inproj",
    )(x, mixpre_g, w["gla"], w["gr"], w["a2"], w["ba"], w["dq"], w["dk"], w["dv"])


def _cumsum_rows(tril, x):
    hi = x.astype(BF16)
    r = x - hi.astype(F32)
    mid = r.astype(BF16)
    lo = (r - mid.astype(F32)).astype(BF16)
    return _dot(tril, hi) + _dot(tril, mid) + _dot(tril, lo)


def _gla_exact_scores(q_ref, k_ref, b_ref, a_ref, bi, c, slot, L):
    rows = slice(c * L, (c + 1) * L)
    q = q_ref[bi, rows, :].astype(F32)
    k = k_ref[bi, rows, :].astype(F32)
    b = b_ref[slot]
    row_id = lax.broadcasted_iota(jnp.int32, (L, GQ), 0)
    col_id = lax.broadcasted_iota(jnp.int32, (BF16_SUBLANES, L), 1)
    head_of_lane = lax.broadcasted_iota(jnp.int32, (BF16_SUBLANES, GQ), 1) >> int(math.log2(DK_G))
    head_ind = (head_of_lane == lax.broadcasted_iota(jnp.int32, (BF16_SUBLANES, GQ), 0)).astype(BF16)

    def one_row(t, carry):
        sel = row_id == t
        qt = jnp.sum(jnp.where(sel, q, 0.0), axis=0, keepdims=True)
        bt = jnp.sum(jnp.where(sel, b, 0.0), axis=0, keepdims=True)
        prod = qt * k * jnp.exp2(jnp.minimum(bt - b, 0.0))
        hi = prod.astype(BF16)
        lo = (prod - hi.astype(F32)).astype(BF16)
        per_head = _dot_nt(head_ind, hi) + _dot_nt(head_ind, lo)
        per_head = jnp.where(col_id <= t, per_head, 0.0)
        for h in range(H_G):
            a_ref[slot, pl.ds(h * L + t, 1), :] = per_head[h:h + 1, :]
        return carry

    lax.fori_loop(0, L, one_row, 0)


def _gla_kernel(q_ref, k_ref, v_ref, la_ref, gg_ref, g_ref, s0_ref, out_ref, sfin_ref,
                s_ref, a_ref, b_ref, *, L, nchunks, nbatch):
    i = pl.program_id(1)

    @pl.when(i == 0)
    def _():
        s_ref[...] = s0_ref[...]

    causal = (lax.broadcasted_iota(jnp.int32, (H_G * L, L), 1)
              <= (lax.broadcasted_iota(jnp.int32, (H_G * L, L), 0) & (L - 1)))
    lane_head = lax.broadcasted_iota(jnp.int32, (L, GQ), 1) >> int(math.log2(DK_G))
    tr = lax.broadcasted_iota(jnp.int32, (L, L), 0)
    tc = lax.broadcasted_iota(jnp.int32, (L, L), 1)
    tril = (tc <= tr).astype(BF16)

    def stack_heads(x):
        return jnp.concatenate([jnp.where(lane_head == h, x, 0.0) for h in range(H_G)], axis=0).astype(BF16)

    work = [(c, bi, bi * nchunks + c) for c in range(nchunks) for bi in range(nbatch)]
    for c, bi, slot in work:
        rows = slice(c * L, (c + 1) * L)
        b = _cumsum_rows(tril, la_ref[bi, rows, :])
        b_ref[slot] = b
        b_mid = b[L // 2 - 1:L // 2, :]
        q_mid = stack_heads(q_ref[bi, rows, :].astype(F32) * jnp.exp2(b - b_mid))
        k_mid = (k_ref[bi, rows, :].astype(F32) * jnp.exp2(b_mid - b)).astype(BF16)
        a_ref[slot] = jnp.where(causal, _dot_nt(q_mid, k_mid), 0.0)

    @pl.when(jnp.min(la_ref[...]) < -(2.0 * GLA_EXP_RANGE / L))
    def _():
        for c, bi, slot in work:
            _gla_exact_scores(q_ref, k_ref, b_ref, a_ref, bi, c, slot, L)

    states = [s_ref[bi] for bi in range(nbatch)]
    for c, bi, slot in work:
        rows = slice(c * L, (c + 1) * L)
        s = states[bi]
        b = b_ref[slot]
        b_end = b[L - 1:L, :]
        q = q_ref[bi, rows, :].astype(F32)
        k = k_ref[bi, rows, :].astype(F32)
        v = v_ref[bi, rows, :]
        q_abs = stack_heads(q * jnp.exp2(b))
        k_end = k * jnp.exp2(b_end - b)
        decay = jnp.exp2(b_end)
        a = a_ref[slot].astype(BF16)
        o_state = _dot(q_abs, s.astype(BF16))

        kt = jnp.concatenate([k_end, jnp.broadcast_to(decay, (8, GQ)),
                              jnp.zeros((LANES - L - 8, GQ), F32)], axis=0).T
        kt_b = kt.astype(BF16)
        v_pad = jnp.concatenate([v, jnp.zeros((LANES - L, GV), BF16)], axis=0)
        decay_col = kt[:, L:L + 1]
        states[bi] = jnp.concatenate(
            [decay_col[h * DK_G:(h + 1) * DK_G] * s[h * DK_G:(h + 1) * DK_G]
             + _dot(kt_b[h * DK_G:(h + 1) * DK_G], v_pad[:, h * DV_G:(h + 1) * DV_G])
             for h in range(H_G)], axis=0)

        for h in range(H_G):
            cols = slice(h * DV_G, (h + 1) * DV_G)
            o = _dot(a[h * L:(h + 1) * L], v[:, cols]) + o_state[h * L:(h + 1) * L]
            gate = gg_ref[bi, rows, cols].astype(F32)
            out_ref[bi, rows, cols] = (_rms(o, g_ref[...]) * _silu(gate)).astype(BF16)

    for bi in range(nbatch):
        s_ref[bi] = states[bi]

    @pl.when(i == pl.num_programs(1) - 1)
    def _():
        for bi in range(nbatch):
            sfin_ref[bi] = states[bi]


def _gla_call(gq, gk, gv, la, gg, gla_g, s0, L, rows_per_step, nbatch):
    nb, seq_len = gq.shape[0], gq.shape[1]
    nchunks = rows_per_step // L
    blk = lambda w: pl.BlockSpec((nbatch, rows_per_step, w), lambda b, i: (b, i, 0))
    state = pl.BlockSpec((nbatch, GQ, DV_G), lambda b, i: (b, 0, 0))
    return pl.pallas_call(
        functools.partial(_gla_kernel, L=L, nchunks=nchunks, nbatch=nbatch),
        grid=(nb // nbatch, seq_len // rows_per_step),
        in_specs=[blk(GQ), blk(GQ), blk(GV), blk(GQ), blk(GV),
                  pl.BlockSpec((1, DV_G), lambda b, i: (0, 0)), state],
        out_specs=[blk(GV), state],
        out_shape=[jax.ShapeDtypeStruct((nb, seq_len, GV), BF16),
                   jax.ShapeDtypeStruct((nb, GQ, DV_G), F32)],
        scratch_shapes=[pltpu.VMEM((nbatch, GQ, DV_G), F32), pltpu.VMEM((nbatch * nchunks, H_G * L, L), F32),
                        pltpu.VMEM((nbatch * nchunks, L, GQ), F32)],
        compiler_params=pltpu.CompilerParams(dimension_semantics=("arbitrary", "arbitrary"),
                                             vmem_limit_bytes=VMEM_LIMIT),
        name="gla",
    )(gq, gk, gv, la, gg, gla_g, s0)


def _lambda(lq1_ref, lk1_ref, lq2_ref, lk2_ref, lam_init):
    s1 = jnp.sum(lq1_ref[...] * lk1_ref[...], axis=-1, keepdims=True)
    s2 = jnp.sum(lq2_ref[...] * lk2_ref[...], axis=-1, keepdims=True)
    return jnp.exp(s1) - jnp.exp(s2) + lam_init


def _diff_prompt_kernel(qt_ref, k_ref, vt_ref, bias_ref, lq1_ref, lk1_ref, lq2_ref, lk2_ref, g_ref, o_ref,
                        rhs_ref, m_ref, acc_ref, s_ref, *, lam_init):
    t = ATT_TILE
    i = pl.program_id(2)
    nq = pl.num_programs(2)

    chains = range(2 * ATT_HEADS)

    def half_queries(iq):
        qrow = lax.broadcasted_iota(jnp.int32, (2 * DH_D, t), 0)
        out = []
        for hh in range(ATT_HEADS):
            qt = qt_ref[0, iq, hh].astype(F32)
            out += [jnp.where((qrow >= c * DH_D) & (qrow < (c + 1) * DH_D), qt, 0.0).astype(BF16) for c in range(2)]
        return out

    def keys(j, hh):
        return k_ref[0, pl.ds(pl.multiple_of(j * t, t), t), hh * 2 * DH_D:(hh + 1) * 2 * DH_D]

    def scores(j, ch):
        return _dot(keys(j, ch // 2), rhs_ref[ch])

    def step(j, parity, diagonal):
        if diagonal:
            q_next = half_queries(jnp.minimum(i + 1, nq - 1))
        for ch in chains:
            if diagonal:
                diagonal_chain(j, parity, ch, q_next[ch])
                continue
            s_ref[1 - parity, ch] = scores(j + 1, ch)
            st = s_ref[parity, ch]
            m_old = m_ref[ch]
            m_new = jnp.maximum(m_old, jnp.max(st, axis=0, keepdims=True))
            alpha = jnp.exp2(m_old - m_new)
            p = jnp.exp2(st - m_new)
            acc_ref[ch] = alpha * acc_ref[ch] + _dot(vt_ref[0, j, ch // 2], p.astype(BF16))
            m_ref[ch] = m_new

    def diagonal_chain(j, parity, ch, q_next):
        u = t // 2
        top = s_ref[parity, ch, :u, :] + bias_ref[:u, :]
        bot = s_ref[parity, ch, u:, u:] + bias_ref[u:, u:]
        blk_max = jnp.max(top, axis=0, keepdims=True)
        blk_max = jnp.concatenate([blk_max[:, :u],
                                   jnp.maximum(blk_max[:, u:], jnp.max(bot, axis=0, keepdims=True))], axis=1)
        m_old = m_ref[ch]
        m_new = jnp.maximum(m_old, blk_max)
        alpha = jnp.exp2(m_old - m_new)
        p_top = jnp.exp2(top - m_new).astype(BF16)
        p_bot = jnp.exp2(bot - m_new[:, u:]).astype(BF16)
        s_ref[0, ch] = _dot(keys(0, ch // 2), q_next)
        vt = vt_ref[0, j, ch // 2]
        pv_bot = _dot(vt[:, u:], p_bot)
        pv = _dot(vt[:, :u], p_top)
        acc_ref[ch] = alpha * acc_ref[ch] + jnp.concatenate([pv[:, :u], pv[:, u:] + pv_bot], axis=1)
        m_ref[ch] = m_new

    def finish():
        lam = _lambda(lq1_ref, lk1_ref, lq2_ref, lk2_ref, lam_init)
        for hh in range(ATT_HEADS):
            a1, a2 = acc_ref[2 * hh], acc_ref[2 * hh + 1]
            ot = a1[:DV_D] * (1.0 / a1[DV_D:DV_D + 1]) - lam * (a2[:DV_D] * (1.0 / a2[DV_D:DV_D + 1]))
            r = lax.rsqrt(jnp.mean(ot * ot, axis=0, keepdims=True) + EPS)
            ot = ot * r * g_ref[...] * (1.0 - lam_init)
            o_ref[0, :, hh * DV_D:(hh + 1) * DV_D] = ot.T.astype(BF16)

    def pair(n, carry):
        step(2 * n, 0, False)
        step(2 * n + 1, 1, False)
        return carry

    q_now = half_queries(i)
    for ch in chains:
        rhs_ref[ch] = q_now[ch]

    @pl.when(i == 0)
    def _():
        for ch in chains:
            s_ref[0, ch] = scores(0, ch)

    m_ref[...] = jnp.full(m_ref.shape, -jnp.inf, F32)
    acc_ref[...] = jnp.zeros(acc_ref.shape, F32)
    lax.fori_loop(0, i // 2, pair, 0)

    @pl.when(i % 2 == 0)
    def _():
        step(i, 0, True)
        finish()

    @pl.when(i % 2 == 1)
    def _():
        step(i - 1, 0, False)
        step(i, 1, True)
        finish()


def _diff_prompt_call(dqt, dkb, dvt, lams, g_col, lam_init):
    nb, seq_len = dkb.shape[0], dkb.shape[1]
    t, nh = ATT_TILE, ATT_HEADS
    chunk_of = jnp.arange(t, dtype=jnp.int32) // CHUNK
    bias = jnp.where(chunk_of[:, None] <= chunk_of[None, :], 0.0, -jnp.inf).astype(F32)
    vec = pl.BlockSpec((1, DH_D), lambda b, h, i: (0, 0))
    return pl.pallas_call(
        functools.partial(_diff_prompt_kernel, lam_init=lam_init),
        grid=(nb, H_D // nh, seq_len // t),
        in_specs=[pl.BlockSpec((1, seq_len // t, nh, 2 * DH_D, t), lambda b, h, i: (b, 0, h, 0, 0)),
                  pl.BlockSpec((1, seq_len, nh * 2 * DH_D), lambda b, h, i: (b, 0, h)),
                  pl.BlockSpec((1, seq_len // t, nh, VT_ROWS, t), lambda b, h, i: (b, 0, h, 0, 0)),
                  pl.BlockSpec((t, t), lambda b, h, i: (0, 0), pipeline_mode=pl.Buffered(1)),
                  vec, vec, vec, vec,
                  pl.BlockSpec((DV_D, 1), lambda b, h, i: (0, 0))],
        out_specs=pl.BlockSpec((1, t, nh * DV_D), lambda b, h, i: (b, i, h)),
        out_shape=jax.ShapeDtypeStruct((nb, seq_len, DV), BF16),
        scratch_shapes=[pltpu.VMEM((2 * nh, 2 * DH_D, t), BF16), pltpu.VMEM((2 * nh, 1, t), F32),
                        pltpu.VMEM((2 * nh, VT_ROWS, t), F32), pltpu.VMEM((2, 2 * nh, t, t), F32)],
        compiler_params=pltpu.CompilerParams(dimension_semantics=("arbitrary", "arbitrary", "arbitrary"),
                                             vmem_limit_bytes=VMEM_LIMIT),
        name="diff_prompt",
    )(dqt, dkb, dvt, bias, *lams, g_col)


def _decode_attention(q_ref, kc_ref, vc_ref, kn_ref, vn_ref, lq1_ref, lk1_ref, lq2_ref, lk2_ref, g_ref,
                      o_ref, *, lam_init):
    T = q_ref.shape[1]
    P = kc_ref.shape[1] // H_D
    lam = _lambda(lq1_ref, lk1_ref, lq2_ref, lk2_ref, lam_init)
    lane = lax.broadcasted_iota(jnp.int32, (T, 2 * DH_D), 1)
    new_col = lax.broadcasted_iota(jnp.int32, (2 * T, LANES), 1)
    pad = jnp.zeros((LANES - T, 2 * DH_D), BF16)
    for h in range(H_D):
        cols = slice(h * 2 * DH_D, (h + 1) * 2 * DH_D)
        q = q_ref[0, :, cols].astype(F32)
        qs = jnp.concatenate([jnp.where(lane < DH_D, q, 0.0), jnp.where(lane >= DH_D, q, 0.0)],
                             axis=0).astype(BF16)
        kc = kc_ref[0, pl.ds(h, P, stride=H_D), :].astype(BF16)
        vc = vc_ref[0, pl.ds(h, P, stride=H_D), :].astype(BF16)
        kn = jnp.concatenate([kn_ref[0, pl.ds(h, T, stride=H_D), :].astype(BF16), pad], axis=0)
        vn = jnp.concatenate([vn_ref[0, pl.ds(h, T, stride=H_D), :].astype(BF16), pad], axis=0)
        sc = _dot_nt(qs, kc)
        sn = jnp.where(new_col < T, _dot_nt(qs, kn), -jnp.inf)
        m = jnp.maximum(jnp.max(sc, axis=-1, keepdims=True), jnp.max(sn, axis=-1, keepdims=True))
        pc = jnp.exp(sc - m)
        pn = jnp.exp(sn - m)
        l = jnp.sum(pc, axis=-1, keepdims=True) + jnp.sum(pn, axis=-1, keepdims=True)
        o2 = (_dot(pc.astype(BF16), vc) + _dot(pn.astype(BF16), vn)) / l
        o = o2[:T] - lam * o2[T:]
        o_ref[0, :, cols] = (_rms(o, g_ref[...]) * (1.0 - lam_init)).astype(BF16)


def kernel(x_prompt, x_sample, state_gla, cache_diff_k, cache_diff_v, w_in, w_gate_a2, b_gate_a, gla_norm_g, lambda_q1, lambda_k1, lambda_q2, lambda_k2, diff_norm_g, w_out, mix_pre_g, mix_post_g, ffn1_pre_g, ffn1_post_g, ffn1_w_gate, ffn1_w_up, ffn1_w_down, ffn2_pre_g, ffn2_post_g, ffn2_w_gate, ffn2_w_up, ffn2_w_down):
    B, S, _ = x_prompt.shape
    Bs, T, _ = x_sample.shape
    depth = w_in.shape[0]
    P = cache_diff_k.shape[2]
    xp = x_prompt.reshape(B * S, D_MODEL)
    xs = x_sample.reshape(Bs * T, D_MODEL)
    cache_k = cache_diff_k.reshape(depth * Bs, P * H_D, LANES)
    cache_v = cache_diff_v.reshape(depth * Bs, P * H_D, LANES)
    outs = [[] for _ in range(6)]
    for l in range(depth):
        lam_init = 0.8 - 0.6 * math.exp(-0.3 * l)
        row = lambda v: v[l].reshape(1, -1)
        w = dict(zip(("gla", "gr", "dq", "dk", "dv"), _regroup_w_in_call(w_in[l])))
        w["a2"] = jnp.pad(w_gate_a2[l], ((0, LANES - GATE_RANK), (0, 0))).astype(BF16)
        w["ba"] = row(b_gate_a)
        ffn1_g = (row(ffn1_pre_g), row(ffn1_post_g))
        ffn2_g = (row(ffn2_pre_g), row(ffn2_post_g))
        wo = w_out[l].astype(BF16)
        wog, wod = wo[:GV], wo[GV:]
        lams = (row(lambda_q1), row(lambda_k1), row(lambda_q2), row(lambda_k2))
        gla_g = row(gla_norm_g)

        x1s, ffn1_w = _ffn_stream_call(xs, *ffn1_g, ffn1_w_gate[l], ffn1_w_up[l], ffn1_w_down[l])
        gq_s, gk_s, gv_s, gg_s, la_s, dq_s, dkf_s, _, dvf_s = _inproj_call(x1s, row(mix_pre_g), w, T, False)
        s3 = lambda a: a.reshape(Bs, T, a.shape[-1])
        decode_attn = (s3(dq_s), cache_k, cache_v, dkf_s.reshape(Bs, T * H_D, LANES),
                       dvf_s.reshape(Bs, T * H_D, LANES), l * Bs, lams, row(diff_norm_g), lam_init)

        x1, d_out_s = _ffn_call(xp, *ffn1_g, *ffn1_w, decode_attn=decode_attn)
        gq, gk, gv, gg, la, dqt, dkf, dkb, dvf, dvt = _inproj_call(x1, row(mix_pre_g), w, S, True)
        b3 = lambda a: a.reshape(B, S, a.shape[-1])
        g_out, s_p = _gla_call(b3(gq), b3(gk), b3(gv), b3(la), b3(gg), gla_g,
                               jnp.zeros((B, GQ, DV_G), F32), CHUNK, GLA_ROWS, GLA_PROMPT_BATCH)
        d_out = _diff_prompt_call(dqt, b3(dkb), dvt, lams, diff_norm_g[l].reshape(DV_D, 1), lam_init)

        g_out_s, s_s = _gla_call(s3(gq_s), s3(gk_s), s3(gv_s), s3(la_s), s3(gg_s), gla_g,
                                 state_gla[l].reshape(Bs, GQ, DV_G), T, T, GLA_DECODE_BATCH)
        xs, ffn2_w = _ffn_stream_call(x1s, *ffn2_g, ffn2_w_gate[l], ffn2_w_up[l], ffn2_w_down[l],
                                      merge=(g_out_s.reshape(Bs * T, GV), d_out_s.reshape(Bs * T, DV),
                                             wog, wod, row(mix_post_g)))

        xp = _ffn_call(x1, *ffn2_g, *ffn2_w, merge=(g_out.reshape(B * S, GV), d_out.reshape(B * S, DV),
                                                    wog, wod, row(mix_post_g)))
        outs[0].append(s_p.reshape(B, H_G, DK_G, DV_G))
        outs[1].append(dkf.reshape(B, S, H_D, 2 * DH_D))
        outs[2].append(dvf.reshape(B, S, H_D, DV_D))
        outs[3].append(s_s.reshape(Bs, H_G, DK_G, DV_G))
        outs[4].append(dkf_s.reshape(Bs, T, H_D, 2 * DH_D))
        outs[5].append(dvf_s.reshape(Bs, T, H_D, DV_D))

    sg_p, k_p, v_p, sg_s, k_s, v_s = (jnp.stack(o) for o in outs)
    return (xp.reshape(B, S, D_MODEL), xs.reshape(Bs, T, D_MODEL), sg_p, k_p, v_p, sg_s, k_s, v_s)
```
